```python
import math
import jax, jax.numpy as jnp
from jax import lax
import numpy as np

D_MODEL = 1024
BATCH = 32
SEQ = 256
DEPTH = 2
DEC_BATCH = 8
DEC_SEQ = 2048
PAST_LEN = 256

GRID_W = 64
N_MIXERS = 4
D_GROUP = D_MODEL // N_MIXERS
N_ATT_HEADS = 4
V_DIM = D_GROUP // N_ATT_HEADS
DK = V_DIM // 2
ROPE_BASE = 10000.0
QBLOCK = 128
CONV_B_W = 3
CONV_C_W = 4
RG_BLOCKS = 4
RG_BW = D_GROUP // RG_BLOCKS
RG_C = 8.0
POOL_WINDOWS = (2, 4, 8, 16)
POOL_GW = D_GROUP // len(POOL_WINDOWS)
D_FF = 4 * D_MODEL
N_MOD = 6
N_IN_SPLITS = 9
D_IN = N_IN_SPLITS * D_GROUP
EPS = 1e-6

kernel_name = "hybrid_diffusion_parallel_heads_step"


def rmsnorm(x, w):
    xf = x.astype(jnp.float32)
    y = xf * lax.rsqrt(jnp.mean(xf * xf, axis=-1, keepdims=True) + EPS)
    return (y * w.astype(jnp.float32)).astype(x.dtype)


def rope_1d(x, pos):
    half = x.shape[-1] // 2
    freqs = ROPE_BASE ** (-jnp.arange(half, dtype=jnp.float32) / half)
    ang = pos.astype(jnp.float32)[:, None] * freqs[None, :]
    cos, sin = jnp.cos(ang), jnp.sin(ang)
    xf = x.astype(jnp.float32)
    x1, x2 = xf[..., :half], xf[..., half:]
    return jnp.concatenate([x1 * cos - x2 * sin, x2 * cos + x1 * sin], axis=-1).astype(x.dtype)


def rope_2d(x, rows, cols):
    h = x.shape[-1] // 2
    return jnp.concatenate([rope_1d(x[..., :h], rows), rope_1d(x[..., h:], cols)], axis=-1)


def rope_heads(t, rows, cols):
    return jnp.concatenate([rope_2d(t[..., :DK], rows, cols), rope_2d(t[..., DK:], rows, cols)], axis=-1)


def depthwise_conv(x, w, b, left):
    k_w = w.shape[0]
    t_len = x.shape[1]
    xp = jnp.pad(x, ((0, 0), (left, k_w - 1 - left), (0, 0)))
    out = xp[:, 0:t_len] * w[0]
    for j in range(1, k_w):
        out = out + xp[:, j:j + t_len] * w[j]
    if b is not None:
        out = out + b
    return out


def diff_attn(q, k, v, lam, lam_init, subln_w):
    bsz, n_h, t_len, _ = q.shape
    nb = t_len // QBLOCK
    qb = q.reshape(bsz, n_h, nb, QBLOCK, 2 * DK).transpose(2, 0, 1, 3, 4)
    k1, k2 = k[..., :DK], k[..., DK:]
    scale = DK ** -0.5

    def block(qi):
        s1 = jnp.einsum('bhqd,bhkd->bhqk', qi[..., :DK], k1).astype(jnp.float32) * scale
        s2 = jnp.einsum('bhqd,bhkd->bhqk', qi[..., DK:], k2).astype(jnp.float32) * scale
        p = jax.nn.softmax(s1, axis=-1) - lam * jax.nn.softmax(s2, axis=-1)
        return jnp.einsum('bhqk,bhkd->bhqd', p.astype(v.dtype), v)

    o = lax.map(block, qb)
    o = o.transpose(1, 2, 0, 3, 4).reshape(bsz, n_h, t_len, V_DIM)
    o = rmsnorm(o, subln_w) * (1.0 - lam_init)
    return o.transpose(0, 2, 1, 3).reshape(bsz, t_len, D_GROUP)


def _scan_combine(e1, e2):
    a1, b1 = e1
    a2, b2 = e2
    return (a1 * a2, a2 * b1 + b2)


def rglru(xc, w_g, b_g, lam, h0, reverse):
    bsz, t_len, ch = xc.shape
    xb = xc.reshape(bsz, t_len, RG_BLOCKS, RG_BW)
    g = jnp.einsum('btnc,kncd->kbtnd', xb, w_g).reshape(2, bsz, t_len, ch) + b_g[:, None, None, :]
    g = g.astype(jnp.float32)
    r = jax.nn.sigmoid(g[0])
    i = jax.nn.sigmoid(g[1])
    log_a = -RG_C * r * jax.nn.softplus(-lam.astype(jnp.float32))
    a = jnp.exp(log_a)
    b = jnp.sqrt(-jnp.expm1(2.0 * log_a)) * i * xc.astype(jnp.float32)
    if h0 is not None:
        idx = t_len - 1 if reverse else 0
        b = b.at[:, idx].add(a[:, idx] * h0.astype(jnp.float32))
    _, h = lax.associative_scan(_scan_combine, (a, b), reverse=reverse, axis=1)
    last = h[:, 0] if reverse else h[:, -1]
    return h.astype(xc.dtype), last.astype(xc.dtype)


def pool_mixer(x, w, scale):
    bsz, t_len, ch = x.shape
    xf = x.astype(jnp.float32)
    cs = jnp.concatenate([jnp.zeros((bsz, 1, ch), jnp.float32), jnp.cumsum(xf, axis=1)], axis=1)
    t = jnp.arange(t_len)
    outs = []
    for g, win in enumerate(POOL_WINDOWS):
        left = win // 2
        right = win - 1 - left
        lo = jnp.clip(t - left, 0, t_len - 1)
        hi = jnp.clip(t + right, 0, t_len - 1)
        csg = cs[..., g * POOL_GW:(g + 1) * POOL_GW]
        s = jnp.take(csg, hi + 1, axis=1) - jnp.take(csg, lo, axis=1)
        cnt = (hi - lo + 1).astype(jnp.float32)[None, :, None]
        outs.append(s / cnt - xf[..., g * POOL_GW:(g + 1) * POOL_GW])
    p = jnp.stack(outs, axis=2).astype(x.dtype)
    y = jnp.einsum('btgc,gcd->btgd', p, w).reshape(bsz, t_len, ch)
    return y * scale


def mixer(h, lp, lam_init, pos, kv_ctx, h0):
    bsz, t_len, _ = h.shape
    z = h @ lp['w_in']
    q, k, v, gb, gc, xb, xr, gr, xp = jnp.split(z, N_IN_SPLITS, axis=-1)

    def heads(t):
        return t.reshape(bsz, t_len, N_ATT_HEADS, V_DIM).transpose(0, 2, 1, 3)

    q, k, v = heads(q), heads(k), heads(v)
    if pos is not None:
        rows, cols = pos
        q = rope_heads(q, rows, cols)
        k = rope_heads(k, rows, cols)
    new_k, new_v = k, v
    if kv_ctx is not None:
        k = jnp.concatenate([kv_ctx[0], k], axis=2)
        v = jnp.concatenate([kv_ctx[1], v], axis=2)
    dl = lp['diff_lambda'].astype(jnp.float32)
    lam = jnp.exp(jnp.sum(dl[0] * dl[1])) - jnp.exp(jnp.sum(dl[2] * dl[3])) + lam_init
    y_a = diff_attn(q, k, v, lam, lam_init, lp['subln_w'])

    y_b = gb * depthwise_conv(gc * xb, lp['conv_b_w'], None, CONV_B_W // 2)

    xc = depthwise_conv(xr, lp['conv_c_w'], lp['conv_c_b'], CONV_C_W // 2)
    h0f = None if h0 is None else h0[:, 0]
    h0b = None if h0 is None else h0[:, 1]
    hf, hf_last = rglru(xc, lp['rg_w'][0], lp['rg_b'][0], lp['rg_lambda'][0], h0f, False)
    hb, hb_last = rglru(xc, lp['rg_w'][1], lp['rg_b'][1], lp['rg_lambda'][1], h0b, True)
    y_c = (hf + hb) * jax.nn.gelu(gr)

    y_d = pool_mixer(xp, lp['pool_w'], lp['pool_scale'])

    y = jnp.concatenate([y_a, y_b, y_c, y_d], axis=-1) @ lp['w_out']
    return y, new_k, new_v, jnp.stack([hf_last, hb_last], axis=1)


def layer(x, mod, lp, lam_init, pos, kv_ctx, h0):
    shift1, scale1, gate1, shift2, scale2, gate2 = jnp.split(mod, N_MOD, axis=-1)
    hn = rmsnorm(x, lp['norm_w'][0]) * (1.0 + scale1) + shift1
    y, k, v, hs = mixer(hn, lp, lam_init, pos, kv_ctx, h0)
    x = x + gate1 * y
    hn = rmsnorm(x, lp['norm_w'][1]) * (1.0 + scale2) + shift2
    x = x + gate2 * (jnp.square(jax.nn.relu(hn @ lp['w_mlp1'])) @ lp['w_mlp2'])
    return x, k, v, hs


def setup_inputs(seed: int = 0) -> dict:
    key = jax.random.key(seed)
    ks = jax.random.split(key, 26)
    f32 = jnp.float32
    nrm = lambda k, shape, s: (jax.random.normal(k, shape, f32) * s)
    a_init = jax.random.uniform(ks[20], (DEPTH, 2, D_GROUP), f32, 0.9, 0.999)
    return {
        'x_prompt': nrm(ks[0], (BATCH, SEQ, D_MODEL), 1.0),
        'x_sample': nrm(ks[1], (DEC_BATCH, DEC_SEQ, D_MODEL), 1.0),
        'cache_k': nrm(ks[2], (DEC_BATCH, DEPTH, N_ATT_HEADS, PAST_LEN, 2 * DK), 1.0),
        'cache_v': nrm(ks[3], (DEC_BATCH, DEPTH, N_ATT_HEADS, PAST_LEN, V_DIM), 1.0),
        'state_rglru': nrm(ks[4], (DEC_BATCH, DEPTH, 2, D_GROUP), 0.5),
        'c': nrm(ks[5], (DEC_BATCH, D_MODEL), 1.0),
        'c_ctx': nrm(ks[6], (D_MODEL,), 1.0),
        'w_ada': nrm(ks[7], (DEPTH, D_MODEL, N_MOD * D_MODEL), 0.5 * D_MODEL ** -0.5),
        'b_ada': nrm(ks[8], (DEPTH, N_MOD * D_MODEL), 0.02),
        'norm_w': 1.0 + nrm(ks[9], (DEPTH, 2, D_MODEL), 0.02),
        'w_in': nrm(ks[10], (DEPTH, D_MODEL, D_IN), D_MODEL ** -0.5),
        'diff_lambda': nrm(ks[11], (DEPTH, 4, DK), 0.1),
        'subln_w': 1.0 + nrm(ks[12], (DEPTH, V_DIM), 0.02),
        'conv_b_w': nrm(ks[13], (DEPTH, CONV_B_W, D_GROUP), CONV_B_W ** -0.5),
        'conv_c_w': nrm(ks[14], (DEPTH, CONV_C_W, D_GROUP), CONV_C_W ** -0.5),
        'conv_c_b': nrm(ks[15], (DEPTH, D_GROUP), 0.02),
        'rg_w': nrm(ks[16], (DEPTH, 2, 2, RG_BLOCKS, RG_BW, RG_BW), RG_BW ** -0.5),
        'rg_b': nrm(ks[17], (DEPTH, 2, 2, D_GROUP), 0.02),
        'rg_lambda': jnp.log(a_init) - jnp.log1p(-a_init),
        'pool_w': nrm(ks[18], (DEPTH, len(POOL_WINDOWS), POOL_GW, POOL_GW), POOL_GW ** -0.5),
        'pool_scale': 1.0 + nrm(ks[19], (DEPTH, D_GROUP), 0.1),
        'w_out': nrm(ks[21], (DEPTH, D_MODEL, D_MODEL), D_MODEL ** -0.5),
        'w_mlp1': nrm(ks[22], (DEPTH, D_MODEL, D_FF), D_MODEL ** -0.5),
        'w_mlp2': nrm(ks[23], (DEPTH, D_FF, D_MODEL), D_FF ** -0.5),
        'final_norm_w': 1.0 + nrm(ks[24], (D_MODEL,), 0.02),
    }


def reference(x_prompt, x_sample, cache_k, cache_v, state_rglru, c, c_ctx, w_ada, b_ada, norm_w,
              w_in, diff_lambda, subln_w, conv_b_w, conv_c_w, conv_c_b, rg_w, rg_b, rg_lambda,
              pool_w, pool_scale, w_out, w_mlp1, w_mlp2, final_norm_w):
    t_lat = x_sample.shape[1]
    n_rows = t_lat // GRID_W
    rows = jnp.repeat(jnp.arange(n_rows), GRID_W)
    cols = jnp.tile(jnp.arange(GRID_W), n_rows)
    pos = (rows, cols)

    xp = x_prompt
    xs = x_sample
    ks_out, vs_out, hs_out = [], [], []
    for l in range(DEPTH):
        lp = dict(w_in=w_in[l], diff_lambda=diff_lambda[l], subln_w=subln_w[l],
                  conv_b_w=conv_b_w[l], conv_c_w=conv_c_w[l], conv_c_b=conv_c_b[l],
                  rg_w=rg_w[l], rg_b=rg_b[l], rg_lambda=rg_lambda[l], pool_w=pool_w[l],
                  pool_scale=pool_scale[l], w_out=w_out[l], norm_w=norm_w[l],
                  w_mlp1=w_mlp1[l], w_mlp2=w_mlp2[l])
        lam_init = 0.8 - 0.6 * math.exp(-0.3 * l)
        mod_ctx = (jax.nn.silu(c_ctx) @ w_ada[l] + b_ada[l])[None, None, :]
        mod_lat = (jax.nn.silu(c) @ w_ada[l] + b_ada[l])[:, None, :]
        xp, k_new, v_new, h_new = layer(xp, mod_ctx, lp, lam_init, None, None, None)
        ks_out.append(k_new)
        vs_out.append(v_new)
        hs_out.append(h_new)
        xs, _, _, _ = layer(xs, mod_lat, lp, lam_init, pos,
                            (cache_k[:, l], cache_v[:, l]), state_rglru[:, l])
    y_prompt = rmsnorm(xp, final_norm_w)
    y_sample = rmsnorm(xs, final_norm_w)
    new_cache_k = jnp.stack(ks_out, axis=1)
    new_cache_v = jnp.stack(vs_out, axis=1)
    new_state_rglru = jnp.stack(hs_out, axis=1)
    return (y_prompt, y_sample, new_cache_k, new_cache_v, new_state_rglru)
```

```python
import functools
import math

import jax
import jax.numpy as jnp
import numpy as np
from jax import lax
from jax.experimental import pallas as pl
from jax.experimental.pallas import tpu as pltpu

D_MODEL = 1024
DEPTH = 2
GRID_W = 64
D_GROUP = 256
N_ATT_HEADS = 4
V_DIM = 64
DK = 32
ROPE_BASE = 10000.0
RG_BLOCKS = 4
RG_BW = 64
RG_C = 8.0
POOL_WINDOWS = (2, 4, 8, 16)
POOL_GW = 64
D_FF = 4 * D_MODEL
N_MOD = 6
D_IN = 9 * D_GROUP
EPS = 1e-6

LANES = 128
SUBLANES = 8
HALO = 8
VMEM_LIMIT = 56 * 1024 * 1024
MOD_ROWS = 16

BF16 = jnp.bfloat16
F32 = jnp.float32


def _params(*sem):
    return pltpu.CompilerParams(dimension_semantics=sem, vmem_limit_bytes=VMEM_LIMIT)


def _resident(shape, index_map):
    return pl.BlockSpec(shape, index_map, pipeline_mode=pl.Buffered(1))


def _rms(x, w):
    ms = jnp.mean(x * x, axis=-1, keepdims=True)
    return x * lax.rsqrt(ms + EPS) * w


def _ada_kernel(c_ref, w_ref, b_ref, o_ref):
    c = c_ref[...]
    s = (c * jax.nn.sigmoid(c)).astype(BF16)
    o_ref[...] = jnp.dot(s, w_ref[...].astype(BF16), preferred_element_type=F32) + b_ref[...]


def _ada(cc, w_ada, b_ada):
    tn = 1536
    n_out = N_MOD * D_MODEL
    return pl.pallas_call(
        _ada_kernel,
        grid=(DEPTH, n_out // tn),
        in_specs=[
            pl.BlockSpec((MOD_ROWS, D_MODEL), lambda l, n: (0, 0)),
            pl.BlockSpec((None, D_MODEL, tn), lambda l, n: (l, 0, n)),
            pl.BlockSpec((None, 1, tn), lambda l, n: (l, 0, n)),
        ],
        out_specs=pl.BlockSpec((None, MOD_ROWS, tn), lambda l, n: (l, 0, n)),
        out_shape=jax.ShapeDtypeStruct((DEPTH, MOD_ROWS, n_out), F32),
        compiler_params=_params("arbitrary", "arbitrary"),
        name="ada_mod",
    )(cc, w_ada, b_ada.reshape(DEPTH, 1, n_out))


def _inproj_kernel(*refs, rope):
    if rope:
        x_ref, mod_ref, nw_ref, w_ref, cos_ref, sin_ref, z_ref = refs
    else:
        x_ref, mod_ref, nw_ref, w_ref, z_ref = refs
    x = x_ref[...]
    shift = mod_ref[:, 0:D_MODEL]
    scale = mod_ref[:, D_MODEL:2 * D_MODEL]
    h = _rms(x, nw_ref[...]) * (1.0 + scale) + shift
    z = jnp.dot(h.astype(BF16), w_ref[...], preferred_element_type=F32)
    if not rope:
        z_ref[...] = z
        return
    cos = cos_ref[...]
    sin = sin_ref[...]
    first_half = (lax.broadcasted_iota(jnp.int32, (1, LANES), 1) % 16) < 8
    for j in range(2 * D_GROUP // LANES):
        zc = z[:, j * LANES:(j + 1) * LANES]
        partner = jnp.where(first_half,
                            pltpu.roll(zc, LANES - 8, 1),
                            pltpu.roll(zc, 8, 1))
        z_ref[:, j * LANES:(j + 1) * LANES] = zc * cos + partner * sin
    z_ref[:, 2 * D_GROUP:] = z[:, 2 * D_GROUP:]


def _inproj(x, mod_l, mod_row, nw, w_in_l, rope_tabs, tt):
    b, t, _ = x.shape
    rope = rope_tabs is not None
    in_specs = [
        pl.BlockSpec((None, tt, D_MODEL), lambda i, j: (i, j, 0)),
        pl.BlockSpec((None, 1, N_MOD * D_MODEL), lambda i, j: (mod_row(i), 0, 0)),
        _resident((1, D_MODEL), lambda i, j: (0, 0)),
        _resident((D_MODEL, D_IN), lambda i, j: (0, 0)),
    ]
    args = [x, mod_l, nw, w_in_l]
    if rope:
        in_specs += [pl.BlockSpec((tt, LANES), lambda i, j: (j, 0))] * 2
        args += list(rope_tabs)
    return pl.pallas_call(
        functools.partial(_inproj_kernel, rope=rope),
        grid=(b, t // tt),
        in_specs=in_specs,
        out_specs=pl.BlockSpec((None, tt, D_IN), lambda i, j: (i, j, 0)),
        out_shape=jax.ShapeDtypeStruct((b, t, D_IN), F32),
        compiler_params=_params("arbitrary", "arbitrary"),
        name="inproj",
    )(*args)


def _attn_kernel(*refs, has_ctx, lam_init, t_len):
    if has_ctx:
        q_ref, k_ref, v_ref, ck_ref, cv_ref, dl_ref, sw_ref, g_ref, o_ref, kb, vm = refs
    else:
        q_ref, k_ref, v_ref, dl_ref, sw_ref, g_ref, o_ref, kb, vm = refs
    lane = lax.broadcasted_iota(jnp.int32, (1, D_GROUP), 1)
    past = ck_ref.shape[0] if has_ctx else 0

    @pl.when(pl.program_id(1) == 0)
    def _():
        if has_ctx:
            kb[0:past, :] = ck_ref[...].astype(BF16)
        kb[past:past + t_len, :] = k_ref[...].astype(BF16)
        for h in range(N_ATT_HEADS):
            hm = (lane // V_DIM) == h
            if has_ctx:
                vm[h, 0:past, :] = jnp.where(hm, cv_ref[...], 0.0).astype(BF16)
            vm[h, past:past + t_len, :] = jnp.where(hm, v_ref[...], 0.0).astype(BF16)

    dl = dl_ref[...]
    lam = (jnp.exp(jnp.sum(dl[0:1] * dl[1:2], axis=-1, keepdims=True))
           - jnp.exp(jnp.sum(dl[2:3] * dl[3:4], axis=-1, keepdims=True)) + lam_init)
    q = q_ref[...] * (DK ** -0.5)
    keys = kb[...]
    nt = (((1,), (1,)), ((), ()))
    acc = jnp.zeros(q.shape, F32)
    for h in range(N_ATT_HEADS):
        probs = []
        for m in range(2):
            qm = jnp.where((lane // DK) == (2 * h + m), q, 0.0).astype(BF16)
            s = lax.dot_general(qm, keys, nt, preferred_element_type=F32)
            e = jnp.exp(s - jnp.max(s, axis=-1, keepdims=True))
            probs.append(e * (1.0 / jnp.sum(e, axis=-1, keepdims=True)))
        p = (probs[0] - lam * probs[1]).astype(BF16)
        acc = acc + jnp.dot(p, vm[h], preferred_element_type=F32)
    ms = jnp.dot(acc * acc, g_ref[...], preferred_element_type=F32,
                 precision=lax.Precision.HIGHEST)
    o_ref[...] = acc * lax.rsqrt(ms + EPS) * sw_ref[...] * (1.0 - lam_init)


def _attn(z, ctx, dl, subw, gmat, lam_init, tq):
    b, t, _ = z.shape
    has_ctx = ctx is not None
    past = ctx[0].shape[1] if has_ctx else 0
    s_len = past + t
    in_specs = [
        pl.BlockSpec((None, tq, D_GROUP), lambda i, j: (i, j, 0)),
        pl.BlockSpec((None, t, D_GROUP), lambda i, j: (i, 0, 1)),
        pl.BlockSpec((None, t, D_GROUP), lambda i, j: (i, 0, 2)),
    ]
    args = [z, z, z]
    if has_ctx:
        in_specs += [pl.BlockSpec((None, past, D_GROUP), lambda i, j: (i, 0, 0))] * 2
        args += list(ctx)
    in_specs += [
        _resident((4, DK), lambda i, j: (0, 0)),
        _resident((1, D_GROUP), lambda i, j: (0, 0)),
        _resident((D_GROUP, D_GROUP), lambda i, j: (0, 0)),
    ]
    args += [dl, subw, gmat]
    return pl.pallas_call(
        functools.partial(_attn_kernel, has_ctx=has_ctx, lam_init=lam_init, t_len=t),
        grid=(b, t // tq),
        in_specs=in_specs,
        out_specs=pl.BlockSpec((None, tq, D_GROUP), lambda i, j: (i, j, 0)),
        out_shape=jax.ShapeDtypeStruct((b, t, D_GROUP), F32),
        scratch_shapes=[pltpu.VMEM((s_len, D_GROUP), BF16),
                        pltpu.VMEM((N_ATT_HEADS, s_len, D_GROUP), BF16)],
        compiler_params=_params("arbitrary", "arbitrary"),
        name="diff_attn",
    )(*args)


RG_CHUNK = 128


def _shift_rows(xe, k):
    n = xe.shape[0]
    y = xe if k == 0 else pltpu.roll(xe, (-k) % n, 0)
    return y[HALO:n - HALO]


def _rg_gates(xc, wg, bg, c_dir):
    g = jnp.dot(xc.astype(BF16), wg, preferred_element_type=F32) + bg
    r = jax.nn.sigmoid(g[:, :D_GROUP])
    i = jax.nn.sigmoid(g[:, D_GROUP:])
    log_a = r * c_dir
    a = jnp.exp(log_a)
    th = jnp.tanh(log_a)
    b = jnp.sqrt(-2.0 * th / (1.0 - th)) * i * xc
    return a, b


def _rg_kernel(xr_ref, gr_ref, h0_ref, cw_ref, cb_ref, wg_ref, bg_ref, lam_ref,
               y_ref, last_ref, xpad, xc_s, hf_s, *, t_len):
    r_rows = RG_CHUNK
    n_chunks = t_len // r_rows
    n_groups = r_rows // SUBLANES
    zeros_halo = jnp.zeros((HALO, D_GROUP), F32)
    xpad[0:HALO, :] = zeros_halo
    xpad[HALO + t_len:, :] = zeros_halo
    xpad[HALO:HALO + t_len, :] = xr_ref[...]

    lam = lam_ref[...]
    neg = -lam
    softplus = jnp.maximum(neg, 0.0) + jnp.log1p(jnp.exp(-jnp.abs(neg)))
    c_all = -RG_C * softplus
    row8 = lax.broadcasted_iota(jnp.int32, (r_rows, D_GROUP), 0) % SUBLANES
    cw = cw_ref[...]

    def bcast_row(v, r):
        return jnp.broadcast_to(v[r:r + 1, :], (SUBLANES, D_GROUP))

    def fwd_body(c, h):
        s = pl.multiple_of(c * r_rows, r_rows)
        xe = xpad[pl.ds(s, r_rows + 2 * HALO), :]
        xc = (cw[0:1] * _shift_rows(xe, -2) + cw[1:2] * _shift_rows(xe, -1)
              + cw[2:3] * _shift_rows(xe, 0) + cw[3:4] * _shift_rows(xe, 1) + cb_ref[...])
        xc_s[pl.ds(s, r_rows), :] = xc
        a, b = _rg_gates(xc, wg_ref[:, 0:2 * D_GROUP], bg_ref[:, 0:2 * D_GROUP], c_all[0:1])
        for d in (1, 2, 4):
            m = row8 >= d
            b = jnp.where(m, a * pltpu.roll(b, d, 0) + b, b)
            a = jnp.where(m, a * pltpu.roll(a, d, 0), a)
        for g in range(n_groups):
            lo = g * SUBLANES
            hg = b[lo:lo + SUBLANES] + a[lo:lo + SUBLANES] * h
            hf_s[pl.ds(s + lo, SUBLANES), :] = hg
            h = bcast_row(hg, SUBLANES - 1)
        return h

    h_f = lax.fori_loop(0, n_chunks, fwd_body, bcast_row(h0_ref[...], 0))
    last_ref[0:1, :] = h_f[0:1]

    def bwd_body(ci, h):
        c = n_chunks - 1 - ci
        s = pl.multiple_of(c * r_rows, r_rows)
        xc = xc_s[pl.ds(s, r_rows), :]
        a, b = _rg_gates(xc, wg_ref[:, 2 * D_GROUP:], bg_ref[:, 2 * D_GROUP:], c_all[1:2])
        for d in (1, 2, 4):
            m = row8 < SUBLANES - d
            b = jnp.where(m, a * pltpu.roll(b, r_rows - d, 0) + b, b)
            a = jnp.where(m, a * pltpu.roll(a, r_rows - d, 0), a)
        pieces = [None] * n_groups
        for g in reversed(range(n_groups)):
            lo = g * SUBLANES
            hg = b[lo:lo + SUBLANES] + a[lo:lo + SUBLANES] * h
            pieces[g] = hg
            h = bcast_row(hg, 0)
        hb = jnp.concatenate(pieces, axis=0)
        gr = gr_ref[pl.ds(s, r_rows), :]
        gelu = 0.5 * gr * (1.0 + jnp.tanh(math.sqrt(2.0 / math.pi) * (gr + 0.044715 * gr * gr * gr)))
        y_ref[pl.ds(s, r_rows), :] = (hf_s[pl.ds(s, r_rows), :] + hb) * gelu
        return h

    h_b = lax.fori_loop(0, n_chunks, bwd_body, bcast_row(h0_ref[...], 1))
    last_ref[1:2, :] = h_b[0:1]


def _rg(z, h0, cw, cb, wg, bg, lam):
    b, t, _ = z.shape
    return pl.pallas_call(
        functools.partial(_rg_kernel, t_len=t),
        grid=(b,),
        in_specs=[
            pl.BlockSpec((None, t, D_GROUP), lambda i: (i, 0, 6)),
            pl.BlockSpec((None, t, D_GROUP), lambda i: (i, 0, 7)),
            pl.BlockSpec((None, 2, D_GROUP), lambda i: (i, 0, 0)),
            _resident((4, D_GROUP), lambda i: (0, 0)),
            _resident((1, D_GROUP), lambda i: (0, 0)),
            _resident((D_GROUP, 4 * D_GROUP), lambda i: (0, 0)),
            _resident((1, 4 * D_GROUP), lambda i: (0, 0)),
            _resident((2, D_GROUP), lambda i: (0, 0)),
        ],
        out_specs=[pl.BlockSpec((None, t, D_GROUP), lambda i: (i, 0, 0)),
                   pl.BlockSpec((None, 2, D_GROUP), lambda i: (i, 0, 0))],
        out_shape=[jax.ShapeDtypeStruct((b, t, D_GROUP), F32),
                   jax.ShapeDtypeStruct((b, 2, D_GROUP), F32)],
        scratch_shapes=[pltpu.VMEM((t + 2 * HALO, D_GROUP), F32),
                        pltpu.VMEM((t, D_GROUP), F32),
                        pltpu.VMEM((t, D_GROUP), F32)],
        compiler_params=_params("arbitrary"),
        name="rglru",
    )(z, z, h0, cw, cb, wg, bg, lam)


CP_CHUNK = 128


def _convpool_kernel(gb_ref, gc_ref, xb_ref, xp_ref, cw_ref, pw_ref, ps_ref,
                     yb_ref, yd_ref, upad, ppad, *, t_len):
    r_rows = CP_CHUNK
    n_chunks = t_len // r_rows
    zeros_halo = jnp.zeros((HALO, D_GROUP), F32)
    for pad in (upad, ppad):
        pad[0:HALO, :] = zeros_halo
        pad[HALO + t_len:, :] = zeros_halo

    def fill(c, carry):
        s = pl.multiple_of(c * r_rows, r_rows)
        upad[pl.ds(HALO + s, r_rows), :] = gc_ref[pl.ds(s, r_rows), :] * xb_ref[pl.ds(s, r_rows), :]
        ppad[pl.ds(HALO + s, r_rows), :] = xp_ref[pl.ds(s, r_rows), :]
        return carry

    lax.fori_loop(0, n_chunks, fill, 0)

    cw = cw_ref[...]
    grp = lax.broadcasted_iota(jnp.int32, (r_rows, D_GROUP), 1) // POOL_GW
    win = jnp.left_shift(2, grp)
    left = win // 2
    right = win - 1 - left
    row = lax.broadcasted_iota(jnp.int32, (r_rows, D_GROUP), 0)

    def body(c, carry):
        s = pl.multiple_of(c * r_rows, r_rows)
        ue = upad[pl.ds(s, r_rows + 2 * HALO), :]
        conv = (cw[0:1] * _shift_rows(ue, -1) + cw[1:2] * _shift_rows(ue, 0)
                + cw[2:3] * _shift_rows(ue, 1))
        yb_ref[pl.ds(s, r_rows), :] = gb_ref[pl.ds(s, r_rows), :] * conv

        pe = ppad[pl.ds(s, r_rows + 2 * HALO), :]
        q2 = pe + pltpu.roll(pe, 1, 0)
        q4 = q2 + pltpu.roll(q2, 2, 0)
        q8 = q4 + pltpu.roll(q4, 4, 0)
        q16 = q8 + pltpu.roll(q8, 8, 0)
        sums = jnp.where(grp == 0, _shift_rows(q2, 0),
                         jnp.where(grp == 1, _shift_rows(q4, 1),
                                   jnp.where(grp == 2, _shift_rows(q8, 3), _shift_rows(q16, 7))))
        t = row + s
        cnt = (jnp.minimum(t + right, t_len - 1) - jnp.maximum(t - left, 0) + 1).astype(F32)
        pm = sums / cnt - _shift_rows(pe, 0)
        yd_ref[pl.ds(s, r_rows), :] = (
            jnp.dot(pm.astype(BF16), pw_ref[...], preferred_element_type=F32) * ps_ref[...])
        return carry

    lax.fori_loop(0, n_chunks, body, 0)


def _convpool(z, cw, pw, ps):
    b, t, _ = z.shape
    col = lambda k: pl.BlockSpec((None, t, D_GROUP), lambda i: (i, 0, k))
    out = pl.BlockSpec((None, t, D_GROUP), lambda i: (i, 0, 0))
    return pl.pallas_call(
        functools.partial(_convpool_kernel, t_len=t),
        grid=(b,),
        in_specs=[col(3), col(4), col(5), col(8),
                  _resident((3, D_GROUP), lambda i: (0, 0)),
                  _resident((D_GROUP, D_GROUP), lambda i: (0, 0)),
                  _resident((1, D_GROUP), lambda i: (0, 0))],
        out_specs=[out, out],
        out_shape=[jax.ShapeDtypeStruct((b, t, D_GROUP), F32)] * 2,
        scratch_shapes=[pltpu.VMEM((t + 2 * HALO, D_GROUP), F32)] * 2,
        compiler_params=_params("arbitrary"),
        name="convpool",
    )(z, z, z, z, cw, pw, ps)


FF_CHUNK = 1024


def _mlp_kernel(x_ref, ya_ref, yb_ref, yc_ref, yd_ref, mod_ref, nw_ref, wo_ref, w1_ref, w2_ref,
                fw_ref, o_ref, *, final):
    x = x_ref[...]
    y = jnp.zeros(x.shape, F32)
    for i, r in enumerate((ya_ref, yb_ref, yc_ref, yd_ref)):
        y = y + jnp.dot(r[...].astype(BF16), wo_ref[i * D_GROUP:(i + 1) * D_GROUP, :],
                        preferred_element_type=F32)
    gate1 = mod_ref[:, 2 * D_MODEL:3 * D_MODEL]
    shift2 = mod_ref[:, 3 * D_MODEL:4 * D_MODEL]
    scale2 = mod_ref[:, 4 * D_MODEL:5 * D_MODEL]
    gate2 = mod_ref[:, 5 * D_MODEL:6 * D_MODEL]
    x1 = x + gate1 * y
    hn = (_rms(x1, nw_ref[...]) * (1.0 + scale2) + shift2).astype(BF16)
    acc = jnp.zeros(x.shape, F32)
    for f in range(D_FF // FF_CHUNK):
        u = jnp.dot(hn, w1_ref[:, f * FF_CHUNK:(f + 1) * FF_CHUNK], preferred_element_type=F32)
        u = jnp.square(jnp.maximum(u, 0.0)).astype(BF16)
        acc = acc + jnp.dot(u, w2_ref[f * FF_CHUNK:(f + 1) * FF_CHUNK, :], preferred_element_type=F32)
    x2 = x1 + gate2 * acc
    if final:
        x2 = _rms(x2, fw_ref[...])
    o_ref[...] = x2


def _mlp(x, ys, mod_l, mod_row, nw, wo, w1, w2, fw, final, tt):
    b, t, _ = x.shape
    ytile = pl.BlockSpec((None, tt, D_GROUP), lambda i, j: (i, j, 0))
    return pl.pallas_call(
        functools.partial(_mlp_kernel, final=final),
        grid=(b, t // tt),
        in_specs=[
            pl.BlockSpec((None, tt, D_MODEL), lambda i, j: (i, j, 0)),
            ytile, ytile, ytile, ytile,
            pl.BlockSpec((None, 1, N_MOD * D_MODEL), lambda i, j: (mod_row(i), 0, 0)),
            _resident((1, D_MODEL), lambda i, j: (0, 0)),
            _resident((D_MODEL, D_MODEL), lambda i, j: (0, 0)),
            _resident((D_MODEL, D_FF), lambda i, j: (0, 0)),
            _resident((D_FF, D_MODEL), lambda i, j: (0, 0)),
            _resident((1, D_MODEL), lambda i, j: (0, 0)),
        ],
        out_specs=pl.BlockSpec((None, tt, D_MODEL), lambda i, j: (i, j, 0)),
        out_shape=jax.ShapeDtypeStruct((b, t, D_MODEL), F32),
        compiler_params=_params("arbitrary", "arbitrary"),
        name="outproj_mlp",
    )(x, *ys, mod_l, nw, wo, w1, w2, fw)


def _block_diag(w):
    n, k, _ = w.shape
    eye = jnp.eye(n, dtype=w.dtype)
    return (eye[:, None, :, None] * w[:, :, None, :]).reshape(n * k, n * k)


def _rope_tables(t_len):
    half = DK // 4
    freqs = ROPE_BASE ** (-jnp.arange(half, dtype=F32) / half)
    lane = np.arange(LANES)
    m = lane % DK
    use_col = m >= DK // 2
    fidx = m % half
    t = jnp.arange(t_len)
    pos = jnp.where(use_col[None, :], (t % GRID_W)[:, None], (t // GRID_W)[:, None]).astype(F32)
    ang = pos * freqs[fidx][None, :]
    sign = np.where((lane % (2 * half)) < half, -1.0, 1.0).astype(np.float32)
    return jnp.cos(ang), jnp.sin(ang) * sign[None, :]


def _heads_to_lanes(c):
    b, h, s, v = c.shape
    return c.transpose(0, 2, 1, 3).reshape(b, s, h * v)


def _lanes_to_heads(x):
    b, s, _ = x.shape
    return x.reshape(b, s, N_ATT_HEADS, V_DIM).transpose(0, 2, 1, 3)


def kernel(x_prompt, x_sample, cache_k, cache_v, state_rglru, c, c_ctx, w_ada, b_ada, norm_w, w_in, diff_lambda, subln_w, conv_b_w, conv_c_w, conv_c_b, rg_w, rg_b, rg_lambda, pool_w, pool_scale, w_out, w_mlp1, w_mlp2, final_norm_w):
    n_lat = c.shape[0]
    ctx_row = n_lat
    cc = jnp.zeros((MOD_ROWS, D_MODEL), F32).at[:n_lat].set(c).at[ctx_row].set(c_ctx)
    mod = _ada(cc, w_ada, b_ada).reshape(DEPTH, MOD_ROWS, 1, N_MOD * D_MODEL)

    w_in_b = w_in.astype(BF16)
    w_out_b = w_out.astype(BF16)
    w1_b = w_mlp1.astype(BF16)
    w2_b = w_mlp2.astype(BF16)
    rope_tabs = _rope_tables(x_sample.shape[1])
    gmat = _block_diag(jnp.full((N_ATT_HEADS, V_DIM, V_DIM), 1.0 / V_DIM, F32))
    fw = final_norm_w.reshape(1, D_MODEL)
    zero_state = jnp.zeros((x_prompt.shape[0], 2, D_GROUP), F32)

    xp, xs = x_prompt, x_sample
    ks_out, vs_out, hs_out = [], [], []
    for l in range(DEPTH):
        lam_init = 0.8 - 0.6 * math.exp(-0.3 * l)
        final = l == DEPTH - 1
        mod_l = mod[l]
        nw1 = norm_w[l, 0].reshape(1, D_MODEL)
        nw2 = norm_w[l, 1].reshape(1, D_MODEL)
        subw = jnp.tile(subln_w[l], N_ATT_HEADS).reshape(1, D_GROUP)
        wg = jnp.concatenate([_block_diag(rg_w[l, d, g]) for d in range(2) for g in range(2)],
                             axis=1).astype(BF16)
        bg = rg_b[l].reshape(1, 4 * D_GROUP)
        pw = _block_diag(pool_w[l]).astype(BF16)
        ps = pool_scale[l].reshape(1, D_GROUP)
        cb = conv_c_b[l].reshape(1, D_GROUP)

        def run(x, mod_row, rope, ctx, h0, tt_in, tq, tt_mlp):
            z = _inproj(x, mod_l, mod_row, nw1, w_in_b[l], rope, tt_in)
            ya = _attn(z, ctx, diff_lambda[l], subw, gmat, lam_init, tq)
            yc, last = _rg(z, h0, conv_c_w[l], cb, wg, bg, rg_lambda[l])
            yb, yd = _convpool(z, conv_b_w[l], pw, ps)
            x_new = _mlp(x, (ya, yb, yc, yd), mod_l, mod_row, nw2, w_out_b[l], w1_b[l], w2_b[l],
                         fw, final, tt_mlp)
            return x_new, z, last

        xp, zp, last_p = run(xp, lambda i: ctx_row, None, None, zero_state, 256, 256, 256)
        ks_out.append(_lanes_to_heads(zp[:, :, D_GROUP:2 * D_GROUP]))
        vs_out.append(_lanes_to_heads(zp[:, :, 2 * D_GROUP:3 * D_GROUP]))
        hs_out.append(last_p)
        ctx = (_heads_to_lanes(cache_k[:, l]), _heads_to_lanes(cache_v[:, l]))
        xs, _, _ = run(xs, lambda i: i, rope_tabs, ctx, state_rglru[:, l], 256, 128, 256)

    return (xp, xs, jnp.stack(ks_out, axis=1), jnp.stack(vs_out, axis=1), jnp.stack(hs_out, axis=1))
```

```python
import functools
import math

import jax
import jax.numpy as jnp
import numpy as np
from jax import lax
from jax.experimental import pallas as pl
from jax.experimental.pallas import tpu as pltpu

D_MODEL = 1024
DEPTH = 2
GRID_W = 64
D_GROUP = 256
N_ATT_HEADS = 4
V_DIM = 64
DK = 32
ROPE_BASE = 10000.0
RG_BLOCKS = 4
RG_BW = 64
RG_C = 8.0
POOL_WINDOWS = (2, 4, 8, 16)
POOL_GW = 64
D_FF = 4 * D_MODEL
N_MOD = 6
D_IN = 9 * D_GROUP
EPS = 1e-6

LANES = 128
SUBLANES = 8
HALO = 8
VMEM_LIMIT = 56 * 1024 * 1024
MOD_ROWS = 16

BF16 = jnp.bfloat16
F32 = jnp.float32


def _params(*sem):
    return pltpu.CompilerParams(dimension_semantics=sem, vmem_limit_bytes=VMEM_LIMIT)


def _resident(shape, index_map):
    return pl.BlockSpec(shape, index_map, pipeline_mode=pl.Buffered(1))


def _rms(x, w):
    ms = jnp.mean(x * x, axis=-1, keepdims=True)
    return x * lax.rsqrt(ms + EPS) * w


def _ada_kernel(c_ref, w_ref, b_ref, o_ref):
    c = c_ref[...]
    s = (c * jax.nn.sigmoid(c)).astype(BF16)
    o_ref[...] = jnp.dot(s, w_ref[...].astype(BF16), preferred_element_type=F32) + b_ref[...]


def _ada(cc, w_ada, b_ada):
    tn = 1536
    n_out = N_MOD * D_MODEL
    return pl.pallas_call(
        _ada_kernel,
        grid=(DEPTH, n_out // tn),
        in_specs=[
            pl.BlockSpec((MOD_ROWS, D_MODEL), lambda l, n: (0, 0)),
            pl.BlockSpec((None, D_MODEL, tn), lambda l, n: (l, 0, n)),
            pl.BlockSpec((None, 1, tn), lambda l, n: (l, 0, n)),
        ],
        out_specs=pl.BlockSpec((None, MOD_ROWS, tn), lambda l, n: (l, 0, n)),
        out_shape=jax.ShapeDtypeStruct((DEPTH, MOD_ROWS, n_out), F32),
        compiler_params=_params("arbitrary", "arbitrary"),
        name="ada_mod",
    )(cc, w_ada, b_ada.reshape(DEPTH, 1, n_out))


def _inproj_kernel(*refs, rope):
    if rope:
        x_ref, mod_ref, nw_ref, w_ref, cos_ref, sin_ref, z_ref = refs
    else:
        x_ref, mod_ref, nw_ref, w_ref, z_ref = refs
    x = x_ref[...]
    shift = mod_ref[:, 0:D_MODEL]
    scale = mod_ref[:, D_MODEL:2 * D_MODEL]
    h = _rms(x, nw_ref[...]) * (1.0 + scale) + shift
    z = jnp.dot(h.astype(BF16), w_ref[...], preferred_element_type=F32)
    if not rope:
        z_ref[...] = z
        return
    cos = cos_ref[...]
    sin = sin_ref[...]
    first_half = (lax.broadcasted_iota(jnp.int32, (1, LANES), 1) % 16) < 8
    for j in range(2 * D_GROUP // LANES):
        zc = z[:, j * LANES:(j + 1) * LANES]
        partner = jnp.where(first_half,
                            pltpu.roll(zc, LANES - 8, 1),
                            pltpu.roll(zc, 8, 1))
        z_ref[:, j * LANES:(j + 1) * LANES] = zc * cos + partner * sin
    z_ref[:, 2 * D_GROUP:] = z[:, 2 * D_GROUP:]


def _inproj(x, mod_l, mod_row, nw, w_in_l, rope_tabs, tt):
    b, t, _ = x.shape
    rope = rope_tabs is not None
    in_specs = [
        pl.BlockSpec((None, tt, D_MODEL), lambda i, j: (i, j, 0)),
        pl.BlockSpec((None, 1, N_MOD * D_MODEL), lambda i, j: (mod_row(i), 0, 0)),
        _resident((1, D_MODEL), lambda i, j: (0, 0)),
        _resident((D_MODEL, D_IN), lambda i, j: (0, 0)),
    ]
    args = [x, mod_l, nw, w_in_l]
    if rope:
        in_specs += [pl.BlockSpec((tt, LANES), lambda i, j: (j, 0))] * 2
        args += list(rope_tabs)
    return pl.pallas_call(
        functools.partial(_inproj_kernel, rope=rope),
        grid=(b, t // tt),
        in_specs=in_specs,
        out_specs=pl.BlockSpec((None, tt, D_IN), lambda i, j: (i, j, 0)),
        out_shape=jax.ShapeDtypeStruct((b, t, D_IN), F32),
        compiler_params=_params("arbitrary", "arbitrary"),
        name="inproj",
    )(*args)


ATT_ROWS = 128
LOG2E = math.log2(math.e)


def _attn_kernel(*refs, has_ctx, lam_init, t_len, n_steps):
    if has_ctx:
        (q_ref, qn_ref, k_ref, v_ref, ck_ref, cv_ref, dl_ref, sw_ref, g_ref, o_ref,
         kt, vs, s0, s1, pb) = refs
    else:
        q_ref, qn_ref, k_ref, v_ref, dl_ref, sw_ref, g_ref, o_ref, kt, vs, s0, s1, pb = refs
    rows = ATT_ROWS
    past = ck_ref.shape[0] if has_ctx else 0
    s_len = past + t_len
    lane = lax.broadcasted_iota(jnp.int32, (1, D_GROUP), 1)

    def scores(q_rows, dst):
        q = q_rows * (DK ** -0.5 * LOG2E)
        stack = jnp.concatenate(
            [jnp.where((lane // DK) == hm, q, 0.0).astype(BF16) for hm in range(2 * N_ATT_HEADS)],
            axis=0)
        dst[...] = jnp.dot(stack, kt[...], preferred_element_type=F32)

    @pl.when(pl.program_id(1) == 0)
    def _():
        if has_ctx:
            kt[:, 0:past] = ck_ref[...].T.astype(BF16)
        kt[:, past:s_len] = k_ref[...].T.astype(BF16)
        for h in range(N_ATT_HEADS):
            hm = (lane // V_DIM) == h
            if has_ctx:
                vs[h * s_len:h * s_len + past, :] = jnp.where(hm, cv_ref[...], 0.0).astype(BF16)
            vs[h * s_len + past:(h + 1) * s_len, :] = jnp.where(hm, v_ref[...], 0.0).astype(BF16)
        scores(q_ref[0:rows, :], s0)

    dl = dl_ref[...]
    lam = (jnp.exp(jnp.sum(dl[0:1] * dl[1:2], axis=-1, keepdims=True))
           - jnp.exp(jnp.sum(dl[2:3] * dl[3:4], axis=-1, keepdims=True)) + lam_init)

    def finish(src, out_lo):
        inv_l1 = jnp.zeros((rows, D_GROUP), F32)
        for h in range(N_ATT_HEADS):
            es, ls = [], []
            for m in range(2):
                lo = (2 * h + m) * rows
                s = src[lo:lo + rows, :]
                e = jnp.exp2(s - jnp.max(s, axis=-1, keepdims=True))
                es.append(e)
                ls.append(jnp.sum(e, axis=-1, keepdims=True))
            beta = lam * ls[0] / ls[1]
            pb[:, h * s_len:(h + 1) * s_len] = (es[0] - beta * es[1]).astype(BF16)
            inv_l1 = jnp.where((lane // V_DIM) == h, 1.0 / ls[0], inv_l1)
        acc = jnp.dot(pb[...], vs[...], preferred_element_type=F32) * inv_l1
        ms = jnp.dot(acc * acc, g_ref[...], preferred_element_type=F32,
                     precision=lax.Precision.HIGHEST)
        o_ref[out_lo:out_lo + rows, :] = acc * lax.rsqrt(ms + EPS) * sw_ref[...] * (1.0 - lam_init)

    scores(q_ref[rows:2 * rows, :], s1)
    finish(s0, 0)
    if n_steps > 1:
        scores(qn_ref[0:rows, :], s0)
    finish(s1, rows)


def _attn(z, ctx, dl, subw, gmat, lam_init):
    b, t, _ = z.shape
    has_ctx = ctx is not None
    past = ctx[0].shape[1] if has_ctx else 0
    s_len = past + t
    tq = 2 * ATT_ROWS
    n_steps = t // tq
    in_specs = [
        pl.BlockSpec((None, tq, D_GROUP), lambda i, j: (i, j, 0)),
        pl.BlockSpec((None, tq, D_GROUP), lambda i, j: (i, jnp.minimum(j + 1, n_steps - 1), 0)),
        pl.BlockSpec((None, t, D_GROUP), lambda i, j: (i, 0, 1)),
        pl.BlockSpec((None, t, D_GROUP), lambda i, j: (i, 0, 2)),
    ]
    args = [z, z, z, z]
    if has_ctx:
        in_specs += [pl.BlockSpec((None, past, D_GROUP), lambda i, j: (i, 0, 0))] * 2
        args += list(ctx)
    in_specs += [
        _resident((4, DK), lambda i, j: (0, 0)),
        _resident((1, D_GROUP), lambda i, j: (0, 0)),
        _resident((D_GROUP, D_GROUP), lambda i, j: (0, 0)),
    ]
    args += [dl, subw, gmat]
    n_hm = 2 * N_ATT_HEADS
    return pl.pallas_call(
        functools.partial(_attn_kernel, has_ctx=has_ctx, lam_init=lam_init, t_len=t,
                          n_steps=n_steps),
        grid=(b, n_steps),
        in_specs=in_specs,
        out_specs=pl.BlockSpec((None, tq, D_GROUP), lambda i, j: (i, j, 0)),
        out_shape=jax.ShapeDtypeStruct((b, t, D_GROUP), F32),
        scratch_shapes=[pltpu.VMEM((D_GROUP, s_len), BF16),
                        pltpu.VMEM((N_ATT_HEADS * s_len, D_GROUP), BF16),
                        pltpu.VMEM((n_hm * ATT_ROWS, s_len), F32),
                        pltpu.VMEM((n_hm * ATT_ROWS, s_len), F32),
                        pltpu.VMEM((ATT_ROWS, N_ATT_HEADS * s_len), BF16)],
        compiler_params=_params("arbitrary", "arbitrary"),
        name="diff_attn",
    )(*args)


RG_CHUNK = 128


def _shift_rows(xe, k):
    n = xe.shape[0]
    y = xe if k == 0 else pltpu.roll(xe, (-k) % n, 0)
    return y[HALO:n - HALO]


def _rg_gates(xc, wg, bg, c_dir):
    g = jnp.dot(xc.astype(BF16), wg, preferred_element_type=F32) + bg
    r = jax.nn.sigmoid(g[:, :D_GROUP])
    i = jax.nn.sigmoid(g[:, D_GROUP:])
    log_a = r * c_dir
    a = jnp.exp(log_a)
    th = jnp.tanh(log_a)
    b = jnp.sqrt(-2.0 * th / (1.0 - th)) * i * xc
    return a, b


def _rg_kernel(xr_ref, gr_ref, h0_ref, cw_ref, cb_ref, wg_ref, bg_ref, lam_ref,
               y_ref, last_ref, xpad, xc_s, hf_s, *, t_len):
    r_rows = RG_CHUNK
    n_chunks = t_len // r_rows
    n_groups = r_rows // SUBLANES
    zeros_halo = jnp.zeros((HALO, D_GROUP), F32)
    xpad[0:HALO, :] = zeros_halo
    xpad[HALO + t_len:, :] = zeros_halo
    xpad[HALO:HALO + t_len, :] = xr_ref[...]

    lam = lam_ref[...]
    neg = -lam
    softplus = jnp.maximum(neg, 0.0) + jnp.log1p(jnp.exp(-jnp.abs(neg)))
    c_all = -RG_C * softplus
    row8 = lax.broadcasted_iota(jnp.int32, (r_rows, D_GROUP), 0) % SUBLANES
    cw = cw_ref[...]

    def bcast_row(v, r):
        return jnp.broadcast_to(v[r:r + 1, :], (SUBLANES, D_GROUP))

    def fwd_body(c, h):
        s = pl.multiple_of(c * r_rows, r_rows)
        xe = xpad[pl.ds(s, r_rows + 2 * HALO), :]
        xc = (cw[0:1] * _shift_rows(xe, -2) + cw[1:2] * _shift_rows(xe, -1)
              + cw[2:3] * _shift_rows(xe, 0) + cw[3:4] * _shift_rows(xe, 1) + cb_ref[...])
        xc_s[pl.ds(s, r_rows), :] = xc
        a, b = _rg_gates(xc, wg_ref[:, 0:2 * D_GROUP], bg_ref[:, 0:2 * D_GROUP], c_all[0:1])
        for d in (1, 2, 4):
            m = row8 >= d
            b = jnp.where(m, a * pltpu.roll(b, d, 0) + b, b)
            a = jnp.where(m, a * pltpu.roll(a, d, 0), a)
        for g in range(n_groups):
            lo = g * SUBLANES
            hg = b[lo:lo + SUBLANES] + a[lo:lo + SUBLANES] * h
            hf_s[pl.ds(s + lo, SUBLANES), :] = hg
            h = bcast_row(hg, SUBLANES - 1)
        return h

    h_f = lax.fori_loop(0, n_chunks, fwd_body, bcast_row(h0_ref[...], 0))
    last_ref[0:1, :] = h_f[0:1]

    def bwd_body(ci, h):
        c = n_chunks - 1 - ci
        s = pl.multiple_of(c * r_rows, r_rows)
        xc = xc_s[pl.ds(s, r_rows), :]
        a, b = _rg_gates(xc, wg_ref[:, 2 * D_GROUP:], bg_ref[:, 2 * D_GROUP:], c_all[1:2])
        for d in (1, 2, 4):
            m = row8 < SUBLANES - d
            b = jnp.where(m, a * pltpu.roll(b, r_rows - d, 0) + b, b)
            a = jnp.where(m, a * pltpu.roll(a, r_rows - d, 0), a)
        pieces = [None] * n_groups
        for g in reversed(range(n_groups)):
            lo = g * SUBLANES
            hg = b[lo:lo + SUBLANES] + a[lo:lo + SUBLANES] * h
            pieces[g] = hg
            h = bcast_row(hg, 0)
        hb = jnp.concatenate(pieces, axis=0)
        gr = gr_ref[pl.ds(s, r_rows), :]
        gelu = 0.5 * gr * (1.0 + jnp.tanh(math.sqrt(2.0 / math.pi) * (gr + 0.044715 * gr * gr * gr)))
        y_ref[pl.ds(s, r_rows), :] = (hf_s[pl.ds(s, r_rows), :] + hb) * gelu
        return h

    h_b = lax.fori_loop(0, n_chunks, bwd_body, bcast_row(h0_ref[...], 1))
    last_ref[1:2, :] = h_b[0:1]


def _rg(z, h0, cw, cb, wg, bg, lam):
    b, t, _ = z.shape
    return pl.pallas_call(
        functools.partial(_rg_kernel, t_len=t),
        grid=(b,),
        in_specs=[
            pl.BlockSpec((None, t, D_GROUP), lambda i: (i, 0, 6)),
            pl.BlockSpec((None, t, D_GROUP), lambda i: (i, 0, 7)),
            pl.BlockSpec((None, 2, D_GROUP), lambda i: (i, 0, 0)),
            _resident((4, D_GROUP), lambda i: (0, 0)),
            _resident((1, D_GROUP), lambda i: (0, 0)),
            _resident((D_GROUP, 4 * D_GROUP), lambda i: (0, 0)),
            _resident((1, 4 * D_GROUP), lambda i: (0, 0)),
            _resident((2, D_GROUP), lambda i: (0, 0)),
        ],
        out_specs=[pl.BlockSpec((None, t, D_GROUP), lambda i: (i, 0, 0)),
                   pl.BlockSpec((None, 2, D_GROUP), lambda i: (i, 0, 0))],
        out_shape=[jax.ShapeDtypeStruct((b, t, D_GROUP), F32),
                   jax.ShapeDtypeStruct((b, 2, D_GROUP), F32)],
        scratch_shapes=[pltpu.VMEM((t + 2 * HALO, D_GROUP), F32),
                        pltpu.VMEM((t, D_GROUP), F32),
                        pltpu.VMEM((t, D_GROUP), F32)],
        compiler_params=_params("arbitrary"),
        name="rglru",
    )(z, z, h0, cw, cb, wg, bg, lam)


CP_CHUNK = 128


def _convpool_kernel(gb_ref, gc_ref, xb_ref, xp_ref, cw_ref, pw_ref, ps_ref,
                     yb_ref, yd_ref, upad, ppad, *, t_len):
    r_rows = CP_CHUNK
    n_chunks = t_len // r_rows
    zeros_halo = jnp.zeros((HALO, D_GROUP), F32)
    for pad in (upad, ppad):
        pad[0:HALO, :] = zeros_halo
        pad[HALO + t_len:, :] = zeros_halo

    def fill(c, carry):
        s = pl.multiple_of(c * r_rows, r_rows)
        upad[pl.ds(HALO + s, r_rows), :] = gc_ref[pl.ds(s, r_rows), :] * xb_ref[pl.ds(s, r_rows), :]
        ppad[pl.ds(HALO + s, r_rows), :] = xp_ref[pl.ds(s, r_rows), :]
        return carry

    lax.fori_loop(0, n_chunks, fill, 0)

    cw = cw_ref[...]
    grp = lax.broadcasted_iota(jnp.int32, (r_rows, D_GROUP), 1) // POOL_GW
    win = jnp.left_shift(2, grp)
    left = win // 2
    right = win - 1 - left
    row = lax.broadcasted_iota(jnp.int32, (r_rows, D_GROUP), 0)

    def body(c, carry):
        s = pl.multiple_of(c * r_rows, r_rows)
        ue = upad[pl.ds(s, r_rows + 2 * HALO), :]
        conv = (cw[0:1] * _shift_rows(ue, -1) + cw[1:2] * _shift_rows(ue, 0)
                + cw[2:3] * _shift_rows(ue, 1))
        yb_ref[pl.ds(s, r_rows), :] = gb_ref[pl.ds(s, r_rows), :] * conv

        pe = ppad[pl.ds(s, r_rows + 2 * HALO), :]
        q2 = pe + pltpu.roll(pe, 1, 0)
        q4 = q2 + pltpu.roll(q2, 2, 0)
        q8 = q4 + pltpu.roll(q4, 4, 0)
        q16 = q8 + pltpu.roll(q8, 8, 0)
        sums = jnp.where(grp == 0, _shift_rows(q2, 0),
                         jnp.where(grp == 1, _shift_rows(q4, 1),
                                   jnp.where(grp == 2, _shift_rows(q8, 3), _shift_rows(q16, 7))))
        t = row + s
        cnt = (jnp.minimum(t + right, t_len - 1) - jnp.maximum(t - left, 0) + 1).astype(F32)
        pm = sums / cnt - _shift_rows(pe, 0)
        yd_ref[pl.ds(s, r_rows), :] = (
            jnp.dot(pm.astype(BF16), pw_ref[...], preferred_element_type=F32) * ps_ref[...])
        return carry

    lax.fori_loop(0, n_chunks, body, 0)


def _convpool(z, cw, pw, ps):
    b, t, _ = z.shape
    col = lambda k: pl.BlockSpec((None, t, D_GROUP), lambda i: (i, 0, k))
    out = pl.BlockSpec((None, t, D_GROUP), lambda i: (i, 0, 0))
    return pl.pallas_call(
        functools.partial(_convpool_kernel, t_len=t),
        grid=(b,),
        in_specs=[col(3), col(4), col(5), col(8),
                  _resident((3, D_GROUP), lambda i: (0, 0)),
                  _resident((D_GROUP, D_GROUP), lambda i: (0, 0)),
                  _resident((1, D_GROUP), lambda i: (0, 0))],
        out_specs=[out, out],
        out_shape=[jax.ShapeDtypeStruct((b, t, D_GROUP), F32)] * 2,
        scratch_shapes=[pltpu.VMEM((t + 2 * HALO, D_GROUP), F32)] * 2,
        compiler_params=_params("arbitrary"),
        name="convpool",
    )(z, z, z, z, cw, pw, ps)


FF_CHUNK = 1024


def _mlp_kernel(x_ref, ya_ref, yb_ref, yc_ref, yd_ref, mod_ref, nw_ref, wo_ref, w1_ref, w2_ref,
                fw_ref, o_ref, *, final):
    x = x_ref[...]
    y = jnp.zeros(x.shape, F32)
    for i, r in enumerate((ya_ref, yb_ref, yc_ref, yd_ref)):
        y = y + jnp.dot(r[...].astype(BF16), wo_ref[i * D_GROUP:(i + 1) * D_GROUP, :],
                        preferred_element_type=F32)
    gate1 = mod_ref[:, 2 * D_MODEL:3 * D_MODEL]
    shift2 = mod_ref[:, 3 * D_MODEL:4 * D_MODEL]
    scale2 = mod_ref[:, 4 * D_MODEL:5 * D_MODEL]
    gate2 = mod_ref[:, 5 * D_MODEL:6 * D_MODEL]
    x1 = x + gate1 * y
    hn = (_rms(x1, nw_ref[...]) * (1.0 + scale2) + shift2).astype(BF16)
    acc = jnp.zeros(x.shape, F32)
    for f in range(D_FF // FF_CHUNK):
        u = jnp.dot(hn, w1_ref[:, f * FF_CHUNK:(f + 1) * FF_CHUNK], preferred_element_type=F32)
        u = jnp.square(jnp.maximum(u, 0.0)).astype(BF16)
        acc = acc + jnp.dot(u, w2_ref[f * FF_CHUNK:(f + 1) * FF_CHUNK, :], preferred_element_type=F32)
    x2 = x1 + gate2 * acc
    if final:
        x2 = _rms(x2, fw_ref[...])
    o_ref[...] = x2


def _mlp(x, ys, mod_l, mod_row, nw, wo, w1, w2, fw, final, tt):
    b, t, _ = x.shape
    ytile = pl.BlockSpec((None, tt, D_GROUP), lambda i, j: (i, j, 0))
    return pl.pallas_call(
        functools.partial(_mlp_kernel, final=final),
        grid=(b, t // tt),
        in_specs=[
            pl.BlockSpec((None, tt, D_MODEL), lambda i, j: (i, j, 0)),
            ytile, ytile, ytile, ytile,
            pl.BlockSpec((None, 1, N_MOD * D_MODEL), lambda i, j: (mod_row(i), 0, 0)),
            _resident((1, D_MODEL), lambda i, j: (0, 0)),
            _resident((D_MODEL, D_MODEL), lambda i, j: (0, 0)),
            _resident((D_MODEL, D_FF), lambda i, j: (0, 0)),
            _resident((D_FF, D_MODEL), lambda i, j: (0, 0)),
            _resident((1, D_MODEL), lambda i, j: (0, 0)),
        ],
        out_specs=pl.BlockSpec((None, tt, D_MODEL), lambda i, j: (i, j, 0)),
        out_shape=jax.ShapeDtypeStruct((b, t, D_MODEL), F32),
        compiler_params=_params("arbitrary", "arbitrary"),
        name="outproj_mlp",
    )(x, *ys, mod_l, nw, wo, w1, w2, fw)


def _block_diag(w):
    n, k, _ = w.shape
    eye = jnp.eye(n, dtype=w.dtype)
    return (eye[:, None, :, None] * w[:, :, None, :]).reshape(n * k, n * k)


def _rope_tables(t_len):
    half = DK // 4
    freqs = ROPE_BASE ** (-jnp.arange(half, dtype=F32) / half)
    lane = np.arange(LANES)
    m = lane % DK
    use_col = m >= DK // 2
    fidx = m % half
    t = jnp.arange(t_len)
    pos = jnp.where(use_col[None, :], (t % GRID_W)[:, None], (t // GRID_W)[:, None]).astype(F32)
    ang = pos * freqs[fidx][None, :]
    sign = np.where((lane % (2 * half)) < half, -1.0, 1.0).astype(np.float32)
    return jnp.cos(ang), jnp.sin(ang) * sign[None, :]


def _heads_to_lanes(c):
    b, h, s, v = c.shape
    return c.transpose(0, 2, 1, 3).reshape(b, s, h * v)


def _lanes_to_heads(x):
    b, s, _ = x.shape
    return x.reshape(b, s, N_ATT_HEADS, V_DIM).transpose(0, 2, 1, 3)


def kernel(x_prompt, x_sample, cache_k, cache_v, state_rglru, c, c_ctx, w_ada, b_ada, norm_w, w_in, diff_lambda, subln_w, conv_b_w, conv_c_w, conv_c_b, rg_w, rg_b, rg_lambda, pool_w, pool_scale, w_out, w_mlp1, w_mlp2, final_norm_w):
    n_lat = c.shape[0]
    ctx_row = n_lat
    cc = jnp.zeros((MOD_ROWS, D_MODEL), F32).at[:n_lat].set(c).at[ctx_row].set(c_ctx)
    mod = _ada(cc, w_ada, b_ada).reshape(DEPTH, MOD_ROWS, 1, N_MOD * D_MODEL)

    w_in_b = w_in.astype(BF16)
    w_out_b = w_out.astype(BF16)
    w1_b = w_mlp1.astype(BF16)
    w2_b = w_mlp2.astype(BF16)
    rope_tabs = _rope_tables(x_sample.shape[1])
    gmat = _block_diag(jnp.full((N_ATT_HEADS, V_DIM, V_DIM), 1.0 / V_DIM, F32))
    fw = final_norm_w.reshape(1, D_MODEL)
    zero_state = jnp.zeros((x_prompt.shape[0], 2, D_GROUP), F32)

    xp, xs = x_prompt, x_sample
    ks_out, vs_out, hs_out = [], [], []
    for l in range(DEPTH):
        lam_init = 0.8 - 0.6 * math.exp(-0.3 * l)
        final = l == DEPTH - 1
        mod_l = mod[l]
        nw1 = norm_w[l, 0].reshape(1, D_MODEL)
        nw2 = norm_w[l, 1].reshape(1, D_MODEL)
        subw = jnp.tile(subln_w[l], N_ATT_HEADS).reshape(1, D_GROUP)
        wg = jnp.concatenate([_block_diag(rg_w[l, d, g]) for d in range(2) for g in range(2)],
                             axis=1).astype(BF16)
        bg = rg_b[l].reshape(1, 4 * D_GROUP)
        pw = _block_diag(pool_w[l]).astype(BF16)
        ps = pool_scale[l].reshape(1, D_GROUP)
        cb = conv_c_b[l].reshape(1, D_GROUP)

        def run(x, mod_row, rope, ctx, h0, tt_in, tt_mlp):
            z = _inproj(x, mod_l, mod_row, nw1, w_in_b[l], rope, tt_in)
            ya = _attn(z, ctx, diff_lambda[l], subw, gmat, lam_init)
            yc, last = _rg(z, h0, conv_c_w[l], cb, wg, bg, rg_lambda[l])
            yb, yd = _convpool(z, conv_b_w[l], pw, ps)
            x_new = _mlp(x, (ya, yb, yc, yd), mod_l, mod_row, nw2, w_out_b[l], w1_b[l], w2_b[l],
                         fw, final, tt_mlp)
            return x_new, z, last

        xp, zp, last_p = run(xp, lambda i: ctx_row, None, None, zero_state, 256, 256)
        ks_out.append(_lanes_to_heads(zp[:, :, D_GROUP:2 * D_GROUP]))
        vs_out.append(_lanes_to_heads(zp[:, :, 2 * D_GROUP:3 * D_GROUP]))
        hs_out.append(last_p)
        ctx = (_heads_to_lanes(cache_k[:, l]), _heads_to_lanes(cache_v[:, l]))
        xs, _, _ = run(xs, lambda i: i, rope_tabs, ctx, state_rglru[:, l], 256, 256)

    return (xp, xs, jnp.stack(ks_out, axis=1), jnp.stack(vs_out, axis=1), jnp.stack(hs_out, axis=1))
```

```python
import functools
import math

import jax
import jax.numpy as jnp
import numpy as np
from jax import lax
from jax.experimental import pallas as pl
from jax.experimental.pallas import tpu as pltpu

D_MODEL = 1024
DEPTH = 2
GRID_W = 64
D_GROUP = 256
N_ATT_HEADS = 4
V_DIM = 64
DK = 32
ROPE_BASE = 10000.0
RG_BLOCKS = 4
RG_BW = 64
RG_C = 8.0
POOL_WINDOWS = (2, 4, 8, 16)
POOL_GW = 64
D_FF = 4 * D_MODEL
N_MOD = 6
D_IN = 9 * D_GROUP
EPS = 1e-6

LANES = 128
SUBLANES = 8
HALO = 8
VMEM_LIMIT = 56 * 1024 * 1024
MOD_ROWS = 16

BF16 = jnp.bfloat16
F32 = jnp.float32


def _params(*sem):
    return pltpu.CompilerParams(dimension_semantics=sem, vmem_limit_bytes=VMEM_LIMIT)


def _resident(shape, index_map):
    return pl.BlockSpec(shape, index_map, pipeline_mode=pl.Buffered(1))


def _rms(x, w):
    ms = jnp.mean(x * x, axis=-1, keepdims=True)
    return x * lax.rsqrt(ms + EPS) * w


def _ada_kernel(c_ref, w_ref, b_ref, o_ref):
    c = c_ref[...]
    s = (c * jax.nn.sigmoid(c)).astype(BF16)
    o_ref[...] = jnp.dot(s, w_ref[...].astype(BF16), preferred_element_type=F32) + b_ref[...]


def _ada(cc, w_ada, b_ada):
    tn = 1536
    n_out = N_MOD * D_MODEL
    return pl.pallas_call(
        _ada_kernel,
        grid=(DEPTH, n_out // tn),
        in_specs=[
            pl.BlockSpec((MOD_ROWS, D_MODEL), lambda l, n: (0, 0)),
            pl.BlockSpec((None, D_MODEL, tn), lambda l, n: (l, 0, n)),
            pl.BlockSpec((None, 1, tn), lambda l, n: (l, 0, n)),
        ],
        out_specs=pl.BlockSpec((None, MOD_ROWS, tn), lambda l, n: (l, 0, n)),
        out_shape=jax.ShapeDtypeStruct((DEPTH, MOD_ROWS, n_out), F32),
        compiler_params=_params("arbitrary", "arbitrary"),
        name="ada_mod",
    )(cc, w_ada, b_ada.reshape(DEPTH, 1, n_out))


def _inproj_kernel(*refs, rope):
    if rope:
        x_ref, mod_ref, nw_ref, w_ref, cos_ref, sin_ref, z_ref = refs
    else:
        x_ref, mod_ref, nw_ref, w_ref, z_ref, ko_ref, vo_ref = refs
    x = x_ref[...]
    shift = mod_ref[:, 0:D_MODEL]
    scale = mod_ref[:, D_MODEL:2 * D_MODEL]
    h = _rms(x, nw_ref[...]) * (1.0 + scale) + shift
    z = jnp.dot(h.astype(BF16), w_ref[...], preferred_element_type=F32)
    if not rope:
        z_ref[...] = z.astype(BF16)
        for hd in range(N_ATT_HEADS):
            ko_ref[hd] = z[:, D_GROUP + hd * V_DIM:D_GROUP + (hd + 1) * V_DIM]
            vo_ref[hd] = z[:, 2 * D_GROUP + hd * V_DIM:2 * D_GROUP + (hd + 1) * V_DIM]
        return
    cos = cos_ref[...]
    sin = sin_ref[...]
    first_half = (lax.broadcasted_iota(jnp.int32, (1, LANES), 1) % 16) < 8
    for j in range(2 * D_GROUP // LANES):
        zc = z[:, j * LANES:(j + 1) * LANES]
        partner = jnp.where(first_half,
                            pltpu.roll(zc, LANES - 8, 1),
                            pltpu.roll(zc, 8, 1))
        z_ref[:, j * LANES:(j + 1) * LANES] = (zc * cos + partner * sin).astype(BF16)
    z_ref[:, 2 * D_GROUP:] = z[:, 2 * D_GROUP:].astype(BF16)


def _inproj(x, mod, layer, mod_row, norm_w, w_in, rope_tabs, tt):
    b, t, _ = x.shape
    rope = rope_tabs is not None
    in_specs = [
        pl.BlockSpec((None, tt, D_MODEL), lambda i, j: (i, j, 0)),
        pl.BlockSpec((None, None, 1, N_MOD * D_MODEL), lambda i, j: (layer, mod_row(i), 0, 0)),
        _resident((None, None, 1, D_MODEL), lambda i, j: (layer, 0, 0, 0)),
        _resident((None, D_MODEL, D_IN), lambda i, j: (layer, 0, 0)),
    ]
    args = [x, mod, norm_w, w_in]
    out_specs = [pl.BlockSpec((None, tt, D_IN), lambda i, j: (i, j, 0))]
    out_shape = [jax.ShapeDtypeStruct((b, t, D_IN), BF16)]
    if rope:
        in_specs += [pl.BlockSpec((tt, LANES), lambda i, j: (j, 0))] * 2
        args += list(rope_tabs)
    else:
        out_specs += [pl.BlockSpec((None, N_ATT_HEADS, tt, V_DIM), lambda i, j: (i, 0, j, 0))] * 2
        out_shape += [jax.ShapeDtypeStruct((b, N_ATT_HEADS, t, V_DIM), F32)] * 2
    return pl.pallas_call(
        functools.partial(_inproj_kernel, rope=rope),
        grid=(b, t // tt),
        in_specs=in_specs,
        out_specs=out_specs,
        out_shape=out_shape,
        compiler_params=_params("arbitrary", "arbitrary"),
        name="inproj",
    )(*args)


ATT_ROWS = 128
LOG2E = math.log2(math.e)


def _attn_kernel(*refs, has_ctx, lam_init, t_len, n_steps):
    if has_ctx:
        (q_ref, qn_ref, k_ref, v_ref, ck_ref, cv_ref, dl_ref, sw_ref, g_ref, o_ref,
         kt, vs, s0, s1, pb) = refs
    else:
        q_ref, qn_ref, k_ref, v_ref, dl_ref, sw_ref, g_ref, o_ref, kt, vs, s0, s1, pb = refs
    rows = ATT_ROWS
    past = ck_ref.shape[1] if has_ctx else 0
    s_len = past + t_len
    lane = lax.broadcasted_iota(jnp.int32, (1, D_GROUP), 1)

    def scores(q_rows, dst):
        q = q_rows.astype(F32) * (DK ** -0.5 * LOG2E)
        stack = jnp.concatenate(
            [jnp.where((lane // DK) == hm, q, 0.0).astype(BF16) for hm in range(2 * N_ATT_HEADS)],
            axis=0)
        dst[...] = jnp.dot(stack, kt[...], preferred_element_type=F32)

    @pl.when(pl.program_id(1) == 0)
    def _():
        if has_ctx:
            heads = range(N_ATT_HEADS)
            ck = jnp.concatenate([ck_ref[h] for h in heads], axis=-1)
            cv = jnp.concatenate([cv_ref[h] for h in heads], axis=-1)
            kt[:, 0:past] = ck.T.astype(BF16)
        kt[:, past:s_len] = k_ref[...].astype(F32).T.astype(BF16)
        v = v_ref[...]
        for h in range(N_ATT_HEADS):
            hm = (lane // V_DIM) == h
            if has_ctx:
                vs[h * s_len:h * s_len + past, :] = jnp.where(hm, cv, 0.0).astype(BF16)
            vs[h * s_len + past:(h + 1) * s_len, :] = jnp.where(hm, v, jnp.zeros_like(v))
        scores(q_ref[0:rows, :], s0)

    dl = dl_ref[...]
    lam = (jnp.exp(jnp.sum(dl[0:1] * dl[1:2], axis=-1, keepdims=True))
           - jnp.exp(jnp.sum(dl[2:3] * dl[3:4], axis=-1, keepdims=True)) + lam_init)

    def finish(src, out_lo):
        inv_l1 = jnp.zeros((rows, D_GROUP), F32)
        for h in range(N_ATT_HEADS):
            es, ls = [], []
            for m in range(2):
                lo = (2 * h + m) * rows
                s = src[lo:lo + rows, :]
                e = jnp.exp2(s - jnp.max(s, axis=-1, keepdims=True))
                es.append(e)
                ls.append(jnp.sum(e, axis=-1, keepdims=True))
            beta = lam * ls[0] / ls[1]
            pb[:, h * s_len:(h + 1) * s_len] = (es[0] - beta * es[1]).astype(BF16)
            inv_l1 = jnp.where((lane // V_DIM) == h, 1.0 / ls[0], inv_l1)
        acc = jnp.dot(pb[...], vs[...], preferred_element_type=F32) * inv_l1
        ms = jnp.dot(acc * acc, g_ref[...], preferred_element_type=F32,
                     precision=lax.Precision.HIGHEST)
        o_ref[out_lo:out_lo + rows, :] = (
            acc * lax.rsqrt(ms + EPS) * sw_ref[...] * (1.0 - lam_init)).astype(BF16)

    scores(q_ref[rows:2 * rows, :], s1)
    finish(s0, 0)
    if n_steps > 1:
        scores(qn_ref[0:rows, :], s0)
    finish(s1, rows)


def _attn(z, ctx, layer, dl, subw, gmat, lam_init):
    b, t, _ = z.shape
    has_ctx = ctx is not None
    past = ctx[0].shape[3] if has_ctx else 0
    s_len = past + t
    tq = 2 * ATT_ROWS
    n_steps = t // tq
    in_specs = [
        pl.BlockSpec((None, tq, D_GROUP), lambda i, j: (i, j, 0)),
        pl.BlockSpec((None, tq, D_GROUP), lambda i, j: (i, jnp.minimum(j + 1, n_steps - 1), 0)),
        pl.BlockSpec((None, t, D_GROUP), lambda i, j: (i, 0, 1)),
        pl.BlockSpec((None, t, D_GROUP), lambda i, j: (i, 0, 2)),
    ]
    args = [z, z, z, z]
    if has_ctx:
        in_specs += [pl.BlockSpec((None, None, N_ATT_HEADS, past, V_DIM),
                                  lambda i, j: (i, layer, 0, 0, 0))] * 2
        args += list(ctx)
    in_specs += [
        _resident((None, 4, DK), lambda i, j: (layer, 0, 0)),
        _resident((None, 1, D_GROUP), lambda i, j: (layer, 0, 0)),
        _resident((D_GROUP, D_GROUP), lambda i, j: (0, 0)),
    ]
    args += [dl, subw, gmat]
    n_hm = 2 * N_ATT_HEADS
    return pl.pallas_call(
        functools.partial(_attn_kernel, has_ctx=has_ctx, lam_init=lam_init, t_len=t,
                          n_steps=n_steps),
        grid=(b, n_steps),
        in_specs=in_specs,
        out_specs=pl.BlockSpec((None, tq, D_GROUP), lambda i, j: (i, j, 0)),
        out_shape=jax.ShapeDtypeStruct((b, t, D_GROUP), BF16),
        scratch_shapes=[pltpu.VMEM((D_GROUP, s_len), BF16),
                        pltpu.VMEM((N_ATT_HEADS * s_len, D_GROUP), BF16),
                        pltpu.VMEM((n_hm * ATT_ROWS, s_len), F32),
                        pltpu.VMEM((n_hm * ATT_ROWS, s_len), F32),
                        pltpu.VMEM((ATT_ROWS, N_ATT_HEADS * s_len), BF16)],
        compiler_params=_params("arbitrary", "arbitrary"),
        name="diff_attn",
    )(*args)


RG_ROWS = 256
RG_BLOCK = 64


def _shift_rows(xe, k):
    n = xe.shape[0]
    y = xe if k == 0 else pltpu.roll(xe, (-k) % n, 0)
    return y[HALO:n - HALO]


def _strided8(ref, half, start):
    return ref[half, pl.ds(start, SUBLANES, stride=SUBLANES), :]


def _rg_gates(xc, wg, bg, c_dir):
    g = jnp.dot(xc.astype(BF16), wg, preferred_element_type=F32) + bg
    r = jax.nn.sigmoid(g[:, :D_GROUP])
    i = jax.nn.sigmoid(g[:, D_GROUP:])
    a = jnp.exp(r * c_dir)
    om = 1.0 - a * a
    b = jnp.where(om > 0.0, om * lax.rsqrt(om), 0.0) * i * xc
    return a, b


def _scan_block(a, b, hp, reverse):
    n = SUBLANES
    order = list(range(n - 1, -1, -1)) if reverse else list(range(n))
    hs, cum = [None] * n, [None] * n
    prev = None
    for i in order:
        if prev is None:
            hs[i], cum[i] = b[i], a[i]
        else:
            hs[i], cum[i] = a[i] * hs[prev] + b[i], a[i] * cum[prev]
        prev = i
    p, f = cum[prev], hs[prev]
    row = lax.broadcasted_iota(jnp.int32, (n, LANES), 0)
    for d in (1, 2, 4):
        m = (row < n - d) if reverse else (row >= d)
        sh = n - d if reverse else d
        f = jnp.where(m, p * pltpu.roll(f, sh, 0) + f, f)
        p = jnp.where(m, p * pltpu.roll(p, sh, 0), p)
    first, last = (n - 1, 0) if reverse else (0, n - 1)
    sh = n - 1 if reverse else 1
    cin = jnp.where(row == first, hp, pltpu.roll(f, sh, 0) + pltpu.roll(p, sh, 0) * hp)
    out = [hs[i] + cum[i] * cin for i in range(n)]
    bc = lambda v: jnp.broadcast_to(v[last:last + 1, :], (n, LANES))
    return out, bc(f) + bc(p) * hp


def _rg_kernel(xr_ref, gr_ref, h0_ref, cw_ref, cb_ref, wg_ref, bg_ref, lam_ref,
               y_ref, last_ref, xs, gs, ys, xc_s, hf_s, *, t_len):
    n_chunks = t_len // RG_ROWS
    n_blk = RG_ROWS // RG_BLOCK
    zeros_halo = jnp.zeros((HALO, LANES), F32)
    for half in range(2):
        lo = half * LANES
        xs[half, 0:HALO, :] = zeros_halo
        xs[half, HALO + t_len:, :] = zeros_halo
        xs[half, HALO:HALO + t_len, :] = xr_ref[:, lo:lo + LANES].astype(F32)
        gs[half, :, :] = gr_ref[:, lo:lo + LANES].astype(F32)

    neg = -lam_ref[...]
    softplus = jnp.maximum(neg, 0.0) + jnp.log1p(jnp.exp(-jnp.abs(neg)))
    c_all = -RG_C * softplus
    cw = cw_ref[...]
    cb = cb_ref[...]

    def piece(v, blk, i, half):
        r0 = blk * RG_BLOCK + i * SUBLANES
        return v[r0:r0 + SUBLANES, half * LANES:(half + 1) * LANES]

    def scan_chunk(a, b, carry, reverse):
        carry = list(carry)
        out = {}
        for blk in (reversed(range(n_blk)) if reverse else range(n_blk)):
            for half in range(2):
                hs, carry[half] = _scan_block([piece(a, blk, i, half) for i in range(SUBLANES)],
                                              [piece(b, blk, i, half) for i in range(SUBLANES)],
                                              carry[half], reverse)
                for i in range(SUBLANES):
                    out[blk, i, half] = hs[i]
        rows = [jnp.concatenate([out[blk, i, 0], out[blk, i, 1]], axis=-1)
                for blk in range(n_blk) for i in range(SUBLANES)]
        return jnp.concatenate(rows, axis=0), tuple(carry)

    def start_state(d):
        return tuple(jnp.broadcast_to(h0_ref[d:d + 1, half * LANES:(half + 1) * LANES],
                                      (SUBLANES, LANES)) for half in range(2))

    def fwd_body(c, carry):
        s = pl.multiple_of(c * RG_ROWS, RG_ROWS)
        rows = []
        for blk in range(n_blk):
            base = s + blk * RG_BLOCK + HALO
            halves = []
            for half in range(2):
                lo = half * LANES
                x = {i: _strided8(xs, half, base + i) for i in range(-2, SUBLANES + 1)}
                halves.append([cw[0:1, lo:lo + LANES] * x[i - 2] + cw[1:2, lo:lo + LANES] * x[i - 1]
                               + cw[2:3, lo:lo + LANES] * x[i] + cw[3:4, lo:lo + LANES] * x[i + 1]
                               + cb[:, lo:lo + LANES] for i in range(SUBLANES)])
            rows += [jnp.concatenate([halves[0][i], halves[1][i]], axis=-1) for i in range(SUBLANES)]
        xc = jnp.concatenate(rows, axis=0)
        xc_s[pl.ds(s, RG_ROWS), :] = xc
        a, b = _rg_gates(xc, wg_ref[:, 0:2 * D_GROUP], bg_ref[:, 0:2 * D_GROUP], c_all[0:1])
        hf, carry = scan_chunk(a, b, carry, False)
        hf_s[pl.ds(s, RG_ROWS), :] = hf
        return carry

    h_f = lax.fori_loop(0, n_chunks, fwd_body, start_state(0))
    last_ref[0:1, :] = jnp.concatenate([h_f[0][0:1], h_f[1][0:1]], axis=-1)

    def bwd_body(ci, carry):
        c = n_chunks - 1 - ci
        s = pl.multiple_of(c * RG_ROWS, RG_ROWS)
        xc = xc_s[pl.ds(s, RG_ROWS), :]
        a, b = _rg_gates(xc, wg_ref[:, 2 * D_GROUP:], bg_ref[:, 2 * D_GROUP:], c_all[1:2])
        hb, carry = scan_chunk(a, b, carry, True)
        gr = jnp.concatenate(
            [jnp.concatenate([_strided8(gs, half, s + blk * RG_BLOCK + i) for half in range(2)], axis=-1)
             for blk in range(n_blk) for i in range(SUBLANES)], axis=0)
        gelu = 0.5 * gr * (1.0 + jnp.tanh(math.sqrt(2.0 / math.pi) * (gr + 0.044715 * gr * gr * gr)))
        y = (hf_s[pl.ds(s, RG_ROWS), :] + hb) * gelu
        for blk in range(n_blk):
            for i in range(SUBLANES):
                for half in range(2):
                    ys[half, pl.ds(s + blk * RG_BLOCK + i, SUBLANES, stride=SUBLANES), :] = (
                        piece(y, blk, i, half))
        y_ref[pl.ds(s, RG_ROWS), :] = jnp.concatenate(
            [ys[0, pl.ds(s, RG_ROWS), :], ys[1, pl.ds(s, RG_ROWS), :]], axis=-1).astype(BF16)
        return carry

    h_b = lax.fori_loop(0, n_chunks, bwd_body, start_state(1))
    last_ref[1:2, :] = jnp.concatenate([h_b[0][0:1], h_b[1][0:1]], axis=-1)


def _rg(z, h0, layer, cw, cb, wg, bg, lam):
    b, t, _ = z.shape
    slab = lambda rows: pltpu.VMEM((2, rows, LANES), F32)
    if h0.ndim == 4:
        h0_spec = pl.BlockSpec((None, None, 2, D_GROUP), lambda i: (i, layer, 0, 0))
    else:
        h0_spec = pl.BlockSpec((None, 2, D_GROUP), lambda i: (i, 0, 0))
    return pl.pallas_call(
        functools.partial(_rg_kernel, t_len=t),
        grid=(b,),
        in_specs=[
            pl.BlockSpec((None, t, D_GROUP), lambda i: (i, 0, 6)),
            pl.BlockSpec((None, t, D_GROUP), lambda i: (i, 0, 7)),
            h0_spec,
            _resident((None, 4, D_GROUP), lambda i: (layer, 0, 0)),
            _resident((None, 1, D_GROUP), lambda i: (layer, 0, 0)),
            _resident((D_GROUP, 4 * D_GROUP), lambda i: (0, 0)),
            _resident((None, 1, 4 * D_GROUP), lambda i: (layer, 0, 0)),
            _resident((None, 2, D_GROUP), lambda i: (layer, 0, 0)),
        ],
        out_specs=[pl.BlockSpec((None, t, D_GROUP), lambda i: (i, 0, 0)),
                   pl.BlockSpec((None, 2, D_GROUP), lambda i: (i, 0, 0))],
        out_shape=[jax.ShapeDtypeStruct((b, t, D_GROUP), BF16),
                   jax.ShapeDtypeStruct((b, 2, D_GROUP), F32)],
        scratch_shapes=[slab(t + 2 * HALO), slab(t), slab(t),
                        pltpu.VMEM((t, D_GROUP), F32),
                        pltpu.VMEM((t, D_GROUP), F32)],
        compiler_params=_params("arbitrary"),
        name="rglru",
    )(z, z, h0, cw, cb, wg, bg, lam)


CP_CHUNK = 128


def _convpool_kernel(gb_ref, gc_ref, xb_ref, xp_ref, cw_ref, pw_ref, ps_ref,
                     yb_ref, yd_ref, upad, ppad, *, t_len):
    r_rows = CP_CHUNK
    n_chunks = t_len // r_rows
    zeros_halo = jnp.zeros((HALO, D_GROUP), F32)
    for pad in (upad, ppad):
        pad[0:HALO, :] = zeros_halo
        pad[HALO + t_len:, :] = zeros_halo

    def fill(c, carry):
        s = pl.multiple_of(c * r_rows, r_rows)
        upad[pl.ds(HALO + s, r_rows), :] = (gc_ref[pl.ds(s, r_rows), :].astype(F32)
                                            * xb_ref[pl.ds(s, r_rows), :].astype(F32))
        ppad[pl.ds(HALO + s, r_rows), :] = xp_ref[pl.ds(s, r_rows), :].astype(F32)
        return carry

    lax.fori_loop(0, n_chunks, fill, 0)

    cw = cw_ref[...]
    grp = lax.broadcasted_iota(jnp.int32, (r_rows, D_GROUP), 1) // POOL_GW
    win = jnp.left_shift(2, grp)
    left = win // 2
    right = win - 1 - left
    row = lax.broadcasted_iota(jnp.int32, (r_rows, D_GROUP), 0)

    def body(c, carry):
        s = pl.multiple_of(c * r_rows, r_rows)
        ue = upad[pl.ds(s, r_rows + 2 * HALO), :]
        conv = (cw[0:1] * _shift_rows(ue, -1) + cw[1:2] * _shift_rows(ue, 0)
                + cw[2:3] * _shift_rows(ue, 1))
        yb_ref[pl.ds(s, r_rows), :] = (gb_ref[pl.ds(s, r_rows), :].astype(F32) * conv).astype(BF16)

        pe = ppad[pl.ds(s, r_rows + 2 * HALO), :]
        q2 = pe + pltpu.roll(pe, 1, 0)
        q4 = q2 + pltpu.roll(q2, 2, 0)
        q8 = q4 + pltpu.roll(q4, 4, 0)
        q16 = q8 + pltpu.roll(q8, 8, 0)
        sums = jnp.where(grp == 0, _shift_rows(q2, 0),
                         jnp.where(grp == 1, _shift_rows(q4, 1),
                                   jnp.where(grp == 2, _shift_rows(q8, 3), _shift_rows(q16, 7))))
        t = row + s
        cnt = (jnp.minimum(t + right, t_len - 1) - jnp.maximum(t - left, 0) + 1).astype(F32)
        pm = sums / cnt - _shift_rows(pe, 0)
        yd_ref[pl.ds(s, r_rows), :] = (
            jnp.dot(pm.astype(BF16), pw_ref[...], preferred_element_type=F32) * ps_ref[...]).astype(BF16)
        return carry

    lax.fori_loop(0, n_chunks, body, 0)


def _convpool(z, layer, cw, pw, ps):
    b, t, _ = z.shape
    col = lambda k: pl.BlockSpec((None, t, D_GROUP), lambda i: (i, 0, k))
    out = pl.BlockSpec((None, t, D_GROUP), lambda i: (i, 0, 0))
    return pl.pallas_call(
        functools.partial(_convpool_kernel, t_len=t),
        grid=(b,),
        in_specs=[col(3), col(4), col(5), col(8),
                  _resident((None, 3, D_GROUP), lambda i: (layer, 0, 0)),
                  _resident((D_GROUP, D_GROUP), lambda i: (0, 0)),
                  _resident((None, 1, D_GROUP), lambda i: (layer, 0, 0))],
        out_specs=[out, out],
        out_shape=[jax.ShapeDtypeStruct((b, t, D_GROUP), BF16)] * 2,
        scratch_shapes=[pltpu.VMEM((t + 2 * HALO, D_GROUP), F32)] * 2,
        compiler_params=_params("arbitrary"),
        name="convpool",
    )(z, z, z, z, cw, pw, ps)


FF_CHUNK = 1024


def _mlp_kernel(x_ref, ya_ref, yb_ref, yc_ref, yd_ref, mod_ref, nw_ref, wo_ref, w1_ref, w2_ref,
                fw_ref, o_ref, *, final):
    x = x_ref[...]
    y = jnp.zeros(x.shape, F32)
    for i, r in enumerate((ya_ref, yb_ref, yc_ref, yd_ref)):
        y = y + jnp.dot(r[...], wo_ref[i * D_GROUP:(i + 1) * D_GROUP, :],
                        preferred_element_type=F32)
    gate1 = mod_ref[:, 2 * D_MODEL:3 * D_MODEL]
    shift2 = mod_ref[:, 3 * D_MODEL:4 * D_MODEL]
    scale2 = mod_ref[:, 4 * D_MODEL:5 * D_MODEL]
    gate2 = mod_ref[:, 5 * D_MODEL:6 * D_MODEL]
    x1 = x + gate1 * y
    hn = (_rms(x1, nw_ref[...]) * (1.0 + scale2) + shift2).astype(BF16)
    acc = jnp.zeros(x.shape, F32)
    for f in range(D_FF // FF_CHUNK):
        u = jnp.dot(hn, w1_ref[:, f * FF_CHUNK:(f + 1) * FF_CHUNK], preferred_element_type=F32)
        u = jnp.square(jnp.maximum(u, 0.0)).astype(BF16)
        acc = acc + jnp.dot(u, w2_ref[f * FF_CHUNK:(f + 1) * FF_CHUNK, :], preferred_element_type=F32)
    x2 = x1 + gate2 * acc
    if final:
        x2 = _rms(x2, fw_ref[...])
    o_ref[...] = x2


def _mlp(x, ys, mod, layer, mod_row, norm_w, wo, w1, w2, fw, final, tt):
    b, t, _ = x.shape
    ytile = pl.BlockSpec((None, tt, D_GROUP), lambda i, j: (i, j, 0))
    return pl.pallas_call(
        functools.partial(_mlp_kernel, final=final),
        grid=(b, t // tt),
        in_specs=[
            pl.BlockSpec((None, tt, D_MODEL), lambda i, j: (i, j, 0)),
            ytile, ytile, ytile, ytile,
            pl.BlockSpec((None, None, 1, N_MOD * D_MODEL), lambda i, j: (layer, mod_row(i), 0, 0)),
            _resident((None, None, 1, D_MODEL), lambda i, j: (layer, 1, 0, 0)),
            _resident((None, D_MODEL, D_MODEL), lambda i, j: (layer, 0, 0)),
            _resident((None, D_MODEL, D_FF), lambda i, j: (layer, 0, 0)),
            _resident((None, D_FF, D_MODEL), lambda i, j: (layer, 0, 0)),
            _resident((1, D_MODEL), lambda i, j: (0, 0)),
        ],
        out_specs=pl.BlockSpec((None, tt, D_MODEL), lambda i, j: (i, j, 0)),
        out_shape=jax.ShapeDtypeStruct((b, t, D_MODEL), F32),
        compiler_params=_params("arbitrary", "arbitrary"),
        name="outproj_mlp",
    )(x, *ys, mod, norm_w, wo, w1, w2, fw)


def _block_diag(w):
    *lead, n, k, _ = w.shape
    eye = jnp.eye(n, dtype=w.dtype)
    return (eye[:, None, :, None] * w[..., :, :, None, :]).reshape(*lead, n * k, n * k)


def _rope_tables(t_len):
    half = DK // 4
    freqs = ROPE_BASE ** (-jnp.arange(half, dtype=F32) / half)
    lane = np.arange(LANES)
    m = lane % DK
    use_col = m >= DK // 2
    fidx = m % half
    t = jnp.arange(t_len)
    pos = jnp.where(use_col[None, :], (t % GRID_W)[:, None], (t // GRID_W)[:, None]).astype(F32)
    ang = pos * freqs[fidx][None, :]
    sign = np.where((lane % (2 * half)) < half, -1.0, 1.0).astype(np.float32)
    return jnp.cos(ang), jnp.sin(ang) * sign[None, :]


def kernel(x_prompt, x_sample, cache_k, cache_v, state_rglru, c, c_ctx, w_ada, b_ada, norm_w, w_in, diff_lambda, subln_w, conv_b_w, conv_c_w, conv_c_b, rg_w, rg_b, rg_lambda, pool_w, pool_scale, w_out, w_mlp1, w_mlp2, final_norm_w):
    n_lat = c.shape[0]
    ctx_row = n_lat
    cc = jnp.zeros((MOD_ROWS, D_MODEL), F32).at[:n_lat].set(c).at[ctx_row].set(c_ctx)
    mod = _ada(cc, w_ada, b_ada).reshape(DEPTH, MOD_ROWS, 1, N_MOD * D_MODEL)

    w_in_b = w_in.astype(BF16)
    w_out_b = w_out.astype(BF16)
    w1_b = w_mlp1.astype(BF16)
    w2_b = w_mlp2.astype(BF16)
    norm_w4 = norm_w.reshape(DEPTH, 2, 1, D_MODEL)
    rope_tabs = _rope_tables(x_sample.shape[1])
    gmat = _block_diag(jnp.full((N_ATT_HEADS, V_DIM, V_DIM), 1.0 / V_DIM, F32))
    subw = jnp.tile(subln_w, (1, N_ATT_HEADS)).reshape(DEPTH, 1, D_GROUP)
    wg = _block_diag(rg_w).transpose(0, 3, 1, 2, 4).reshape(DEPTH, D_GROUP, 4 * D_GROUP).astype(BF16)
    bg = rg_b.reshape(DEPTH, 1, 4 * D_GROUP)
    pw = _block_diag(pool_w).astype(BF16)
    ps = pool_scale.reshape(DEPTH, 1, D_GROUP)
    cb = conv_c_b.reshape(DEPTH, 1, D_GROUP)
    fw = final_norm_w.reshape(1, D_MODEL)
    zero_state = jnp.zeros((x_prompt.shape[0], 2, D_GROUP), F32)

    xp, xs = x_prompt, x_sample
    ks_out, vs_out, hs_out = [], [], []
    for l in range(DEPTH):
        lam_init = 0.8 - 0.6 * math.exp(-0.3 * l)
        final = l == DEPTH - 1

        def mixers(z, ctx, h0):
            ya = _attn(z, ctx, l, diff_lambda, subw, gmat, lam_init)
            yc, last = _rg(z, h0, l, conv_c_w, cb, wg[l], bg, rg_lambda)
            yb, yd = _convpool(z, l, conv_b_w, pw[l], ps)
            return (ya, yb, yc, yd), last

        ctx_rows = lambda i: ctx_row
        zp, k_new, v_new = _inproj(xp, mod, l, ctx_rows, norm_w4, w_in_b, None, 256)
        ys, last_p = mixers(zp, None, zero_state)
        xp = _mlp(xp, ys, mod, l, ctx_rows, norm_w4, w_out_b, w1_b, w2_b, fw, final, 256)
        ks_out.append(k_new)
        vs_out.append(v_new)
        hs_out.append(last_p)

        lat_rows = lambda i: i
        (zs,) = _inproj(xs, mod, l, lat_rows, norm_w4, w_in_b, rope_tabs, 256)
        ys, _ = mixers(zs, (cache_k, cache_v), state_rglru)
        xs = _mlp(xs, ys, mod, l, lat_rows, norm_w4, w_out_b, w1_b, w2_b, fw, final, 256)

    return (xp, xs, jnp.stack(ks_out, axis=1), jnp.stack(vs_out, axis=1), jnp.stack(hs_out, axis=1))
```

```python
import functools
import math

import jax
import jax.numpy as jnp
import numpy as np
from jax import lax
from jax.experimental import pallas as pl
from jax.experimental.pallas import tpu as pltpu

D_MODEL = 1024
DEPTH = 2
GRID_W = 64
D_GROUP = 256
N_ATT_HEADS = 4
V_DIM = 64
DK = 32
ROPE_BASE = 10000.0
RG_BLOCKS = 4
RG_BW = 64
RG_C = 8.0
POOL_WINDOWS = (2, 4, 8, 16)
POOL_GW = 64
D_FF = 4 * D_MODEL
N_MOD = 6
D_IN = 9 * D_GROUP
EPS = 1e-6

LANES = 128
SUBLANES = 8
HALO = 8
VMEM_LIMIT = 56 * 1024 * 1024
MOD_ROWS = 16
TOKEN_TILE = 512

BF16 = jnp.bfloat16
F32 = jnp.float32


def _params(*sem):
    return pltpu.CompilerParams(dimension_semantics=sem, vmem_limit_bytes=VMEM_LIMIT)


def _resident(shape, index_map):
    return pl.BlockSpec(shape, index_map, pipeline_mode=pl.Buffered(1))


def _rms(x, w):
    ms = jnp.mean(x * x, axis=-1, keepdims=True)
    return x * lax.rsqrt(ms + EPS) * w


def _ada_kernel(c_ref, w_ref, b_ref, o_ref):
    c = c_ref[...]
    s = (c * jax.nn.sigmoid(c)).astype(BF16)
    o_ref[...] = jnp.dot(s, w_ref[...].astype(BF16), preferred_element_type=F32) + b_ref[...]


def _ada(cc, w_ada, b_ada):
    tn = 1536
    n_out = N_MOD * D_MODEL
    return pl.pallas_call(
        _ada_kernel,
        grid=(DEPTH, n_out // tn),
        in_specs=[
            pl.BlockSpec((MOD_ROWS, D_MODEL), lambda l, n: (0, 0)),
            pl.BlockSpec((None, D_MODEL, tn), lambda l, n: (l, 0, n)),
            pl.BlockSpec((None, 1, tn), lambda l, n: (l, 0, n)),
        ],
        out_specs=pl.BlockSpec((None, MOD_ROWS, tn), lambda l, n: (l, 0, n)),
        out_shape=jax.ShapeDtypeStruct((DEPTH, MOD_ROWS, n_out), F32),
        compiler_params=_params("arbitrary", "arbitrary"),
        name="ada_mod",
    )(cc, w_ada, b_ada.reshape(DEPTH, 1, n_out))


def _inproj_kernel(*refs, rope, n_prev, seq_len):
    if rope:
        x_ref, mod_ref, nw_ref, w_ref, cos_ref, sin_ref, z_ref = refs
    else:
        x_ref, mod_ref, nw_ref, w_ref = refs[:4]
        prev_refs = refs[4:4 + 2 * n_prev]
        z_ref, ko_ref, vo_ref = refs[4 + 2 * n_prev:]
    x = x_ref[...]
    shift = mod_ref[:, 0:D_MODEL]
    scale = mod_ref[:, D_MODEL:2 * D_MODEL]
    h = _rms(x, nw_ref[...]) * (1.0 + scale) + shift
    z = jnp.dot(h.astype(BF16), w_ref[...], preferred_element_type=F32)
    if not rope:
        z_ref[...] = z.astype(BF16)
        for sq in range(x.shape[0] // seq_len):
            r0 = sq * seq_len
            if n_prev:
                for lp in range(n_prev):
                    ko_ref[sq, lp] = prev_refs[2 * lp][sq]
                    vo_ref[sq, lp] = prev_refs[2 * lp + 1][sq]
                ko_l, vo_l = ko_ref.at[sq, n_prev], vo_ref.at[sq, n_prev]
            else:
                ko_l, vo_l = ko_ref.at[sq], vo_ref.at[sq]
            for hd in range(N_ATT_HEADS):
                c0 = D_GROUP + hd * V_DIM
                ko_l[hd] = z[r0:r0 + seq_len, c0:c0 + V_DIM]
                vo_l[hd] = z[r0:r0 + seq_len, D_GROUP + c0:D_GROUP + c0 + V_DIM]
        return
    cos = cos_ref[...]
    sin = sin_ref[...]
    first_half = (lax.broadcasted_iota(jnp.int32, (1, LANES), 1) % 16) < 8
    for j in range(2 * D_GROUP // LANES):
        zc = z[:, j * LANES:(j + 1) * LANES]
        partner = jnp.where(first_half,
                            pltpu.roll(zc, LANES - 8, 1),
                            pltpu.roll(zc, 8, 1))
        z_ref[:, j * LANES:(j + 1) * LANES] = (zc * cos + partner * sin).astype(BF16)
    z_ref[:, 2 * D_GROUP:] = z[:, 2 * D_GROUP:].astype(BF16)


def _inproj(x, mod, layer, mod_row, norm_w, w_in, rope_tabs, tt, kv_prev=()):
    b, t, _ = x.shape
    rope = rope_tabs is not None
    n_prev = len(kv_prev) // 2
    if rope:
        grid = (b, t // tt)
        tok = lambda i, j: (i, j, 0)
        x3 = x
    else:
        grid = (1, b * t // tt)
        tok = lambda i, j: (0, j, 0)
        x3 = x.reshape(1, b * t, D_MODEL)
    in_specs = [
        pl.BlockSpec((None, tt, D_MODEL), tok),
        pl.BlockSpec((None, None, 1, N_MOD * D_MODEL), lambda i, j: (layer, mod_row(i), 0, 0)),
        _resident((None, None, 1, D_MODEL), lambda i, j: (layer, 0, 0, 0)),
        _resident((None, D_MODEL, D_IN), lambda i, j: (layer, 0, 0)),
    ]
    args = [x3, mod, norm_w, w_in]
    out_specs = [pl.BlockSpec((None, tt, D_IN), tok)]
    out_shape = [jax.ShapeDtypeStruct(x3.shape[:2] + (D_IN,), BF16)]
    if rope:
        in_specs += [pl.BlockSpec((tt, LANES), lambda i, j: (j, 0))] * 2
        args += list(rope_tabs)
    else:
        n_seq = tt // t
        per_layer = pl.BlockSpec((n_seq, N_ATT_HEADS, t, V_DIM), lambda i, j: (j, 0, 0, 0))
        in_specs += [per_layer] * (2 * n_prev)
        args += list(kv_prev)
        if n_prev:
            out_specs += [pl.BlockSpec((n_seq, n_prev + 1, N_ATT_HEADS, t, V_DIM),
                                       lambda i, j: (j, 0, 0, 0, 0))] * 2
            out_shape += [jax.ShapeDtypeStruct((b, n_prev + 1, N_ATT_HEADS, t, V_DIM), F32)] * 2
        else:
            out_specs += [per_layer] * 2
            out_shape += [jax.ShapeDtypeStruct((b, N_ATT_HEADS, t, V_DIM), F32)] * 2
    outs = pl.pallas_call(
        functools.partial(_inproj_kernel, rope=rope, n_prev=n_prev, seq_len=t),
        grid=grid,
        in_specs=in_specs,
        out_specs=out_specs,
        out_shape=out_shape,
        compiler_params=_params("arbitrary", "arbitrary"),
        name="inproj",
    )(*args)
    return [outs[0].reshape(b, t, D_IN)] + list(outs[1:])


ATT_ROWS = 128
LOG2E = math.log2(math.e)


def _attn_kernel(*refs, has_ctx, lam_init, t_len, n_steps):
    if has_ctx:
        (q_ref, qn_ref, k_ref, v_ref, ck_ref, cv_ref, dl_ref, sw_ref, g_ref, o_ref,
         kt, vs, s0, s1, pb) = refs
    else:
        q_ref, qn_ref, k_ref, v_ref, dl_ref, sw_ref, g_ref, o_ref, kt, vs, s0, s1, pb = refs
    rows = ATT_ROWS
    past = ck_ref.shape[1] if has_ctx else 0
    s_len = past + t_len
    lane = lax.broadcasted_iota(jnp.int32, (1, D_GROUP), 1)

    def scores(q_rows, dst):
        q = q_rows.astype(F32) * (DK ** -0.5 * LOG2E)
        stack = jnp.concatenate(
            [jnp.where((lane // DK) == hm, q, 0.0).astype(BF16) for hm in range(2 * N_ATT_HEADS)],
            axis=0)
        dst[...] = jnp.dot(stack, kt[...], preferred_element_type=F32)

    @pl.when(pl.program_id(1) == 0)
    def _():
        if has_ctx:
            heads = range(N_ATT_HEADS)
            ck = jnp.concatenate([ck_ref[h] for h in heads], axis=-1)
            cv = jnp.concatenate([cv_ref[h] for h in heads], axis=-1)
            kt[:, 0:past] = ck.T.astype(BF16)
        kt[:, past:s_len] = k_ref[...].astype(F32).T.astype(BF16)
        v = v_ref[...]
        for h in range(N_ATT_HEADS):
            hm = (lane // V_DIM) == h
            if has_ctx:
                vs[h * s_len:h * s_len + past, :] = jnp.where(hm, cv, 0.0).astype(BF16)
            vs[h * s_len + past:(h + 1) * s_len, :] = jnp.where(hm, v, jnp.zeros_like(v))
        scores(q_ref[0:rows, :], s0)

    dl = dl_ref[...]
    lam = (jnp.exp(jnp.sum(dl[0:1] * dl[1:2], axis=-1, keepdims=True))
           - jnp.exp(jnp.sum(dl[2:3] * dl[3:4], axis=-1, keepdims=True)) + lam_init)

    def finish(src, out_lo):
        inv_l1 = jnp.zeros((rows, D_GROUP), F32)
        for h in range(N_ATT_HEADS):
            es, ls = [], []
            for m in range(2):
                lo = (2 * h + m) * rows
                s = src[lo:lo + rows, :]
                e = jnp.exp2(s - jnp.max(s, axis=-1, keepdims=True))
                es.append(e)
                ls.append(jnp.sum(e, axis=-1, keepdims=True))
            beta = lam * ls[0] / ls[1]
            pb[:, h * s_len:(h + 1) * s_len] = (es[0] - beta * es[1]).astype(BF16)
            inv_l1 = jnp.where((lane // V_DIM) == h, 1.0 / ls[0], inv_l1)
        acc = jnp.dot(pb[...], vs[...], preferred_element_type=F32) * inv_l1
        ms = jnp.dot(acc * acc, g_ref[...], preferred_element_type=F32,
                     precision=lax.Precision.HIGHEST)
        o_ref[out_lo:out_lo + rows, :] = (
            acc * lax.rsqrt(ms + EPS) * sw_ref[...] * (1.0 - lam_init)).astype(BF16)

    scores(q_ref[rows:2 * rows, :], s1)
    finish(s0, 0)
    if n_steps > 1:
        scores(qn_ref[0:rows, :], s0)
    finish(s1, rows)


def _attn(z, ctx, layer, dl, subw, gmat, lam_init):
    b, t, _ = z.shape
    has_ctx = ctx is not None
    past = ctx[0].shape[3] if has_ctx else 0
    s_len = past + t
    tq = 2 * ATT_ROWS
    n_steps = t // tq
    in_specs = [
        pl.BlockSpec((None, tq, D_GROUP), lambda i, j: (i, j, 0)),
        pl.BlockSpec((None, tq, D_GROUP), lambda i, j: (i, jnp.minimum(j + 1, n_steps - 1), 0)),
        pl.BlockSpec((None, t, D_GROUP), lambda i, j: (i, 0, 1)),
        pl.BlockSpec((None, t, D_GROUP), lambda i, j: (i, 0, 2)),
    ]
    args = [z, z, z, z]
    if has_ctx:
        in_specs += [pl.BlockSpec((None, None, N_ATT_HEADS, past, V_DIM),
                                  lambda i, j: (i, layer, 0, 0, 0))] * 2
        args += list(ctx)
    in_specs += [
        _resident((None, 4, DK), lambda i, j: (layer, 0, 0)),
        _resident((None, 1, D_GROUP), lambda i, j: (layer, 0, 0)),
        _resident((D_GROUP, D_GROUP), lambda i, j: (0, 0)),
    ]
    args += [dl, subw, gmat]
    n_hm = 2 * N_ATT_HEADS
    return pl.pallas_call(
        functools.partial(_attn_kernel, has_ctx=has_ctx, lam_init=lam_init, t_len=t,
                          n_steps=n_steps),
        grid=(b, n_steps),
        in_specs=in_specs,
        out_specs=pl.BlockSpec((None, tq, D_GROUP), lambda i, j: (i, j, 0)),
        out_shape=jax.ShapeDtypeStruct((b, t, D_GROUP), BF16),
        scratch_shapes=[pltpu.VMEM((D_GROUP, s_len), BF16),
                        pltpu.VMEM((N_ATT_HEADS * s_len, D_GROUP), BF16),
                        pltpu.VMEM((n_hm * ATT_ROWS, s_len), F32),
                        pltpu.VMEM((n_hm * ATT_ROWS, s_len), F32),
                        pltpu.VMEM((ATT_ROWS, N_ATT_HEADS * s_len), BF16)],
        compiler_params=_params("arbitrary", "arbitrary"),
        name="diff_attn",
    )(*args)


RG_ROWS = 256
RG_BLOCK = 64


def _shift_rows(xe, k):
    n = xe.shape[0]
    y = xe if k == 0 else pltpu.roll(xe, (-k) % n, 0)
    return y[HALO:n - HALO]


def _strided8(ref, half, start):
    return ref[half, pl.ds(start, SUBLANES, stride=SUBLANES), :]


def _rg_gates(xc, wg, bg, c_dir):
    g = jnp.dot(xc.astype(BF16), wg, preferred_element_type=F32) + bg
    r = jax.nn.sigmoid(g[:, :D_GROUP])
    i = jax.nn.sigmoid(g[:, D_GROUP:])
    a = jnp.exp(r * c_dir)
    om = 1.0 - a * a
    b = jnp.where(om > 0.0, om * lax.rsqrt(om), 0.0) * i * xc
    return a, b


def _scan_block(a, b, hp, reverse):
    n = SUBLANES
    order = list(range(n - 1, -1, -1)) if reverse else list(range(n))
    hs, cum = [None] * n, [None] * n
    prev = None
    for i in order:
        if prev is None:
            hs[i], cum[i] = b[i], a[i]
        else:
            hs[i], cum[i] = a[i] * hs[prev] + b[i], a[i] * cum[prev]
        prev = i
    p, f = cum[prev], hs[prev]
    row = lax.broadcasted_iota(jnp.int32, (n, LANES), 0)
    for d in (1, 2, 4):
        m = (row < n - d) if reverse else (row >= d)
        sh = n - d if reverse else d
        f = jnp.where(m, p * pltpu.roll(f, sh, 0) + f, f)
        p = jnp.where(m, p * pltpu.roll(p, sh, 0), p)
    first, last = (n - 1, 0) if reverse else (0, n - 1)
    sh = n - 1 if reverse else 1
    cin = jnp.where(row == first, hp, pltpu.roll(f, sh, 0) + pltpu.roll(p, sh, 0) * hp)
    out = [hs[i] + cum[i] * cin for i in range(n)]
    bc = lambda v: jnp.broadcast_to(v[last:last + 1, :], (n, LANES))
    return out, bc(f) + bc(p) * hp


def _rg_kernel(xr_ref, gr_ref, h0_ref, cw_ref, cb_ref, wg_ref, bg_ref, lam_ref,
               y_ref, last_ref, xs, gs, ys, xc_s, hf_s, *, t_len):
    n_chunks = t_len // RG_ROWS
    n_blk = RG_ROWS // RG_BLOCK
    zeros_halo = jnp.zeros((HALO, LANES), F32)
    for half in range(2):
        lo = half * LANES
        xs[half, 0:HALO, :] = zeros_halo
        xs[half, HALO + t_len:, :] = zeros_halo
        xs[half, HALO:HALO + t_len, :] = xr_ref[:, lo:lo + LANES].astype(F32)
        gs[half, :, :] = gr_ref[:, lo:lo + LANES].astype(F32)

    neg = -lam_ref[...]
    softplus = jnp.maximum(neg, 0.0) + jnp.log1p(jnp.exp(-jnp.abs(neg)))
    c_all = -RG_C * softplus
    cw = cw_ref[...]
    cb = cb_ref[...]

    def piece(v, blk, i, half):
        r0 = blk * RG_BLOCK + i * SUBLANES
        return v[r0:r0 + SUBLANES, half * LANES:(half + 1) * LANES]

    def scan_chunk(a, b, carry, reverse):
        carry = list(carry)
        out = {}
        for blk in (reversed(range(n_blk)) if reverse else range(n_blk)):
            for half in range(2):
                hs, carry[half] = _scan_block([piece(a, blk, i, half) for i in range(SUBLANES)],
                                              [piece(b, blk, i, half) for i in range(SUBLANES)],
                                              carry[half], reverse)
                for i in range(SUBLANES):
                    out[blk, i, half] = hs[i]
        rows = [jnp.concatenate([out[blk, i, 0], out[blk, i, 1]], axis=-1)
                for blk in range(n_blk) for i in range(SUBLANES)]
        return jnp.concatenate(rows, axis=0), tuple(carry)

    def start_state(d):
        return tuple(jnp.broadcast_to(h0_ref[d:d + 1, half * LANES:(half + 1) * LANES],
                                      (SUBLANES, LANES)) for half in range(2))

    def fwd_body(c, carry):
        s = pl.multiple_of(c * RG_ROWS, RG_ROWS)
        rows = []
        for blk in range(n_blk):
            base = s + blk * RG_BLOCK + HALO
            halves = []
            for half in range(2):
                lo = half * LANES
                x = {i: _strided8(xs, half, base + i) for i in range(-2, SUBLANES + 1)}
                halves.append([cw[0:1, lo:lo + LANES] * x[i - 2] + cw[1:2, lo:lo + LANES] * x[i - 1]
                               + cw[2:3, lo:lo + LANES] * x[i] + cw[3:4, lo:lo + LANES] * x[i + 1]
                               + cb[:, lo:lo + LANES] for i in range(SUBLANES)])
            rows += [jnp.concatenate([halves[0][i], halves[1][i]], axis=-1) for i in range(SUBLANES)]
        xc = jnp.concatenate(rows, axis=0)
        xc_s[pl.ds(s, RG_ROWS), :] = xc
        a, b = _rg_gates(xc, wg_ref[:, 0:2 * D_GROUP], bg_ref[:, 0:2 * D_GROUP], c_all[0:1])
        hf, carry = scan_chunk(a, b, carry, False)
        hf_s[pl.ds(s, RG_ROWS), :] = hf
        return carry

    h_f = lax.fori_loop(0, n_chunks, fwd_body, start_state(0))
    last_ref[0:1, :] = jnp.concatenate([h_f[0][0:1], h_f[1][0:1]], axis=-1)

    def bwd_body(ci, carry):
        c = n_chunks - 1 - ci
        s = pl.multiple_of(c * RG_ROWS, RG_ROWS)
        xc = xc_s[pl.ds(s, RG_ROWS), :]
        a, b = _rg_gates(xc, wg_ref[:, 2 * D_GROUP:], bg_ref[:, 2 * D_GROUP:], c_all[1:2])
        hb, carry = scan_chunk(a, b, carry, True)
        gr = jnp.concatenate(
            [jnp.concatenate([_strided8(gs, half, s + blk * RG_BLOCK + i) for half in range(2)], axis=-1)
             for blk in range(n_blk) for i in range(SUBLANES)], axis=0)
        gelu = 0.5 * gr * (1.0 + jnp.tanh(math.sqrt(2.0 / math.pi) * (gr + 0.044715 * gr * gr * gr)))
        y = (hf_s[pl.ds(s, RG_ROWS), :] + hb) * gelu
        for blk in range(n_blk):
            for i in range(SUBLANES):
                for half in range(2):
                    ys[half, pl.ds(s + blk * RG_BLOCK + i, SUBLANES, stride=SUBLANES), :] = (
                        piece(y, blk, i, half))
        y_ref[pl.ds(s, RG_ROWS), :] = jnp.concatenate(
            [ys[0, pl.ds(s, RG_ROWS), :], ys[1, pl.ds(s, RG_ROWS), :]], axis=-1).astype(BF16)
        return carry

    h_b = lax.fori_loop(0, n_chunks, bwd_body, start_state(1))
    last_ref[1:2, :] = jnp.concatenate([h_b[0][0:1], h_b[1][0:1]], axis=-1)


def _rg(z, h0, layer, cw, cb, wg, bg, lam):
    b, t, _ = z.shape
    slab = lambda rows: pltpu.VMEM((2, rows, LANES), F32)
    if h0.ndim == 4:
        h0_spec = pl.BlockSpec((None, None, 2, D_GROUP), lambda i: (i, layer, 0, 0))
    else:
        h0_spec = pl.BlockSpec((None, 2, D_GROUP), lambda i: (i, 0, 0))
    return pl.pallas_call(
        functools.partial(_rg_kernel, t_len=t),
        grid=(b,),
        in_specs=[
            pl.BlockSpec((None, t, D_GROUP), lambda i: (i, 0, 6)),
            pl.BlockSpec((None, t, D_GROUP), lambda i: (i, 0, 7)),
            h0_spec,
            _resident((None, 4, D_GROUP), lambda i: (layer, 0, 0)),
            _resident((None, 1, D_GROUP), lambda i: (layer, 0, 0)),
            _resident((D_GROUP, 4 * D_GROUP), lambda i: (0, 0)),
            _resident((None, 1, 4 * D_GROUP), lambda i: (layer, 0, 0)),
            _resident((None, 2, D_GROUP), lambda i: (layer, 0, 0)),
        ],
        out_specs=[pl.BlockSpec((None, t, D_GROUP), lambda i: (i, 0, 0)),
                   pl.BlockSpec((None, 2, D_GROUP), lambda i: (i, 0, 0))],
        out_shape=[jax.ShapeDtypeStruct((b, t, D_GROUP), BF16),
                   jax.ShapeDtypeStruct((b, 2, D_GROUP), F32)],
        scratch_shapes=[slab(t + 2 * HALO), slab(t), slab(t),
                        pltpu.VMEM((t, D_GROUP), F32),
                        pltpu.VMEM((t, D_GROUP), F32)],
        compiler_params=_params("arbitrary"),
        name="rglru",
    )(z, z, h0, cw, cb, wg, bg, lam)


CP_CHUNK = 128


def _convpool_kernel(gb_ref, gc_ref, xb_ref, xp_ref, cw_ref, pw_ref, ps_ref,
                     yb_ref, yd_ref, upad, ppad, *, t_len):
    r_rows = CP_CHUNK
    n_chunks = t_len // r_rows
    zeros_halo = jnp.zeros((HALO, D_GROUP), F32)
    for pad in (upad, ppad):
        pad[0:HALO, :] = zeros_halo
        pad[HALO + t_len:, :] = zeros_halo

    def fill(c, carry):
        s = pl.multiple_of(c * r_rows, r_rows)
        upad[pl.ds(HALO + s, r_rows), :] = (gc_ref[pl.ds(s, r_rows), :].astype(F32)
                                            * xb_ref[pl.ds(s, r_rows), :].astype(F32))
        ppad[pl.ds(HALO + s, r_rows), :] = xp_ref[pl.ds(s, r_rows), :].astype(F32)
        return carry

    lax.fori_loop(0, n_chunks, fill, 0)

    cw = cw_ref[...]
    grp = lax.broadcasted_iota(jnp.int32, (r_rows, D_GROUP), 1) // POOL_GW
    win = jnp.left_shift(2, grp)
    left = win // 2
    right = win - 1 - left
    row = lax.broadcasted_iota(jnp.int32, (r_rows, D_GROUP), 0)

    def body(c, carry):
        s = pl.multiple_of(c * r_rows, r_rows)
        ue = upad[pl.ds(s, r_rows + 2 * HALO), :]
        conv = (cw[0:1] * _shift_rows(ue, -1) + cw[1:2] * _shift_rows(ue, 0)
                + cw[2:3] * _shift_rows(ue, 1))
        yb_ref[pl.ds(s, r_rows), :] = (gb_ref[pl.ds(s, r_rows), :].astype(F32) * conv).astype(BF16)

        pe = ppad[pl.ds(s, r_rows + 2 * HALO), :]
        q2 = pe + pltpu.roll(pe, 1, 0)
        q4 = q2 + pltpu.roll(q2, 2, 0)
        q8 = q4 + pltpu.roll(q4, 4, 0)
        q16 = q8 + pltpu.roll(q8, 8, 0)
        sums = jnp.where(grp == 0, _shift_rows(q2, 0),
                         jnp.where(grp == 1, _shift_rows(q4, 1),
                                   jnp.where(grp == 2, _shift_rows(q8, 3), _shift_rows(q16, 7))))
        t = row + s
        cnt = (jnp.minimum(t + right, t_len - 1) - jnp.maximum(t - left, 0) + 1).astype(F32)
        pm = sums / cnt - _shift_rows(pe, 0)
        yd_ref[pl.ds(s, r_rows), :] = (
            jnp.dot(pm.astype(BF16), pw_ref[...], preferred_element_type=F32) * ps_ref[...]).astype(BF16)
        return carry

    lax.fori_loop(0, n_chunks, body, 0)


def _convpool(z, layer, cw, pw, ps):
    b, t, _ = z.shape
    col = lambda k: pl.BlockSpec((None, t, D_GROUP), lambda i: (i, 0, k))
    out = pl.BlockSpec((None, t, D_GROUP), lambda i: (i, 0, 0))
    return pl.pallas_call(
        functools.partial(_convpool_kernel, t_len=t),
        grid=(b,),
        in_specs=[col(3), col(4), col(5), col(8),
                  _resident((None, 3, D_GROUP), lambda i: (layer, 0, 0)),
                  _resident((D_GROUP, D_GROUP), lambda i: (0, 0)),
                  _resident((None, 1, D_GROUP), lambda i: (layer, 0, 0))],
        out_specs=[out, out],
        out_shape=[jax.ShapeDtypeStruct((b, t, D_GROUP), BF16)] * 2,
        scratch_shapes=[pltpu.VMEM((t + 2 * HALO, D_GROUP), F32)] * 2,
        compiler_params=_params("arbitrary"),
        name="convpool",
    )(z, z, z, z, cw, pw, ps)


FF_CHUNK = 1024


def _mlp_kernel(x_ref, ya_ref, yb_ref, yc_ref, yd_ref, mod_ref, nw_ref, wo_ref, w1_ref, w2_ref,
                fw_ref, o_ref, *, final):
    x = x_ref[...]
    y = jnp.zeros(x.shape, F32)
    for i, r in enumerate((ya_ref, yb_ref, yc_ref, yd_ref)):
        y = y + jnp.dot(r[...], wo_ref[i * D_GROUP:(i + 1) * D_GROUP, :],
                        preferred_element_type=F32)
    gate1 = mod_ref[:, 2 * D_MODEL:3 * D_MODEL]
    shift2 = mod_ref[:, 3 * D_MODEL:4 * D_MODEL]
    scale2 = mod_ref[:, 4 * D_MODEL:5 * D_MODEL]
    gate2 = mod_ref[:, 5 * D_MODEL:6 * D_MODEL]
    x1 = x + gate1 * y
    hn = (_rms(x1, nw_ref[...]) * (1.0 + scale2) + shift2).astype(BF16)
    acc = jnp.zeros(x.shape, F32)
    for f in range(D_FF // FF_CHUNK):
        u = jnp.dot(hn, w1_ref[:, f * FF_CHUNK:(f + 1) * FF_CHUNK], preferred_element_type=F32)
        u = jnp.square(jnp.maximum(u, 0.0)).astype(BF16)
        acc = acc + jnp.dot(u, w2_ref[f * FF_CHUNK:(f + 1) * FF_CHUNK, :], preferred_element_type=F32)
    x2 = x1 + gate2 * acc
    if final:
        x2 = _rms(x2, fw_ref[...])
    o_ref[...] = x2


def _mlp(x, ys, mod, layer, mod_row, norm_w, wo, w1, w2, fw, final, tt, shared_mod=False):
    b, t, _ = x.shape
    if shared_mod:
        x, ys = x.reshape(1, b * t, D_MODEL), [y.reshape(1, b * t, D_GROUP) for y in ys]
    nb, nt, _ = x.shape
    ytile = pl.BlockSpec((None, tt, D_GROUP), lambda i, j: (i, j, 0))
    out = pl.pallas_call(
        functools.partial(_mlp_kernel, final=final),
        grid=(nb, nt // tt),
        in_specs=[
            pl.BlockSpec((None, tt, D_MODEL), lambda i, j: (i, j, 0)),
            ytile, ytile, ytile, ytile,
            pl.BlockSpec((None, None, 1, N_MOD * D_MODEL), lambda i, j: (layer, mod_row(i), 0, 0)),
            _resident((None, None, 1, D_MODEL), lambda i, j: (layer, 1, 0, 0)),
            _resident((None, D_MODEL, D_MODEL), lambda i, j: (layer, 0, 0)),
            _resident((None, D_MODEL, D_FF), lambda i, j: (layer, 0, 0)),
            _resident((None, D_FF, D_MODEL), lambda i, j: (layer, 0, 0)),
            _resident((1, D_MODEL), lambda i, j: (0, 0)),
        ],
        out_specs=pl.BlockSpec((None, tt, D_MODEL), lambda i, j: (i, j, 0)),
        out_shape=jax.ShapeDtypeStruct((nb, nt, D_MODEL), F32),
        compiler_params=_params("arbitrary", "arbitrary"),
        name="outproj_mlp",
    )(x, *ys, mod, norm_w, wo, w1, w2, fw)
    return out.reshape(b, t, D_MODEL)


def _block_diag(w):
    *lead, n, k, _ = w.shape
    eye = jnp.eye(n, dtype=w.dtype)
    return (eye[:, None, :, None] * w[..., :, :, None, :]).reshape(*lead, n * k, n * k)


def _rope_tables(t_len):
    half = DK // 4
    freqs = ROPE_BASE ** (-jnp.arange(half, dtype=F32) / half)
    lane = np.arange(LANES)
    m = lane % DK
    use_col = m >= DK // 2
    fidx = m % half
    t = jnp.arange(t_len)
    pos = jnp.where(use_col[None, :], (t % GRID_W)[:, None], (t // GRID_W)[:, None]).astype(F32)
    ang = pos * freqs[fidx][None, :]
    sign = np.where((lane % (2 * half)) < half, -1.0, 1.0).astype(np.float32)
    return jnp.cos(ang), jnp.sin(ang) * sign[None, :]


def kernel(x_prompt, x_sample, cache_k, cache_v, state_rglru, c, c_ctx, w_ada, b_ada, norm_w, w_in, diff_lambda, subln_w, conv_b_w, conv_c_w, conv_c_b, rg_w, rg_b, rg_lambda, pool_w, pool_scale, w_out, w_mlp1, w_mlp2, final_norm_w):
    n_lat = c.shape[0]
    ctx_row = n_lat
    cc = jnp.zeros((MOD_ROWS, D_MODEL), F32).at[:n_lat].set(c).at[ctx_row].set(c_ctx)
    mod = _ada(cc, w_ada, b_ada).reshape(DEPTH, MOD_ROWS, 1, N_MOD * D_MODEL)

    w_in_b = w_in.astype(BF16)
    w_out_b = w_out.astype(BF16)
    w1_b = w_mlp1.astype(BF16)
    w2_b = w_mlp2.astype(BF16)
    norm_w4 = norm_w.reshape(DEPTH, 2, 1, D_MODEL)
    rope_tabs = _rope_tables(x_sample.shape[1])
    gmat = _block_diag(jnp.full((N_ATT_HEADS, V_DIM, V_DIM), 1.0 / V_DIM, F32))
    subw = jnp.tile(subln_w, (1, N_ATT_HEADS)).reshape(DEPTH, 1, D_GROUP)
    wg = _block_diag(rg_w).transpose(0, 3, 1, 2, 4).reshape(DEPTH, D_GROUP, 4 * D_GROUP).astype(BF16)
    bg = rg_b.reshape(DEPTH, 1, 4 * D_GROUP)
    pw = _block_diag(pool_w).astype(BF16)
    ps = pool_scale.reshape(DEPTH, 1, D_GROUP)
    cb = conv_c_b.reshape(DEPTH, 1, D_GROUP)
    fw = final_norm_w.reshape(1, D_MODEL)
    zero_state = jnp.zeros((x_prompt.shape[0], 2, D_GROUP), F32)

    xp, xs = x_prompt, x_sample
    kv_out, hs_out = [], []
    for l in range(DEPTH):
        lam_init = 0.8 - 0.6 * math.exp(-0.3 * l)
        final = l == DEPTH - 1

        def mixers(z, ctx, h0):
            ya = _attn(z, ctx, l, diff_lambda, subw, gmat, lam_init)
            yc, last = _rg(z, h0, l, conv_c_w, cb, wg[l], bg, rg_lambda)
            yb, yd = _convpool(z, l, conv_b_w, pw[l], ps)
            return (ya, yb, yc, yd), last

        ctx_rows = lambda i: ctx_row
        kv_prev = kv_out if final else ()
        zp, k_new, v_new = _inproj(xp, mod, l, ctx_rows, norm_w4, w_in_b, None, TOKEN_TILE, kv_prev)
        ys, last_p = mixers(zp, None, zero_state)
        xp = _mlp(xp, ys, mod, l, ctx_rows, norm_w4, w_out_b, w1_b, w2_b, fw, final, TOKEN_TILE,
                  shared_mod=True)
        kv_out += [k_new, v_new]
        hs_out.append(last_p)

        lat_rows = lambda i: i
        (zs,) = _inproj(xs, mod, l, lat_rows, norm_w4, w_in_b, rope_tabs, TOKEN_TILE)
        ys, _ = mixers(zs, (cache_k, cache_v), state_rglru)
        xs = _mlp(xs, ys, mod, l, lat_rows, norm_w4, w_out_b, w1_b, w2_b, fw, final, TOKEN_TILE)

    return (xp, xs, kv_out[-2], kv_out[-1], jnp.stack(hs_out, axis=1))
```

```python
import functools
import math

import jax
import jax.numpy as jnp
import numpy as np
from jax import lax
from jax.experimental import pallas as pl
from jax.experimental.pallas import tpu as pltpu

D_MODEL = 1024
DEPTH = 2
GRID_W = 64
D_GROUP = 256
N_ATT_HEADS = 4
V_DIM = 64
DK = 32
ROPE_BASE = 10000.0
RG_BLOCKS = 4
RG_BW = 64
RG_C = 8.0
POOL_WINDOWS = (2, 4, 8, 16)
POOL_GW = 64
D_FF = 4 * D_MODEL
N_MOD = 6
D_IN = 9 * D_GROUP
EPS = 1e-6

LANES = 128
SUBLANES = 8
HALO = 8
VMEM_LIMIT = 56 * 1024 * 1024
MOD_ROWS = 16
TOKEN_TILE = 512

BF16 = jnp.bfloat16
F32 = jnp.float32


def _params(*sem):
    return pltpu.CompilerParams(dimension_semantics=sem, vmem_limit_bytes=VMEM_LIMIT)


def _resident(shape, index_map):
    return pl.BlockSpec(shape, index_map, pipeline_mode=pl.Buffered(1))


def _rms(x, w):
    ms = jnp.mean(x * x, axis=-1, keepdims=True)
    return x * lax.rsqrt(ms + EPS) * w


def _ada_kernel(c_ref, w_ref, b_ref, o_ref):
    c = c_ref[...]
    s = (c * jax.nn.sigmoid(c)).astype(BF16)
    o_ref[...] = jnp.dot(s, w_ref[...].astype(BF16), preferred_element_type=F32) + b_ref[...]


def _ada(cc, w_ada, b_ada):
    tn = 1536
    n_out = N_MOD * D_MODEL
    return pl.pallas_call(
        _ada_kernel,
        grid=(DEPTH, n_out // tn),
        in_specs=[
            pl.BlockSpec((MOD_ROWS, D_MODEL), lambda l, n: (0, 0)),
            pl.BlockSpec((None, D_MODEL, tn), lambda l, n: (l, 0, n)),
            pl.BlockSpec((None, 1, tn), lambda l, n: (l, 0, n)),
        ],
        out_specs=pl.BlockSpec((None, MOD_ROWS, tn), lambda l, n: (l, 0, n)),
        out_shape=jax.ShapeDtypeStruct((DEPTH, MOD_ROWS, n_out), F32),
        compiler_params=_params("arbitrary", "arbitrary"),
        name="ada_mod",
    )(cc, w_ada, b_ada.reshape(DEPTH, 1, n_out))


def _inproj_kernel(*refs, rope, n_prev, seq_len):
    if rope:
        x_ref, mod_ref, nw_ref, w_ref, cos_ref, sin_ref, z_ref = refs
    else:
        x_ref, mod_ref, nw_ref, w_ref = refs[:4]
        prev_refs = refs[4:4 + 2 * n_prev]
        z_ref, ko_ref, vo_ref = refs[4 + 2 * n_prev:]
    x = x_ref[...]
    shift = mod_ref[:, 0:D_MODEL]
    scale = mod_ref[:, D_MODEL:2 * D_MODEL]
    h = _rms(x, nw_ref[...]) * (1.0 + scale) + shift
    z = jnp.dot(h.astype(BF16), w_ref[...], preferred_element_type=F32)
    if not rope:
        z_ref[...] = z.astype(BF16)
        for sq in range(x.shape[0] // seq_len):
            r0 = sq * seq_len
            if n_prev:
                for lp in range(n_prev):
                    ko_ref[sq, lp] = prev_refs[2 * lp][sq]
                    vo_ref[sq, lp] = prev_refs[2 * lp + 1][sq]
                ko_l, vo_l = ko_ref.at[sq, n_prev], vo_ref.at[sq, n_prev]
            else:
                ko_l, vo_l = ko_ref.at[sq], vo_ref.at[sq]
            for hd in range(N_ATT_HEADS):
                c0 = D_GROUP + hd * V_DIM
                ko_l[hd] = z[r0:r0 + seq_len, c0:c0 + V_DIM]
                vo_l[hd] = z[r0:r0 + seq_len, D_GROUP + c0:D_GROUP + c0 + V_DIM]
        return
    cos = cos_ref[...]
    sin = sin_ref[...]
    first_half = (lax.broadcasted_iota(jnp.int32, (1, LANES), 1) % 16) < 8
    for j in range(2 * D_GROUP // LANES):
        zc = z[:, j * LANES:(j + 1) * LANES]
        partner = jnp.where(first_half,
                            pltpu.roll(zc, LANES - 8, 1),
                            pltpu.roll(zc, 8, 1))
        z_ref[:, j * LANES:(j + 1) * LANES] = (zc * cos + partner * sin).astype(BF16)
    z_ref[:, 2 * D_GROUP:] = z[:, 2 * D_GROUP:].astype(BF16)


def _inproj(x, mod, layer, mod_row, norm_w, w_in, rope_tabs, tt, kv_prev=()):
    b, t, _ = x.shape
    rope = rope_tabs is not None
    n_prev = len(kv_prev) // 2
    if rope:
        grid = (b, t // tt)
        tok = lambda i, j: (i, j, 0)
        x3 = x
    else:
        grid = (1, b * t // tt)
        tok = lambda i, j: (0, j, 0)
        x3 = x.reshape(1, b * t, D_MODEL)
    in_specs = [
        pl.BlockSpec((None, tt, D_MODEL), tok),
        pl.BlockSpec((None, None, 1, N_MOD * D_MODEL), lambda i, j: (layer, mod_row(i), 0, 0)),
        _resident((None, None, 1, D_MODEL), lambda i, j: (layer, 0, 0, 0)),
        _resident((None, D_MODEL, D_IN), lambda i, j: (layer, 0, 0)),
    ]
    args = [x3, mod, norm_w, w_in]
    out_specs = [pl.BlockSpec((None, tt, D_IN), tok)]
    out_shape = [jax.ShapeDtypeStruct(x3.shape[:2] + (D_IN,), BF16)]
    if rope:
        in_specs += [pl.BlockSpec((tt, LANES), lambda i, j: (j, 0))] * 2
        args += list(rope_tabs)
    else:
        n_seq = tt // t
        per_layer = pl.BlockSpec((n_seq, N_ATT_HEADS, t, V_DIM), lambda i, j: (j, 0, 0, 0))
        in_specs += [per_layer] * (2 * n_prev)
        args += list(kv_prev)
        if n_prev:
            out_specs += [pl.BlockSpec((n_seq, n_prev + 1, N_ATT_HEADS, t, V_DIM),
                                       lambda i, j: (j, 0, 0, 0, 0))] * 2
            out_shape += [jax.ShapeDtypeStruct((b, n_prev + 1, N_ATT_HEADS, t, V_DIM), F32)] * 2
        else:
            out_specs += [per_layer] * 2
            out_shape += [jax.ShapeDtypeStruct((b, N_ATT_HEADS, t, V_DIM), F32)] * 2
    outs = pl.pallas_call(
        functools.partial(_inproj_kernel, rope=rope, n_prev=n_prev, seq_len=t),
        grid=grid,
        in_specs=in_specs,
        out_specs=out_specs,
        out_shape=out_shape,
        compiler_params=_params("arbitrary", "arbitrary"),
        name="inproj",
    )(*args)
    return [outs[0].reshape(b, t, D_IN)] + list(outs[1:])


ATT_ROWS = 128
LOG2E = math.log2(math.e)


def _attn_kernel(*refs, has_ctx, lam_init, t_len, n_steps):
    if has_ctx:
        (q_ref, qn_ref, k_ref, v_ref, ck_ref, cv_ref, dl_ref, sw_ref, g_ref, o_ref,
         kt, vs, s0, s1, pb) = refs
    else:
        q_ref, qn_ref, k_ref, v_ref, dl_ref, sw_ref, g_ref, o_ref, kt, vs, s0, s1, pb = refs
    rows = ATT_ROWS
    past = ck_ref.shape[1] if has_ctx else 0
    s_len = past + t_len
    lane = lax.broadcasted_iota(jnp.int32, (1, D_GROUP), 1)

    def scores(q_rows, dst):
        q = q_rows.astype(F32) * (DK ** -0.5 * LOG2E)
        stack = jnp.concatenate(
            [jnp.where((lane // DK) == hm, q, 0.0).astype(BF16) for hm in range(2 * N_ATT_HEADS)],
            axis=0)
        dst[...] = jnp.dot(stack, kt[...], preferred_element_type=F32)

    @pl.when(pl.program_id(1) == 0)
    def _():
        if has_ctx:
            heads = range(N_ATT_HEADS)
            ck = jnp.concatenate([ck_ref[h] for h in heads], axis=-1)
            cv = jnp.concatenate([cv_ref[h] for h in heads], axis=-1)
            kt[:, 0:past] = ck.T.astype(BF16)
            vs[0:past, :] = cv.astype(BF16)
        kt[:, past:s_len] = k_ref[...].astype(F32).T.astype(BF16)
        vs[past:s_len, :] = v_ref[...]
        scores(q_ref[0:rows, :], s0)

    dl = dl_ref[...]
    lam = (jnp.exp(jnp.sum(dl[0:1] * dl[1:2], axis=-1, keepdims=True))
           - jnp.exp(jnp.sum(dl[2:3] * dl[3:4], axis=-1, keepdims=True)) + lam_init)

    def finish(src, out_lo):
        inv_l1 = jnp.zeros((rows, D_GROUP), F32)
        for h in range(N_ATT_HEADS):
            es, ls = [], []
            for m in range(2):
                lo = (2 * h + m) * rows
                s = src[lo:lo + rows, :]
                e = jnp.exp2(s - jnp.max(s, axis=-1, keepdims=True))
                es.append(e)
                ls.append(jnp.sum(e, axis=-1, keepdims=True))
            beta = lam * ls[0] / ls[1]
            pb[h * rows:(h + 1) * rows, :] = (es[0] - beta * es[1]).astype(BF16)
            inv_l1 = jnp.where((lane // V_DIM) == h, 1.0 / ls[0], inv_l1)
        full = jnp.dot(pb[...], vs[...], preferred_element_type=F32)
        acc = full[0:rows]
        for h in range(1, N_ATT_HEADS):
            acc = jnp.where((lane // V_DIM) == h, full[h * rows:(h + 1) * rows], acc)
        acc = acc * inv_l1
        ms = jnp.dot(acc * acc, g_ref[...], preferred_element_type=F32,
                     precision=lax.Precision.HIGHEST)
        o_ref[out_lo:out_lo + rows, :] = (
            acc * lax.rsqrt(ms + EPS) * sw_ref[...] * (1.0 - lam_init)).astype(BF16)

    scores(q_ref[rows:2 * rows, :], s1)
    finish(s0, 0)
    if n_steps > 1:
        scores(qn_ref[0:rows, :], s0)
    finish(s1, rows)


def _attn(z, ctx, layer, dl, subw, gmat, lam_init):
    b, t, _ = z.shape
    has_ctx = ctx is not None
    past = ctx[0].shape[3] if has_ctx else 0
    s_len = past + t
    tq = 2 * ATT_ROWS
    n_steps = t // tq
    in_specs = [
        pl.BlockSpec((None, tq, D_GROUP), lambda i, j: (i, j, 0)),
        pl.BlockSpec((None, tq, D_GROUP), lambda i, j: (i, jnp.minimum(j + 1, n_steps - 1), 0)),
        pl.BlockSpec((None, t, D_GROUP), lambda i, j: (i, 0, 1)),
        pl.BlockSpec((None, t, D_GROUP), lambda i, j: (i, 0, 2)),
    ]
    args = [z, z, z, z]
    if has_ctx:
        in_specs += [pl.BlockSpec((None, None, N_ATT_HEADS, past, V_DIM),
                                  lambda i, j: (i, layer, 0, 0, 0))] * 2
        args += list(ctx)
    in_specs += [
        _resident((None, 4, DK), lambda i, j: (layer, 0, 0)),
        _resident((None, 1, D_GROUP), lambda i, j: (layer, 0, 0)),
        _resident((D_GROUP, D_GROUP), lambda i, j: (0, 0)),
    ]
    args += [dl, subw, gmat]
    n_hm = 2 * N_ATT_HEADS
    return pl.pallas_call(
        functools.partial(_attn_kernel, has_ctx=has_ctx, lam_init=lam_init, t_len=t,
                          n_steps=n_steps),
        grid=(b, n_steps),
        in_specs=in_specs,
        out_specs=pl.BlockSpec((None, tq, D_GROUP), lambda i, j: (i, j, 0)),
        out_shape=jax.ShapeDtypeStruct((b, t, D_GROUP), BF16),
        scratch_shapes=[pltpu.VMEM((D_GROUP, s_len), BF16),
                        pltpu.VMEM((s_len, D_GROUP), BF16),
                        pltpu.VMEM((n_hm * ATT_ROWS, s_len), F32),
                        pltpu.VMEM((n_hm * ATT_ROWS, s_len), F32),
                        pltpu.VMEM((N_ATT_HEADS * ATT_ROWS, s_len), BF16)],
        compiler_params=_params("arbitrary", "arbitrary"),
        name="diff_attn",
    )(*args)


RG_ROWS = 256
RG_BLOCK = 64


def _strided8(ref, half, start):
    return ref[half, pl.ds(start, SUBLANES, stride=SUBLANES), :]


def _rg_gates(xc, wg, bg, c_dir):
    g = jnp.dot(xc.astype(BF16), wg, preferred_element_type=F32) + bg
    r = jax.nn.sigmoid(g[:, :D_GROUP])
    i = jax.nn.sigmoid(g[:, D_GROUP:])
    a = jnp.exp(r * c_dir)
    om = 1.0 - a * a
    b = jnp.where(om > 0.0, om * lax.rsqrt(om), 0.0) * i * xc
    return a, b


def _scan_block(a, b, hp, reverse):
    n = SUBLANES
    order = list(range(n - 1, -1, -1)) if reverse else list(range(n))
    hs, cum = [None] * n, [None] * n
    prev = None
    for i in order:
        if prev is None:
            hs[i], cum[i] = b[i], a[i]
        else:
            hs[i], cum[i] = a[i] * hs[prev] + b[i], a[i] * cum[prev]
        prev = i
    p, f = cum[prev], hs[prev]
    row = lax.broadcasted_iota(jnp.int32, (n, LANES), 0)
    for d in (1, 2, 4):
        m = (row < n - d) if reverse else (row >= d)
        sh = n - d if reverse else d
        f = jnp.where(m, p * pltpu.roll(f, sh, 0) + f, f)
        p = jnp.where(m, p * pltpu.roll(p, sh, 0), p)
    first, last = (n - 1, 0) if reverse else (0, n - 1)
    sh = n - 1 if reverse else 1
    cin = jnp.where(row == first, hp, pltpu.roll(f, sh, 0) + pltpu.roll(p, sh, 0) * hp)
    out = [hs[i] + cum[i] * cin for i in range(n)]
    bc = lambda v: jnp.broadcast_to(v[last:last + 1, :], (n, LANES))
    return out, bc(f) + bc(p) * hp


def _rg_kernel(xr_ref, gr_ref, h0_ref, cw_ref, cb_ref, wg_ref, bg_ref, lam_ref,
               y_ref, last_ref, xs, gs, ys, xc_s, hf_s, *, t_len):
    n_chunks = t_len // RG_ROWS
    n_blk = RG_ROWS // RG_BLOCK
    zeros_halo = jnp.zeros((HALO, LANES), F32)
    for half in range(2):
        lo = half * LANES
        xs[half, 0:HALO, :] = zeros_halo
        xs[half, HALO + t_len:, :] = zeros_halo
        xs[half, HALO:HALO + t_len, :] = xr_ref[:, lo:lo + LANES].astype(F32)
        gs[half, :, :] = gr_ref[:, lo:lo + LANES].astype(F32)

    neg = -lam_ref[...]
    softplus = jnp.maximum(neg, 0.0) + jnp.log1p(jnp.exp(-jnp.abs(neg)))
    c_all = -RG_C * softplus
    cw = cw_ref[...]
    cb = cb_ref[...]

    def piece(v, blk, i, half):
        r0 = blk * RG_BLOCK + i * SUBLANES
        return v[r0:r0 + SUBLANES, half * LANES:(half + 1) * LANES]

    def scan_chunk(a, b, carry, reverse):
        carry = list(carry)
        out = {}
        for blk in (reversed(range(n_blk)) if reverse else range(n_blk)):
            for half in range(2):
                hs, carry[half] = _scan_block([piece(a, blk, i, half) for i in range(SUBLANES)],
                                              [piece(b, blk, i, half) for i in range(SUBLANES)],
                                              carry[half], reverse)
                for i in range(SUBLANES):
                    out[blk, i, half] = hs[i]
        rows = [jnp.concatenate([out[blk, i, 0], out[blk, i, 1]], axis=-1)
                for blk in range(n_blk) for i in range(SUBLANES)]
        return jnp.concatenate(rows, axis=0), tuple(carry)

    def start_state(d):
        return tuple(jnp.broadcast_to(h0_ref[d:d + 1, half * LANES:(half + 1) * LANES],
                                      (SUBLANES, LANES)) for half in range(2))

    def fwd_body(c, carry):
        s = pl.multiple_of(c * RG_ROWS, RG_ROWS)
        rows = []
        for blk in range(n_blk):
            base = s + blk * RG_BLOCK + HALO
            halves = []
            for half in range(2):
                lo = half * LANES
                x = {i: _strided8(xs, half, base + i) for i in range(-2, SUBLANES + 1)}
                halves.append([cw[0:1, lo:lo + LANES] * x[i - 2] + cw[1:2, lo:lo + LANES] * x[i - 1]
                               + cw[2:3, lo:lo + LANES] * x[i] + cw[3:4, lo:lo + LANES] * x[i + 1]
                               + cb[:, lo:lo + LANES] for i in range(SUBLANES)])
            rows += [jnp.concatenate([halves[0][i], halves[1][i]], axis=-1) for i in range(SUBLANES)]
        xc = jnp.concatenate(rows, axis=0)
        xc_s[pl.ds(s, RG_ROWS), :] = xc
        a, b = _rg_gates(xc, wg_ref[:, 0:2 * D_GROUP], bg_ref[:, 0:2 * D_GROUP], c_all[0:1])
        hf, carry = scan_chunk(a, b, carry, False)
        hf_s[pl.ds(s, RG_ROWS), :] = hf
        return carry

    h_f = lax.fori_loop(0, n_chunks, fwd_body, start_state(0))
    last_ref[0:1, :] = jnp.concatenate([h_f[0][0:1], h_f[1][0:1]], axis=-1)

    def bwd_body(ci, carry):
        c = n_chunks - 1 - ci
        s = pl.multiple_of(c * RG_ROWS, RG_ROWS)
        xc = xc_s[pl.ds(s, RG_ROWS), :]
        a, b = _rg_gates(xc, wg_ref[:, 2 * D_GROUP:], bg_ref[:, 2 * D_GROUP:], c_all[1:2])
        hb, carry = scan_chunk(a, b, carry, True)
        gr = jnp.concatenate(
            [jnp.concatenate([_strided8(gs, half, s + blk * RG_BLOCK + i) for half in range(2)], axis=-1)
             for blk in range(n_blk) for i in range(SUBLANES)], axis=0)
        gelu = 0.5 * gr * (1.0 + jnp.tanh(math.sqrt(2.0 / math.pi) * (gr + 0.044715 * gr * gr * gr)))
        y = (hf_s[pl.ds(s, RG_ROWS), :] + hb) * gelu
        for blk in range(n_blk):
            for i in range(SUBLANES):
                for half in range(2):
                    ys[half, pl.ds(s + blk * RG_BLOCK + i, SUBLANES, stride=SUBLANES), :] = (
                        piece(y, blk, i, half))
        y_ref[pl.ds(s, RG_ROWS), :] = jnp.concatenate(
            [ys[0, pl.ds(s, RG_ROWS), :], ys[1, pl.ds(s, RG_ROWS), :]], axis=-1).astype(BF16)
        return carry

    h_b = lax.fori_loop(0, n_chunks, bwd_body, start_state(1))
    last_ref[1:2, :] = jnp.concatenate([h_b[0][0:1], h_b[1][0:1]], axis=-1)


def _rg(z, h0, layer, cw, cb, wg, bg, lam):
    b, t, _ = z.shape
    slab = lambda rows: pltpu.VMEM((2, rows, LANES), F32)
    if h0.ndim == 4:
        h0_spec = pl.BlockSpec((None, None, 2, D_GROUP), lambda i: (i, layer, 0, 0))
    else:
        h0_spec = pl.BlockSpec((None, 2, D_GROUP), lambda i: (i, 0, 0))
    return pl.pallas_call(
        functools.partial(_rg_kernel, t_len=t),
        grid=(b,),
        in_specs=[
            pl.BlockSpec((None, t, D_GROUP), lambda i: (i, 0, 6)),
            pl.BlockSpec((None, t, D_GROUP), lambda i: (i, 0, 7)),
            h0_spec,
            _resident((None, 4, D_GROUP), lambda i: (layer, 0, 0)),
            _resident((None, 1, D_GROUP), lambda i: (layer, 0, 0)),
            _resident((D_GROUP, 4 * D_GROUP), lambda i: (0, 0)),
            _resident((None, 1, 4 * D_GROUP), lambda i: (layer, 0, 0)),
            _resident((None, 2, D_GROUP), lambda i: (layer, 0, 0)),
        ],
        out_specs=[pl.BlockSpec((None, t, D_GROUP), lambda i: (i, 0, 0)),
                   pl.BlockSpec((None, 2, D_GROUP), lambda i: (i, 0, 0))],
        out_shape=[jax.ShapeDtypeStruct((b, t, D_GROUP), BF16),
                   jax.ShapeDtypeStruct((b, 2, D_GROUP), F32)],
        scratch_shapes=[slab(t + 2 * HALO), slab(t), slab(t),
                        pltpu.VMEM((t, D_GROUP), F32),
                        pltpu.VMEM((t, D_GROUP), F32)],
        compiler_params=_params("arbitrary"),
        name="rglru",
    )(z, z, h0, cw, cb, wg, bg, lam)


CP_ROWS = 256


def _window_sums(x, lo, hi, widths):
    out, prev, w = {}, {i: x[i] for i in range(lo, hi + 1)}, 1
    while w < max(widths):
        prev = {i: prev[i] + prev[i - w] for i in prev if i - w in prev}
        w *= 2
        out[w] = prev
    return out


def _convpool_kernel(gb_ref, gc_ref, xb_ref, xp_ref, cw_ref, pw_ref, ps_ref,
                     yb_ref, yd_ref, us, xs, cs, ds, *, t_len):
    n_chunks = t_len // CP_ROWS
    n_blk = CP_ROWS // RG_BLOCK
    zeros_halo = jnp.zeros((HALO, LANES), F32)
    for half in range(2):
        for pad in (us, xs):
            pad[half, 0:HALO, :] = zeros_halo
            pad[half, HALO + t_len:, :] = zeros_halo

    def fill(c, carry):
        s = pl.multiple_of(c * CP_ROWS, CP_ROWS)
        u = gc_ref[pl.ds(s, CP_ROWS), :].astype(F32) * xb_ref[pl.ds(s, CP_ROWS), :].astype(F32)
        x = xp_ref[pl.ds(s, CP_ROWS), :].astype(F32)
        for half in range(2):
            us[half, pl.ds(HALO + s, CP_ROWS), :] = u[:, half * LANES:(half + 1) * LANES]
            xs[half, pl.ds(HALO + s, CP_ROWS), :] = x[:, half * LANES:(half + 1) * LANES]
        return carry

    lax.fori_loop(0, n_chunks, fill, 0)

    cw = cw_ref[...]
    low_group = lax.broadcasted_iota(jnp.int32, (1, LANES), 1) < POOL_GW
    sub8 = lax.broadcasted_iota(jnp.int32, (SUBLANES, LANES), 0) * SUBLANES
    wins = [jnp.where(low_group, POOL_WINDOWS[2 * half], POOL_WINDOWS[2 * half + 1])
            for half in range(2)]

    def body(c, carry):
        s = pl.multiple_of(c * CP_ROWS, CP_ROWS)
        pm_rows = []
        for blk in range(n_blk):
            r0 = s + blk * RG_BLOCK
            halves = []
            for half in range(2):
                lo = half * LANES
                u = {i: _strided8(us, half, r0 + HALO + i) for i in range(-1, SUBLANES + 1)}
                for i in range(SUBLANES):
                    cs[half, pl.ds(r0 + i, SUBLANES, stride=SUBLANES), :] = (
                        cw[0:1, lo:lo + LANES] * u[i - 1] + cw[1:2, lo:lo + LANES] * u[i]
                        + cw[2:3, lo:lo + LANES] * u[i + 1])
                w_lo, w_hi = POOL_WINDOWS[2 * half], POOL_WINDOWS[2 * half + 1]
                p_lo, p_hi = -(w_hi // 2), SUBLANES - 1 + w_hi // 2 - 1
                x = {i: _strided8(xs, half, r0 + HALO + i) for i in range(p_lo, p_hi + 1)}
                sums = _window_sums(x, p_lo, p_hi, (w_lo, w_hi))
                win = wins[half]
                left = win // 2
                right = win - 1 - left
                pieces = []
                for i in range(SUBLANES):
                    tot = jnp.where(low_group, sums[w_lo][i + w_lo // 2 - 1], sums[w_hi][i + w_hi // 2 - 1])
                    t = sub8 + (r0 + i)
                    cnt = jnp.minimum(t + right, t_len - 1) - jnp.maximum(t - left, 0) + 1
                    pieces.append(tot / cnt.astype(F32) - x[i])
                halves.append(pieces)
            pm_rows += [jnp.concatenate([halves[0][i], halves[1][i]], axis=-1)
                        for i in range(SUBLANES)]
        pm = jnp.concatenate(pm_rows, axis=0)
        yd = jnp.dot(pm.astype(BF16), pw_ref[...], preferred_element_type=F32) * ps_ref[...]
        for blk in range(n_blk):
            for i in range(SUBLANES):
                k0 = blk * RG_BLOCK + i * SUBLANES
                for half in range(2):
                    ds[half, pl.ds(s + blk * RG_BLOCK + i, SUBLANES, stride=SUBLANES), :] = (
                        yd[k0:k0 + SUBLANES, half * LANES:(half + 1) * LANES])
        conv = jnp.concatenate([cs[0, pl.ds(s, CP_ROWS), :], cs[1, pl.ds(s, CP_ROWS), :]], axis=-1)
        yb_ref[pl.ds(s, CP_ROWS), :] = (gb_ref[pl.ds(s, CP_ROWS), :].astype(F32) * conv).astype(BF16)
        yd_ref[pl.ds(s, CP_ROWS), :] = jnp.concatenate(
            [ds[0, pl.ds(s, CP_ROWS), :], ds[1, pl.ds(s, CP_ROWS), :]], axis=-1).astype(BF16)
        return carry

    lax.fori_loop(0, n_chunks, body, 0)


def _convpool(z, layer, cw, pw, ps):
    b, t, _ = z.shape
    col = lambda k: pl.BlockSpec((None, t, D_GROUP), lambda i: (i, 0, k))
    out = pl.BlockSpec((None, t, D_GROUP), lambda i: (i, 0, 0))
    slab = lambda rows: pltpu.VMEM((2, rows, LANES), F32)
    return pl.pallas_call(
        functools.partial(_convpool_kernel, t_len=t),
        grid=(b,),
        in_specs=[col(3), col(4), col(5), col(8),
                  _resident((None, 3, D_GROUP), lambda i: (layer, 0, 0)),
                  _resident((D_GROUP, D_GROUP), lambda i: (0, 0)),
                  _resident((None, 1, D_GROUP), lambda i: (layer, 0, 0))],
        out_specs=[out, out],
        out_shape=[jax.ShapeDtypeStruct((b, t, D_GROUP), BF16)] * 2,
        scratch_shapes=[slab(t + 2 * HALO), slab(t + 2 * HALO), slab(t), slab(t)],
        compiler_params=_params("arbitrary"),
        name="convpool",
    )(z, z, z, z, cw, pw, ps)


FF_CHUNK = 1024


def _mlp_kernel(x_ref, ya_ref, yb_ref, yc_ref, yd_ref, mod_ref, nw_ref, wo_ref, w1_ref, w2_ref,
                fw_ref, o_ref, *, final):
    x = x_ref[...]
    y = jnp.zeros(x.shape, F32)
    for i, r in enumerate((ya_ref, yb_ref, yc_ref, yd_ref)):
        y = y + jnp.dot(r[...], wo_ref[i * D_GROUP:(i + 1) * D_GROUP, :],
                        preferred_element_type=F32)
    gate1 = mod_ref[:, 2 * D_MODEL:3 * D_MODEL]
    shift2 = mod_ref[:, 3 * D_MODEL:4 * D_MODEL]
    scale2 = mod_ref[:, 4 * D_MODEL:5 * D_MODEL]
    gate2 = mod_ref[:, 5 * D_MODEL:6 * D_MODEL]
    x1 = x + gate1 * y
    hn = (_rms(x1, nw_ref[...]) * (1.0 + scale2) + shift2).astype(BF16)
    acc = jnp.zeros(x.shape, F32)
    for f in range(D_FF // FF_CHUNK):
        u = jnp.dot(hn, w1_ref[:, f * FF_CHUNK:(f + 1) * FF_CHUNK], preferred_element_type=F32)
        u = jnp.square(jnp.maximum(u, 0.0)).astype(BF16)
        acc = acc + jnp.dot(u, w2_ref[f * FF_CHUNK:(f + 1) * FF_CHUNK, :], preferred_element_type=F32)
    x2 = x1 + gate2 * acc
    if final:
        x2 = _rms(x2, fw_ref[...])
    o_ref[...] = x2


def _mlp(x, ys, mod, layer, mod_row, norm_w, wo, w1, w2, fw, final, tt, shared_mod=False):
    b, t, _ = x.shape
    if shared_mod:
        x, ys = x.reshape(1, b * t, D_MODEL), [y.reshape(1, b * t, D_GROUP) for y in ys]
    nb, nt, _ = x.shape
    ytile = pl.BlockSpec((None, tt, D_GROUP), lambda i, j: (i, j, 0))
    out = pl.pallas_call(
        functools.partial(_mlp_kernel, final=final),
        grid=(nb, nt // tt),
        in_specs=[
            pl.BlockSpec((None, tt, D_MODEL), lambda i, j: (i, j, 0)),
            ytile, ytile, ytile, ytile,
            pl.BlockSpec((None, None, 1, N_MOD * D_MODEL), lambda i, j: (layer, mod_row(i), 0, 0)),
            _resident((None, None, 1, D_MODEL), lambda i, j: (layer, 1, 0, 0)),
            _resident((None, D_MODEL, D_MODEL), lambda i, j: (layer, 0, 0)),
            _resident((None, D_MODEL, D_FF), lambda i, j: (layer, 0, 0)),
            _resident((None, D_FF, D_MODEL), lambda i, j: (layer, 0, 0)),
            _resident((1, D_MODEL), lambda i, j: (0, 0)),
        ],
        out_specs=pl.BlockSpec((None, tt, D_MODEL), lambda i, j: (i, j, 0)),
        out_shape=jax.ShapeDtypeStruct((nb, nt, D_MODEL), F32),
        compiler_params=_params("arbitrary", "arbitrary"),
        name="outproj_mlp",
    )(x, *ys, mod, norm_w, wo, w1, w2, fw)
    return out.reshape(b, t, D_MODEL)


def _block_diag(w):
    *lead, n, k, _ = w.shape
    eye = jnp.eye(n, dtype=w.dtype)
    return (eye[:, None, :, None] * w[..., :, :, None, :]).reshape(*lead, n * k, n * k)


def _rope_tables(t_len):
    half = DK // 4
    freqs = ROPE_BASE ** (-jnp.arange(half, dtype=F32) / half)
    lane = np.arange(LANES)
    m = lane % DK
    use_col = m >= DK // 2
    fidx = m % half
    t = jnp.arange(t_len)
    pos = jnp.where(use_col[None, :], (t % GRID_W)[:, None], (t // GRID_W)[:, None]).astype(F32)
    ang = pos * freqs[fidx][None, :]
    sign = np.where((lane % (2 * half)) < half, -1.0, 1.0).astype(np.float32)
    return jnp.cos(ang), jnp.sin(ang) * sign[None, :]


def kernel(x_prompt, x_sample, cache_k, cache_v, state_rglru, c, c_ctx, w_ada, b_ada, norm_w, w_in, diff_lambda, subln_w, conv_b_w, conv_c_w, conv_c_b, rg_w, rg_b, rg_lambda, pool_w, pool_scale, w_out, w_mlp1, w_mlp2, final_norm_w):
    n_lat = c.shape[0]
    ctx_row = n_lat
    cc = jnp.zeros((MOD_ROWS, D_MODEL), F32).at[:n_lat].set(c).at[ctx_row].set(c_ctx)
    mod = _ada(cc, w_ada, b_ada).reshape(DEPTH, MOD_ROWS, 1, N_MOD * D_MODEL)

    w_in_b = w_in.astype(BF16)
    w_out_b = w_out.astype(BF16)
    w1_b = w_mlp1.astype(BF16)
    w2_b = w_mlp2.astype(BF16)
    norm_w4 = norm_w.reshape(DEPTH, 2, 1, D_MODEL)
    rope_tabs = _rope_tables(x_sample.shape[1])
    gmat = _block_diag(jnp.full((N_ATT_HEADS, V_DIM, V_DIM), 1.0 / V_DIM, F32))
    subw = jnp.tile(subln_w, (1, N_ATT_HEADS)).reshape(DEPTH, 1, D_GROUP)
    wg = _block_diag(rg_w).transpose(0, 3, 1, 2, 4).reshape(DEPTH, D_GROUP, 4 * D_GROUP).astype(BF16)
    bg = rg_b.reshape(DEPTH, 1, 4 * D_GROUP)
    pw = _block_diag(pool_w).astype(BF16)
    ps = pool_scale.reshape(DEPTH, 1, D_GROUP)
    cb = conv_c_b.reshape(DEPTH, 1, D_GROUP)
    fw = final_norm_w.reshape(1, D_MODEL)
    zero_state = jnp.zeros((x_prompt.shape[0], 2, D_GROUP), F32)

    xp, xs = x_prompt, x_sample
    kv_out, hs_out = [], []
    for l in range(DEPTH):
        lam_init = 0.8 - 0.6 * math.exp(-0.3 * l)
        final = l == DEPTH - 1

        def mixers(z, ctx, h0):
            ya = _attn(z, ctx, l, diff_lambda, subw, gmat, lam_init)
            yc, last = _rg(z, h0, l, conv_c_w, cb, wg[l], bg, rg_lambda)
            yb, yd = _convpool(z, l, conv_b_w, pw[l], ps)
            return (ya, yb, yc, yd), last

        ctx_rows = lambda i: ctx_row
        kv_prev = kv_out if final else ()
        zp, k_new, v_new = _inproj(xp, mod, l, ctx_rows, norm_w4, w_in_b, None, TOKEN_TILE, kv_prev)
        ys, last_p = mixers(zp, None, zero_state)
        xp = _mlp(xp, ys, mod, l, ctx_rows, norm_w4, w_out_b, w1_b, w2_b, fw, final, TOKEN_TILE,
                  shared_mod=True)
        kv_out += [k_new, v_new]
        hs_out.append(last_p)

        lat_rows = lambda i: i
        (zs,) = _inproj(xs, mod, l, lat_rows, norm_w4, w_in_b, rope_tabs, TOKEN_TILE)
        ys, _ = mixers(zs, (cache_k, cache_v), state_rglru)
        xs = _mlp(xs, ys, mod, l, lat_rows, norm_w4, w_out_b, w1_b, w2_b, fw, final, TOKEN_TILE)

    return (xp, xs, kv_out[-2], kv_out[-1], jnp.stack(hs_out, axis=1))
```

```python
import functools
import math

import jax
import jax.numpy as jnp
import numpy as np
from jax import lax
from jax.experimental import pallas as pl
from jax.experimental.pallas import tpu as pltpu

D_MODEL = 1024
DEPTH = 2
GRID_W = 64
D_GROUP = 256
N_ATT_HEADS = 4
V_DIM = 64
DK = 32
ROPE_BASE = 10000.0
RG_BLOCKS = 4
RG_BW = 64
RG_C = 8.0
POOL_WINDOWS = (2, 4, 8, 16)
POOL_GW = 64
D_FF = 4 * D_MODEL
N_MOD = 6
D_IN = 9 * D_GROUP
EPS = 1e-6

LANES = 128
SUBLANES = 8
HALO = 8
VMEM_LIMIT = 56 * 1024 * 1024
MOD_ROWS = 16
TOKEN_TILE = 512

BF16 = jnp.bfloat16
F32 = jnp.float32


def _params(*sem):
    return pltpu.CompilerParams(dimension_semantics=sem, vmem_limit_bytes=VMEM_LIMIT)


def _resident(shape, index_map):
    return pl.BlockSpec(shape, index_map, pipeline_mode=pl.Buffered(1))


def _rms(x, w):
    ms = jnp.mean(x * x, axis=-1, keepdims=True)
    return x * lax.rsqrt(ms + EPS) * w


def _ada_kernel(c_ref, w_ref, b_ref, o_ref):
    c = c_ref[...]
    s = (c * jax.nn.sigmoid(c)).astype(BF16)
    o_ref[...] = jnp.dot(s, w_ref[...].astype(BF16), preferred_element_type=F32) + b_ref[...]


def _ada(cc, w_ada, b_ada):
    tn = 1536
    n_out = N_MOD * D_MODEL
    return pl.pallas_call(
        _ada_kernel,
        grid=(DEPTH, n_out // tn),
        in_specs=[
            pl.BlockSpec((MOD_ROWS, D_MODEL), lambda l, n: (0, 0)),
            pl.BlockSpec((None, D_MODEL, tn), lambda l, n: (l, 0, n)),
            pl.BlockSpec((None, 1, tn), lambda l, n: (l, 0, n)),
        ],
        out_specs=pl.BlockSpec((None, MOD_ROWS, tn), lambda l, n: (l, 0, n)),
        out_shape=jax.ShapeDtypeStruct((DEPTH, MOD_ROWS, n_out), F32),
        compiler_params=_params("arbitrary", "arbitrary"),
        name="ada_mod",
    )(cc, w_ada, b_ada.reshape(DEPTH, 1, n_out))


def _inproj_kernel(*refs, rope, n_prev, seq_len):
    if rope:
        x_ref, mod_ref, nw_ref, w_ref, cos_ref, sin_ref, z_ref = refs
    else:
        x_ref, mod_ref, nw_ref, w_ref = refs[:4]
        prev_refs = refs[4:4 + 2 * n_prev]
        z_ref, ko_ref, vo_ref = refs[4 + 2 * n_prev:]
    x = x_ref[...]
    shift = mod_ref[:, 0:D_MODEL]
    scale = mod_ref[:, D_MODEL:2 * D_MODEL]
    h = _rms(x, nw_ref[...]) * (1.0 + scale) + shift
    z = jnp.dot(h.astype(BF16), w_ref[...], preferred_element_type=F32)
    if not rope:
        z_ref[...] = z.astype(BF16)
        for sq in range(x.shape[0] // seq_len):
            r0 = sq * seq_len
            if n_prev:
                for lp in range(n_prev):
                    ko_ref[sq, lp] = prev_refs[2 * lp][sq]
                    vo_ref[sq, lp] = prev_refs[2 * lp + 1][sq]
                ko_l, vo_l = ko_ref.at[sq, n_prev], vo_ref.at[sq, n_prev]
            else:
                ko_l, vo_l = ko_ref.at[sq], vo_ref.at[sq]
            for hd in range(N_ATT_HEADS):
                c0 = D_GROUP + hd * V_DIM
                ko_l[hd] = z[r0:r0 + seq_len, c0:c0 + V_DIM]
                vo_l[hd] = z[r0:r0 + seq_len, D_GROUP + c0:D_GROUP + c0 + V_DIM]
        return
    cos = cos_ref[...]
    sin = sin_ref[...]
    first_half = (lax.broadcasted_iota(jnp.int32, (1, LANES), 1) % 16) < 8
    for j in range(2 * D_GROUP // LANES):
        zc = z[:, j * LANES:(j + 1) * LANES]
        partner = jnp.where(first_half,
                            pltpu.roll(zc, LANES - 8, 1),
                            pltpu.roll(zc, 8, 1))
        z_ref[:, j * LANES:(j + 1) * LANES] = (zc * cos + partner * sin).astype(BF16)
    z_ref[:, 2 * D_GROUP:] = z[:, 2 * D_GROUP:].astype(BF16)


def _inproj(x, mod, layer, mod_row, norm_w, w_in, rope_tabs, tt, kv_prev=()):
    b, t, _ = x.shape
    rope = rope_tabs is not None
    n_prev = len(kv_prev) // 2
    if rope:
        grid = (b, t // tt)
        tok = lambda i, j: (i, j, 0)
        x3 = x
    else:
        grid = (1, b * t // tt)
        tok = lambda i, j: (0, j, 0)
        x3 = x.reshape(1, b * t, D_MODEL)
    in_specs = [
        pl.BlockSpec((None, tt, D_MODEL), tok),
        pl.BlockSpec((None, None, 1, N_MOD * D_MODEL), lambda i, j: (layer, mod_row(i), 0, 0)),
        _resident((None, None, 1, D_MODEL), lambda i, j: (layer, 0, 0, 0)),
        _resident((None, D_MODEL, D_IN), lambda i, j: (layer, 0, 0)),
    ]
    args = [x3, mod, norm_w, w_in]
    out_specs = [pl.BlockSpec((None, tt, D_IN), tok)]
    out_shape = [jax.ShapeDtypeStruct(x3.shape[:2] + (D_IN,), BF16)]
    if rope:
        in_specs += [pl.BlockSpec((tt, LANES), lambda i, j: (j, 0))] * 2
        args += list(rope_tabs)
    else:
        n_seq = tt // t
        per_layer = pl.BlockSpec((n_seq, N_ATT_HEADS, t, V_DIM), lambda i, j: (j, 0, 0, 0))
        in_specs += [per_layer] * (2 * n_prev)
        args += list(kv_prev)
        if n_prev:
            out_specs += [pl.BlockSpec((n_seq, n_prev + 1, N_ATT_HEADS, t, V_DIM),
                                       lambda i, j: (j, 0, 0, 0, 0))] * 2
            out_shape += [jax.ShapeDtypeStruct((b, n_prev + 1, N_ATT_HEADS, t, V_DIM), F32)] * 2
        else:
            out_specs += [per_layer] * 2
            out_shape += [jax.ShapeDtypeStruct((b, N_ATT_HEADS, t, V_DIM), F32)] * 2
    outs = pl.pallas_call(
        functools.partial(_inproj_kernel, rope=rope, n_prev=n_prev, seq_len=t),
        grid=grid,
        in_specs=in_specs,
        out_specs=out_specs,
        out_shape=out_shape,
        compiler_params=_params("arbitrary", "arbitrary"),
        name="inproj",
    )(*args)
    return [outs[0].reshape(b, t, D_IN)] + list(outs[1:])


ATT_ROWS = 128
LOG2E = math.log2(math.e)


def _attn_kernel(*refs, has_ctx, lam_init, t_len, n_steps):
    if has_ctx:
        (q_ref, qn_ref, k_ref, v_ref, ck_ref, cv_ref, dl_ref, sw_ref, g_ref, o_ref,
         kt, vs, s0, s1, pb) = refs
    else:
        q_ref, qn_ref, k_ref, v_ref, dl_ref, sw_ref, g_ref, o_ref, kt, vs, s0, s1, pb = refs
    rows = ATT_ROWS
    past = ck_ref.shape[1] if has_ctx else 0
    s_len = past + t_len
    lane = lax.broadcasted_iota(jnp.int32, (1, D_GROUP), 1)

    def scores(q_rows, dst):
        q = q_rows.astype(F32) * (DK ** -0.5 * LOG2E)
        stack = jnp.concatenate(
            [jnp.where((lane // DK) == hm, q, 0.0).astype(BF16) for hm in range(2 * N_ATT_HEADS)],
            axis=0)
        dst[...] = jnp.dot(stack, kt[...], preferred_element_type=F32)

    @pl.when(pl.program_id(1) == 0)
    def _():
        if has_ctx:
            heads = range(N_ATT_HEADS)
            ck = jnp.concatenate([ck_ref[h] for h in heads], axis=-1)
            cv = jnp.concatenate([cv_ref[h] for h in heads], axis=-1)
            kt[:, 0:past] = ck.T.astype(BF16)
            vs[0:past, :] = cv.astype(BF16)
        kt[:, past:s_len] = k_ref[...].astype(F32).T.astype(BF16)
        vs[past:s_len, :] = v_ref[...]
        scores(q_ref[0:rows, :], s0)

    dl = dl_ref[...]
    lam = (jnp.exp(jnp.sum(dl[0:1] * dl[1:2], axis=-1, keepdims=True))
           - jnp.exp(jnp.sum(dl[2:3] * dl[3:4], axis=-1, keepdims=True)) + lam_init)

    def finish(src, out_lo):
        inv_l1 = jnp.zeros((rows, D_GROUP), F32)
        for h in range(N_ATT_HEADS):
            es, ls = [], []
            for m in range(2):
                lo = (2 * h + m) * rows
                s = src[lo:lo + rows, :]
                e = jnp.exp2(s - jnp.max(s, axis=-1, keepdims=True))
                es.append(e)
                ls.append(jnp.sum(e, axis=-1, keepdims=True))
            beta = lam * ls[0] / ls[1]
            pb[h * rows:(h + 1) * rows, :] = (es[0] - beta * es[1]).astype(BF16)
            inv_l1 = jnp.where((lane // V_DIM) == h, 1.0 / ls[0], inv_l1)
        full = jnp.dot(pb[...], vs[...], preferred_element_type=F32)
        acc = full[0:rows]
        for h in range(1, N_ATT_HEADS):
            acc = jnp.where((lane // V_DIM) == h, full[h * rows:(h + 1) * rows], acc)
        acc = acc * inv_l1
        ms = jnp.dot(acc * acc, g_ref[...], preferred_element_type=F32,
                     precision=lax.Precision.HIGHEST)
        o_ref[out_lo:out_lo + rows, :] = (
            acc * lax.rsqrt(ms + EPS) * sw_ref[...] * (1.0 - lam_init)).astype(BF16)

    scores(q_ref[rows:2 * rows, :], s1)
    finish(s0, 0)
    if n_steps > 1:
        scores(qn_ref[0:rows, :], s0)
    finish(s1, rows)


def _attn(z, ctx, layer, dl, subw, gmat, lam_init):
    b, t, _ = z.shape
    has_ctx = ctx is not None
    past = ctx[0].shape[3] if has_ctx else 0
    s_len = past + t
    tq = 2 * ATT_ROWS
    n_steps = t // tq
    in_specs = [
        pl.BlockSpec((None, tq, D_GROUP), lambda i, j: (i, j, 0)),
        pl.BlockSpec((None, tq, D_GROUP), lambda i, j: (i, jnp.minimum(j + 1, n_steps - 1), 0)),
        pl.BlockSpec((None, t, D_GROUP), lambda i, j: (i, 0, 1)),
        pl.BlockSpec((None, t, D_GROUP), lambda i, j: (i, 0, 2)),
    ]
    args = [z, z, z, z]
    if has_ctx:
        in_specs += [pl.BlockSpec((None, None, N_ATT_HEADS, past, V_DIM),
                                  lambda i, j: (i, layer, 0, 0, 0))] * 2
        args += list(ctx)
    in_specs += [
        _resident((None, 4, DK), lambda i, j: (layer, 0, 0)),
        _resident((None, 1, D_GROUP), lambda i, j: (layer, 0, 0)),
        _resident((D_GROUP, D_GROUP), lambda i, j: (0, 0)),
    ]
    args += [dl, subw, gmat]
    n_hm = 2 * N_ATT_HEADS
    return pl.pallas_call(
        functools.partial(_attn_kernel, has_ctx=has_ctx, lam_init=lam_init, t_len=t,
                          n_steps=n_steps),
        grid=(b, n_steps),
        in_specs=in_specs,
        out_specs=pl.BlockSpec((None, tq, D_GROUP), lambda i, j: (i, j, 0)),
        out_shape=jax.ShapeDtypeStruct((b, t, D_GROUP), BF16),
        scratch_shapes=[pltpu.VMEM((D_GROUP, s_len), BF16),
                        pltpu.VMEM((s_len, D_GROUP), BF16),
                        pltpu.VMEM((n_hm * ATT_ROWS, s_len), F32),
                        pltpu.VMEM((n_hm * ATT_ROWS, s_len), F32),
                        pltpu.VMEM((N_ATT_HEADS * ATT_ROWS, s_len), BF16)],
        compiler_params=_params("arbitrary", "arbitrary"),
        name="diff_attn",
    )(*args)


RG_ROWS = 256
RG_BLOCK = 64


def _strided8(ref, half, start):
    return ref[half, pl.ds(start, SUBLANES, stride=SUBLANES), :]


def _rg_gates(xc, wg, bg, c_dir):
    g = jnp.dot(xc.astype(BF16), wg, preferred_element_type=F32) + bg
    r = jax.nn.sigmoid(g[:, :D_GROUP])
    i = jax.nn.sigmoid(g[:, D_GROUP:])
    a = jnp.exp2(r * c_dir)
    om = 1.0 - a * a
    b = jnp.where(om > 0.0, om * lax.rsqrt(om), 0.0) * i * xc
    return a, b


def _scan_block(a, b, hp, reverse):
    n = SUBLANES
    order = list(range(n - 1, -1, -1)) if reverse else list(range(n))
    hs, cum = [None] * n, [None] * n
    prev = None
    for i in order:
        if prev is None:
            hs[i], cum[i] = b[i], a[i]
        else:
            hs[i], cum[i] = a[i] * hs[prev] + b[i], a[i] * cum[prev]
        prev = i
    p, f = cum[prev], hs[prev]
    row = lax.broadcasted_iota(jnp.int32, (n, LANES), 0)
    for d in (1, 2, 4):
        m = (row < n - d) if reverse else (row >= d)
        sh = n - d if reverse else d
        f = jnp.where(m, p * pltpu.roll(f, sh, 0) + f, f)
        p = jnp.where(m, p * pltpu.roll(p, sh, 0), p)
    first, last = (n - 1, 0) if reverse else (0, n - 1)
    sh = n - 1 if reverse else 1
    cin = jnp.where(row == first, hp, pltpu.roll(f, sh, 0) + pltpu.roll(p, sh, 0) * hp)
    out = [hs[i] + cum[i] * cin for i in range(n)]
    bc = lambda v: jnp.broadcast_to(v[last:last + 1, :], (n, LANES))
    return out, bc(f) + bc(p) * hp


def _rg_kernel(xr_ref, gr_ref, h0_ref, cw_ref, cb_ref, wg_ref, bg_ref, lam_ref,
               y_ref, last_ref, xs, gs, ys, xc_s, hf_s, *, t_len):
    n_chunks = t_len // RG_ROWS
    n_blk = RG_ROWS // RG_BLOCK
    zeros_halo = jnp.zeros((HALO, LANES), F32)
    for half in range(2):
        lo = half * LANES
        xs[half, 0:HALO, :] = zeros_halo
        xs[half, HALO + t_len:, :] = zeros_halo
        xs[half, HALO:HALO + t_len, :] = xr_ref[:, lo:lo + LANES].astype(F32)
        gs[half, :, :] = gr_ref[:, lo:lo + LANES].astype(F32)

    neg = -lam_ref[...]
    softplus = jnp.maximum(neg, 0.0) + jnp.log1p(jnp.exp(-jnp.abs(neg)))
    c_all = (-RG_C * LOG2E) * softplus
    cw = cw_ref[...]
    cb = cb_ref[...]

    def piece(v, blk, i, half):
        r0 = blk * RG_BLOCK + i * SUBLANES
        return v[r0:r0 + SUBLANES, half * LANES:(half + 1) * LANES]

    def scan_chunk(a, b, carry, reverse):
        carry = list(carry)
        out = {}
        for blk in (reversed(range(n_blk)) if reverse else range(n_blk)):
            for half in range(2):
                hs, carry[half] = _scan_block([piece(a, blk, i, half) for i in range(SUBLANES)],
                                              [piece(b, blk, i, half) for i in range(SUBLANES)],
                                              carry[half], reverse)
                for i in range(SUBLANES):
                    out[blk, i, half] = hs[i]
        rows = [jnp.concatenate([out[blk, i, 0], out[blk, i, 1]], axis=-1)
                for blk in range(n_blk) for i in range(SUBLANES)]
        return jnp.concatenate(rows, axis=0), tuple(carry)

    def start_state(d):
        return tuple(jnp.broadcast_to(h0_ref[d:d + 1, half * LANES:(half + 1) * LANES],
                                      (SUBLANES, LANES)) for half in range(2))

    def fwd_body(c, carry):
        s = pl.multiple_of(c * RG_ROWS, RG_ROWS)
        rows = []
        for blk in range(n_blk):
            base = s + blk * RG_BLOCK + HALO
            halves = []
            for half in range(2):
                lo = half * LANES
                x = {i: _strided8(xs, half, base + i) for i in range(-2, SUBLANES + 1)}
                halves.append([cw[0:1, lo:lo + LANES] * x[i - 2] + cw[1:2, lo:lo + LANES] * x[i - 1]
                               + cw[2:3, lo:lo + LANES] * x[i] + cw[3:4, lo:lo + LANES] * x[i + 1]
                               + cb[:, lo:lo + LANES] for i in range(SUBLANES)])
            rows += [jnp.concatenate([halves[0][i], halves[1][i]], axis=-1) for i in range(SUBLANES)]
        xc = jnp.concatenate(rows, axis=0)
        xc_s[pl.ds(s, RG_ROWS), :] = xc
        a, b = _rg_gates(xc, wg_ref[:, 0:2 * D_GROUP], bg_ref[:, 0:2 * D_GROUP], c_all[0:1])
        hf, carry = scan_chunk(a, b, carry, False)
        hf_s[pl.ds(s, RG_ROWS), :] = hf
        return carry

    h_f = lax.fori_loop(0, n_chunks, fwd_body, start_state(0))
    last_ref[0:1, :] = jnp.concatenate([h_f[0][0:1], h_f[1][0:1]], axis=-1)

    def bwd_body(ci, carry):
        c = n_chunks - 1 - ci
        s = pl.multiple_of(c * RG_ROWS, RG_ROWS)
        xc = xc_s[pl.ds(s, RG_ROWS), :]
        a, b = _rg_gates(xc, wg_ref[:, 2 * D_GROUP:], bg_ref[:, 2 * D_GROUP:], c_all[1:2])
        hb, carry = scan_chunk(a, b, carry, True)
        gr = jnp.concatenate(
            [jnp.concatenate([_strided8(gs, half, s + blk * RG_BLOCK + i) for half in range(2)], axis=-1)
             for blk in range(n_blk) for i in range(SUBLANES)], axis=0)
        k_gelu = math.sqrt(2.0 / math.pi)
        hg = 0.5 * gr
        gelu = hg + hg * jnp.tanh(gr * (k_gelu + (k_gelu * 0.044715) * (gr * gr)))
        y = (hf_s[pl.ds(s, RG_ROWS), :] + hb) * gelu
        for blk in range(n_blk):
            for i in range(SUBLANES):
                for half in range(2):
                    ys[half, pl.ds(s + blk * RG_BLOCK + i, SUBLANES, stride=SUBLANES), :] = (
                        piece(y, blk, i, half))
        y_ref[pl.ds(s, RG_ROWS), :] = jnp.concatenate(
            [ys[0, pl.ds(s, RG_ROWS), :], ys[1, pl.ds(s, RG_ROWS), :]], axis=-1).astype(BF16)
        return carry

    h_b = lax.fori_loop(0, n_chunks, bwd_body, start_state(1))
    last_ref[1:2, :] = jnp.concatenate([h_b[0][0:1], h_b[1][0:1]], axis=-1)


CP_ROWS = 256


def _window_sums(x, lo, hi, widths):
    out, prev, w = {}, {i: x[i] for i in range(lo, hi + 1)}, 1
    while w < max(widths):
        prev = {i: prev[i] + prev[i - w] for i in prev if i - w in prev}
        w *= 2
        out[w] = prev
    return out


def _convpool_kernel(gb_ref, gc_ref, xb_ref, xp_ref, cw_ref, pw_ref, ps_ref,
                     yb_ref, yd_ref, us, xs, cs, ds, *, t_len):
    n_chunks = t_len // CP_ROWS
    n_blk = CP_ROWS // RG_BLOCK
    zeros_halo = jnp.zeros((HALO, LANES), F32)
    for half in range(2):
        for pad in (us, xs):
            pad[half, 0:HALO, :] = zeros_halo
            pad[half, HALO + t_len:, :] = zeros_halo

    def fill(c, carry):
        s = pl.multiple_of(c * CP_ROWS, CP_ROWS)
        u = gc_ref[pl.ds(s, CP_ROWS), :].astype(F32) * xb_ref[pl.ds(s, CP_ROWS), :].astype(F32)
        x = xp_ref[pl.ds(s, CP_ROWS), :].astype(F32)
        for half in range(2):
            us[half, pl.ds(HALO + s, CP_ROWS), :] = u[:, half * LANES:(half + 1) * LANES]
            xs[half, pl.ds(HALO + s, CP_ROWS), :] = x[:, half * LANES:(half + 1) * LANES]
        return carry

    lax.fori_loop(0, n_chunks, fill, 0)

    cw = cw_ref[...]
    low_group = lax.broadcasted_iota(jnp.int32, (1, LANES), 1) < POOL_GW
    sub8 = lax.broadcasted_iota(jnp.int32, (SUBLANES, LANES), 0) * SUBLANES
    wins = [jnp.where(low_group, POOL_WINDOWS[2 * half], POOL_WINDOWS[2 * half + 1])
            for half in range(2)]

    def body(c, carry):
        s = pl.multiple_of(c * CP_ROWS, CP_ROWS)
        pm_rows = []
        for blk in range(n_blk):
            r0 = s + blk * RG_BLOCK
            halves = []
            for half in range(2):
                lo = half * LANES
                u = {i: _strided8(us, half, r0 + HALO + i) for i in range(-1, SUBLANES + 1)}
                for i in range(SUBLANES):
                    cs[half, pl.ds(r0 + i, SUBLANES, stride=SUBLANES), :] = (
                        cw[0:1, lo:lo + LANES] * u[i - 1] + cw[1:2, lo:lo + LANES] * u[i]
                        + cw[2:3, lo:lo + LANES] * u[i + 1])
                w_lo, w_hi = POOL_WINDOWS[2 * half], POOL_WINDOWS[2 * half + 1]
                p_lo, p_hi = -(w_hi // 2), SUBLANES - 1 + w_hi // 2 - 1
                x = {i: _strided8(xs, half, r0 + HALO + i) for i in range(p_lo, p_hi + 1)}
                sums = _window_sums(x, p_lo, p_hi, (w_lo, w_hi))
                win = wins[half]
                left = win // 2
                right = win - 1 - left
                pieces = []
                for i in range(SUBLANES):
                    tot = jnp.where(low_group, sums[w_lo][i + w_lo // 2 - 1], sums[w_hi][i + w_hi // 2 - 1])
                    t = sub8 + (r0 + i)
                    cnt = jnp.minimum(t + right, t_len - 1) - jnp.maximum(t - left, 0) + 1
                    pieces.append(tot / cnt.astype(F32) - x[i])
                halves.append(pieces)
            pm_rows += [jnp.concatenate([halves[0][i], halves[1][i]], axis=-1)
                        for i in range(SUBLANES)]
        pm = jnp.concatenate(pm_rows, axis=0)
        yd = jnp.dot(pm.astype(BF16), pw_ref[...], preferred_element_type=F32) * ps_ref[...]
        for blk in range(n_blk):
            for i in range(SUBLANES):
                k0 = blk * RG_BLOCK + i * SUBLANES
                for half in range(2):
                    ds[half, pl.ds(s + blk * RG_BLOCK + i, SUBLANES, stride=SUBLANES), :] = (
                        yd[k0:k0 + SUBLANES, half * LANES:(half + 1) * LANES])
        conv = jnp.concatenate([cs[0, pl.ds(s, CP_ROWS), :], cs[1, pl.ds(s, CP_ROWS), :]], axis=-1)
        yb_ref[pl.ds(s, CP_ROWS), :] = (gb_ref[pl.ds(s, CP_ROWS), :].astype(F32) * conv).astype(BF16)
        yd_ref[pl.ds(s, CP_ROWS), :] = jnp.concatenate(
            [ds[0, pl.ds(s, CP_ROWS), :], ds[1, pl.ds(s, CP_ROWS), :]], axis=-1).astype(BF16)
        return carry

    lax.fori_loop(0, n_chunks, body, 0)


def _seqmix_kernel(xr_ref, gr_ref, gb_ref, gc_ref, xb_ref, xp_ref, h0_ref,
                   ccw_ref, ccb_ref, wg_ref, bg_ref, lam_ref, bcw_ref, pw_ref, ps_ref,
                   yb_ref, yc_ref, yd_ref, last_ref,
                   xs, gs, ys, xc_s, hf_s, us, ps_s, ds, *, t_len):
    _rg_kernel(xr_ref, gr_ref, h0_ref, ccw_ref, ccb_ref, wg_ref, bg_ref, lam_ref,
               yc_ref, last_ref, xs, gs, ys, xc_s, hf_s, t_len=t_len)
    _convpool_kernel(gb_ref, gc_ref, xb_ref, xp_ref, bcw_ref, pw_ref, ps_ref,
                     yb_ref, yd_ref, us, ps_s, ys, ds, t_len=t_len)


def _seqmix(z, h0, layer, conv_c_w, conv_c_b, wg, bg, lam, conv_b_w, pw, ps):
    b, t, _ = z.shape
    col = lambda k: pl.BlockSpec((None, t, D_GROUP), lambda i: (i, 0, k))
    out = pl.BlockSpec((None, t, D_GROUP), lambda i: (i, 0, 0))
    slab = lambda rows: pltpu.VMEM((2, rows, LANES), F32)
    if h0.ndim == 4:
        h0_spec = pl.BlockSpec((None, None, 2, D_GROUP), lambda i: (i, layer, 0, 0))
    else:
        h0_spec = pl.BlockSpec((None, 2, D_GROUP), lambda i: (i, 0, 0))
    yb, yc, yd, last = pl.pallas_call(
        functools.partial(_seqmix_kernel, t_len=t),
        grid=(b,),
        in_specs=[col(6), col(7), col(3), col(4), col(5), col(8), h0_spec,
                  _resident((None, 4, D_GROUP), lambda i: (layer, 0, 0)),
                  _resident((None, 1, D_GROUP), lambda i: (layer, 0, 0)),
                  _resident((None, D_GROUP, 4 * D_GROUP), lambda i: (layer, 0, 0)),
                  _resident((None, 1, 4 * D_GROUP), lambda i: (layer, 0, 0)),
                  _resident((None, 2, D_GROUP), lambda i: (layer, 0, 0)),
                  _resident((None, 3, D_GROUP), lambda i: (layer, 0, 0)),
                  _resident((None, D_GROUP, D_GROUP), lambda i: (layer, 0, 0)),
                  _resident((None, 1, D_GROUP), lambda i: (layer, 0, 0))],
        out_specs=[out, out, out, pl.BlockSpec((None, 2, D_GROUP), lambda i: (i, 0, 0))],
        out_shape=[jax.ShapeDtypeStruct((b, t, D_GROUP), BF16)] * 3
        + [jax.ShapeDtypeStruct((b, 2, D_GROUP), F32)],
        scratch_shapes=[slab(t + 2 * HALO), slab(t), slab(t),
                        pltpu.VMEM((t, D_GROUP), F32), pltpu.VMEM((t, D_GROUP), F32),
                        slab(t + 2 * HALO), slab(t + 2 * HALO), slab(t)],
        compiler_params=_params("arbitrary"),
        name="seqmix",
    )(z, z, z, z, z, z, h0, conv_c_w, conv_c_b, wg, bg, lam, conv_b_w, pw, ps)
    return (yb, yc, yd), last


FF_CHUNK = 1024


def _mlp_kernel(x_ref, ya_ref, yb_ref, yc_ref, yd_ref, mod_ref, nw_ref, wo_ref, w1_ref, w2_ref,
                fw_ref, o_ref, *, final):
    x = x_ref[...]
    y = jnp.zeros(x.shape, F32)
    for i, r in enumerate((ya_ref, yb_ref, yc_ref, yd_ref)):
        y = y + jnp.dot(r[...], wo_ref[i * D_GROUP:(i + 1) * D_GROUP, :],
                        preferred_element_type=F32)
    gate1 = mod_ref[:, 2 * D_MODEL:3 * D_MODEL]
    shift2 = mod_ref[:, 3 * D_MODEL:4 * D_MODEL]
    scale2 = mod_ref[:, 4 * D_MODEL:5 * D_MODEL]
    gate2 = mod_ref[:, 5 * D_MODEL:6 * D_MODEL]
    x1 = x + gate1 * y
    hn = (_rms(x1, nw_ref[...]) * (1.0 + scale2) + shift2).astype(BF16)
    acc = jnp.zeros(x.shape, F32)
    for f in range(D_FF // FF_CHUNK):
        u = jnp.dot(hn, w1_ref[:, f * FF_CHUNK:(f + 1) * FF_CHUNK], preferred_element_type=F32)
        u = jnp.square(jnp.maximum(u, 0.0)).astype(BF16)
        acc = acc + jnp.dot(u, w2_ref[f * FF_CHUNK:(f + 1) * FF_CHUNK, :], preferred_element_type=F32)
    x2 = x1 + gate2 * acc
    if final:
        x2 = _rms(x2, fw_ref[...])
    o_ref[...] = x2


def _mlp(x, ys, mod, layer, mod_row, norm_w, wo, w1, w2, fw, final, tt, shared_mod=False):
    b, t, _ = x.shape
    if shared_mod:
        x, ys = x.reshape(1, b * t, D_MODEL), [y.reshape(1, b * t, D_GROUP) for y in ys]
    nb, nt, _ = x.shape
    ytile = pl.BlockSpec((None, tt, D_GROUP), lambda i, j: (i, j, 0))
    out = pl.pallas_call(
        functools.partial(_mlp_kernel, final=final),
        grid=(nb, nt // tt),
        in_specs=[
            pl.BlockSpec((None, tt, D_MODEL), lambda i, j: (i, j, 0)),
            ytile, ytile, ytile, ytile,
            pl.BlockSpec((None, None, 1, N_MOD * D_MODEL), lambda i, j: (layer, mod_row(i), 0, 0)),
            _resident((None, None, 1, D_MODEL), lambda i, j: (layer, 1, 0, 0)),
            _resident((None, D_MODEL, D_MODEL), lambda i, j: (layer, 0, 0)),
            _resident((None, D_MODEL, D_FF), lambda i, j: (layer, 0, 0)),
            _resident((None, D_FF, D_MODEL), lambda i, j: (layer, 0, 0)),
            _resident((1, D_MODEL), lambda i, j: (0, 0)),
        ],
        out_specs=pl.BlockSpec((None, tt, D_MODEL), lambda i, j: (i, j, 0)),
        out_shape=jax.ShapeDtypeStruct((nb, nt, D_MODEL), F32),
        compiler_params=_params("arbitrary", "arbitrary"),
        name="outproj_mlp",
    )(x, *ys, mod, norm_w, wo, w1, w2, fw)
    return out.reshape(b, t, D_MODEL)


def _block_diag(w):
    *lead, n, k, _ = w.shape
    eye = jnp.eye(n, dtype=w.dtype)
    return (eye[:, None, :, None] * w[..., :, :, None, :]).reshape(*lead, n * k, n * k)


def _rope_tables(t_len):
    half = DK // 4
    freqs = ROPE_BASE ** (-np.arange(half, dtype=np.float64) / half)
    lane = np.arange(LANES)
    m = lane % DK
    use_col = m >= DK // 2
    fidx = m % half
    t = np.arange(t_len)
    pos = np.where(use_col[None, :], (t % GRID_W)[:, None], (t // GRID_W)[:, None])
    ang = pos * freqs[fidx][None, :]
    sign = np.where((lane % (2 * half)) < half, -1.0, 1.0)
    return (jnp.asarray(np.cos(ang), dtype=F32), jnp.asarray(np.sin(ang) * sign[None, :], dtype=F32))


def kernel(x_prompt, x_sample, cache_k, cache_v, state_rglru, c, c_ctx, w_ada, b_ada, norm_w, w_in, diff_lambda, subln_w, conv_b_w, conv_c_w, conv_c_b, rg_w, rg_b, rg_lambda, pool_w, pool_scale, w_out, w_mlp1, w_mlp2, final_norm_w):
    n_lat = c.shape[0]
    ctx_row = n_lat
    cc = jnp.zeros((MOD_ROWS, D_MODEL), F32).at[:n_lat].set(c).at[ctx_row].set(c_ctx)
    mod = _ada(cc, w_ada, b_ada).reshape(DEPTH, MOD_ROWS, 1, N_MOD * D_MODEL)

    w_in_b = w_in.astype(BF16)
    w_out_b = w_out.astype(BF16)
    w1_b = w_mlp1.astype(BF16)
    w2_b = w_mlp2.astype(BF16)
    norm_w4 = norm_w.reshape(DEPTH, 2, 1, D_MODEL)
    rope_tabs = _rope_tables(x_sample.shape[1])
    gmat = _block_diag(jnp.full((N_ATT_HEADS, V_DIM, V_DIM), 1.0 / V_DIM, F32))
    subw = jnp.tile(subln_w, (1, N_ATT_HEADS)).reshape(DEPTH, 1, D_GROUP)
    wg = _block_diag(rg_w).transpose(0, 3, 1, 2, 4).reshape(DEPTH, D_GROUP, 4 * D_GROUP).astype(BF16)
    bg = rg_b.reshape(DEPTH, 1, 4 * D_GROUP)
    pw = _block_diag(pool_w).astype(BF16)
    ps = pool_scale.reshape(DEPTH, 1, D_GROUP)
    cb = conv_c_b.reshape(DEPTH, 1, D_GROUP)
    fw = final_norm_w.reshape(1, D_MODEL)
    zero_state = jnp.zeros((x_prompt.shape[0], 2, D_GROUP), F32)

    xp, xs = x_prompt, x_sample
    kv_out, hs_out = [], []
    for l in range(DEPTH):
        lam_init = 0.8 - 0.6 * math.exp(-0.3 * l)
        final = l == DEPTH - 1

        def mixers(z, ctx, h0):
            ya = _attn(z, ctx, l, diff_lambda, subw, gmat, lam_init)
            (yb, yc, yd), last = _seqmix(z, h0, l, conv_c_w, cb, wg, bg, rg_lambda,
                                         conv_b_w, pw, ps)
            return (ya, yb, yc, yd), last

        ctx_rows = lambda i: ctx_row
        kv_prev = kv_out if final else ()
        zp, k_new, v_new = _inproj(xp, mod, l, ctx_rows, norm_w4, w_in_b, None, TOKEN_TILE, kv_prev)
        ys, last_p = mixers(zp, None, zero_state)
        xp = _mlp(xp, ys, mod, l, ctx_rows, norm_w4, w_out_b, w1_b, w2_b, fw, final, TOKEN_TILE,
                  shared_mod=True)
        kv_out += [k_new, v_new]
        hs_out.append(last_p)

        lat_rows = lambda i: i
        (zs,) = _inproj(xs, mod, l, lat_rows, norm_w4, w_in_b, rope_tabs, TOKEN_TILE)
        ys, _ = mixers(zs, (cache_k, cache_v), state_rglru)
        xs = _mlp(xs, ys, mod, l, lat_rows, norm_w4, w_out_b, w1_b, w2_b, fw, final, TOKEN_TILE)

    return (xp, xs, kv_out[-2], kv_out[-1], jnp.stack(hs_out, axis=1))
```

```python
import functools
import math

import jax
import jax.numpy as jnp
import numpy as np
from jax import lax
from jax.experimental import pallas as pl
from jax.experimental.pallas import tpu as pltpu

D_MODEL = 1024
DEPTH = 2
GRID_W = 64
D_GROUP = 256
N_ATT_HEADS = 4
V_DIM = 64
DK = 32
ROPE_BASE = 10000.0
RG_BLOCKS = 4
RG_BW = 64
RG_C = 8.0
POOL_WINDOWS = (2, 4, 8, 16)
POOL_GW = 64
D_FF = 4 * D_MODEL
N_MOD = 6
D_IN = 9 * D_GROUP
EPS = 1e-6

LANES = 128
SUBLANES = 8
HALO = 8
VMEM_LIMIT = 56 * 1024 * 1024
MOD_ROWS = 16
TOKEN_TILE = 512
MLP_TILE = 1024

BF16 = jnp.bfloat16
F32 = jnp.float32


def _params(*sem):
    return pltpu.CompilerParams(dimension_semantics=sem, vmem_limit_bytes=VMEM_LIMIT)


def _resident(shape, index_map):
    return pl.BlockSpec(shape, index_map, pipeline_mode=pl.Buffered(1))


def _rms(x, w):
    ms = jnp.mean(x * x, axis=-1, keepdims=True)
    return x * lax.rsqrt(ms + EPS) * w


def _ada_kernel(c_ref, w_ref, b_ref, o_ref):
    c = c_ref[...]
    s = (c * jax.nn.sigmoid(c)).astype(BF16)
    o_ref[...] = jnp.dot(s, w_ref[...].astype(BF16), preferred_element_type=F32) + b_ref[...]


def _ada(cc, w_ada, b_ada):
    tn = 1536
    n_out = N_MOD * D_MODEL
    return pl.pallas_call(
        _ada_kernel,
        grid=(DEPTH, n_out // tn),
        in_specs=[
            pl.BlockSpec((MOD_ROWS, D_MODEL), lambda l, n: (0, 0)),
            pl.BlockSpec((None, D_MODEL, tn), lambda l, n: (l, 0, n)),
            pl.BlockSpec((None, 1, tn), lambda l, n: (l, 0, n)),
        ],
        out_specs=pl.BlockSpec((None, MOD_ROWS, tn), lambda l, n: (l, 0, n)),
        out_shape=jax.ShapeDtypeStruct((DEPTH, MOD_ROWS, n_out), F32),
        compiler_params=_params("arbitrary", "arbitrary"),
        name="ada_mod",
    )(cc, w_ada, b_ada.reshape(DEPTH, 1, n_out))


def _inproj_kernel(*refs, rope, n_prev, seq_len, n_ride=0, ride_len=0):
    if rope:
        x_ref, mod_ref, nw_ref, w_ref, cos_ref, sin_ref = refs[:6]
        z_ref = refs[6 + n_ride]
        if n_ride:
            _seqmix_kernel(*refs[6:6 + n_ride], *refs[7 + n_ride:], t_len=ride_len)
    else:
        x_ref, mod_ref, nw_ref, w_ref = refs[:4]
        prev_refs = refs[4:4 + 2 * n_prev]
        z_ref, ko_ref, vo_ref = refs[4 + 2 * n_prev:]
    x = x_ref[...]
    shift = mod_ref[:, 0:D_MODEL]
    scale = mod_ref[:, D_MODEL:2 * D_MODEL]
    h = _rms(x, nw_ref[...]) * (1.0 + scale) + shift
    z = jnp.dot(h.astype(BF16), w_ref[...], preferred_element_type=F32)
    if not rope:
        z_ref[...] = z.astype(BF16)
        for sq in range(x.shape[0] // seq_len):
            r0 = sq * seq_len
            if n_prev:
                for lp in range(n_prev):
                    ko_ref[sq, lp] = prev_refs[2 * lp][sq]
                    vo_ref[sq, lp] = prev_refs[2 * lp + 1][sq]
                ko_l, vo_l = ko_ref.at[sq, n_prev], vo_ref.at[sq, n_prev]
            else:
                ko_l, vo_l = ko_ref.at[sq], vo_ref.at[sq]
            for hd in range(N_ATT_HEADS):
                c0 = D_GROUP + hd * V_DIM
                ko_l[hd] = z[r0:r0 + seq_len, c0:c0 + V_DIM]
                vo_l[hd] = z[r0:r0 + seq_len, D_GROUP + c0:D_GROUP + c0 + V_DIM]
        return
    cos = cos_ref[...]
    sin = sin_ref[...]
    first_half = (lax.broadcasted_iota(jnp.int32, (1, LANES), 1) % 16) < 8
    for j in range(2 * D_GROUP // LANES):
        zc = z[:, j * LANES:(j + 1) * LANES]
        partner = jnp.where(first_half,
                            pltpu.roll(zc, LANES - 8, 1),
                            pltpu.roll(zc, 8, 1))
        z_ref[:, j * LANES:(j + 1) * LANES] = (zc * cos + partner * sin).astype(BF16)
    z_ref[:, 2 * D_GROUP:] = z[:, 2 * D_GROUP:].astype(BF16)


def _inproj(x, mod, layer, mod_row, norm_w, w_in, rope_tabs, tt, kv_prev=(), rider=None):
    b, t, _ = x.shape
    rope = rope_tabs is not None
    n_prev = len(kv_prev) // 2
    if rope:
        grid = (b, t // tt)
        tok = lambda i, j: (i, j, 0)
        x3 = x
    else:
        grid = (1, b * t // tt)
        tok = lambda i, j: (0, j, 0)
        x3 = x.reshape(1, b * t, D_MODEL)
    in_specs = [
        pl.BlockSpec((None, tt, D_MODEL), tok),
        pl.BlockSpec((None, None, 1, N_MOD * D_MODEL), lambda i, j: (layer, mod_row(i), 0, 0)),
        _resident((None, None, 1, D_MODEL), lambda i, j: (layer, 0, 0, 0)),
        _resident((None, D_MODEL, D_IN), lambda i, j: (layer, 0, 0)),
    ]
    args = [x3, mod, norm_w, w_in]
    out_specs = [pl.BlockSpec((None, tt, D_IN), tok)]
    out_shape = [jax.ShapeDtypeStruct(x3.shape[:2] + (D_IN,), BF16)]
    scratch, ride = [], {}
    if rope:
        in_specs += [pl.BlockSpec((tt, LANES), lambda i, j: (j, 0))] * 2
        args += list(rope_tabs)
        if rider is not None:
            parts = rider(grid)
            in_specs += parts["in_specs"]
            args += parts["args"]
            out_specs += parts["out_specs"]
            out_shape += parts["out_shape"]
            scratch = parts["scratch"]
            ride = dict(n_ride=len(parts["args"]), ride_len=parts["t_len"])
    else:
        n_seq = tt // t
        per_layer = pl.BlockSpec((n_seq, N_ATT_HEADS, t, V_DIM), lambda i, j: (j, 0, 0, 0))
        in_specs += [per_layer] * (2 * n_prev)
        args += list(kv_prev)
        if n_prev:
            out_specs += [pl.BlockSpec((n_seq, n_prev + 1, N_ATT_HEADS, t, V_DIM),
                                       lambda i, j: (j, 0, 0, 0, 0))] * 2
            out_shape += [jax.ShapeDtypeStruct((b, n_prev + 1, N_ATT_HEADS, t, V_DIM), F32)] * 2
        else:
            out_specs += [per_layer] * 2
            out_shape += [jax.ShapeDtypeStruct((b, N_ATT_HEADS, t, V_DIM), F32)] * 2
    outs = pl.pallas_call(
        functools.partial(_inproj_kernel, rope=rope, n_prev=n_prev, seq_len=t, **ride),
        grid=grid,
        in_specs=in_specs,
        out_specs=out_specs,
        out_shape=out_shape,
        scratch_shapes=scratch,
        compiler_params=_params("arbitrary", "arbitrary"),
        name="inproj",
    )(*args)
    return [outs[0].reshape(b, t, D_IN)] + list(outs[1:])


ATT_ROWS = 128
LOG2E = math.log2(math.e)


def _attn_kernel(*refs, has_ctx, lam_init, t_len, n_steps):
    if has_ctx:
        (q_ref, qn_ref, k_ref, v_ref, ck_ref, cv_ref, dl_ref, sw_ref, g_ref, o_ref,
         kt, vs, s0, s1, pb) = refs
    else:
        q_ref, qn_ref, k_ref, v_ref, dl_ref, sw_ref, g_ref, o_ref, kt, vs, s0, s1, pb = refs
    rows = ATT_ROWS
    past = ck_ref.shape[1] if has_ctx else 0
    s_len = past + t_len
    lane = lax.broadcasted_iota(jnp.int32, (1, D_GROUP), 1)

    def q_stack(q_rows):
        q = q_rows.astype(F32) * (DK ** -0.5 * LOG2E)
        return jnp.concatenate(
            [jnp.where((lane // DK) == hm, q, 0.0).astype(BF16) for hm in range(2 * N_ATT_HEADS)],
            axis=0)

    n_tiles = s_len // (2 * LANES)
    cuts = [2 * LANES * ((n_tiles * c) // N_ATT_HEADS) for c in range(N_ATT_HEADS + 1)]

    def scores_chunk(stack, dst, c):
        if cuts[c + 1] > cuts[c]:
            dst[:, cuts[c]:cuts[c + 1]] = jnp.dot(stack, kt[:, cuts[c]:cuts[c + 1]],
                                                  preferred_element_type=F32)

    def scores(q_rows, dst):
        stack = q_stack(q_rows)
        for c in range(N_ATT_HEADS):
            scores_chunk(stack, dst, c)

    @pl.when(pl.program_id(1) == 0)
    def _():
        if has_ctx:
            heads = range(N_ATT_HEADS)
            ck = jnp.concatenate([ck_ref[h] for h in heads], axis=-1)
            cv = jnp.concatenate([cv_ref[h] for h in heads], axis=-1)
            kt[:, 0:past] = ck.T.astype(BF16)
            vs[0:past, :] = cv.astype(BF16)
        kt[:, past:s_len] = k_ref[...].astype(F32).T.astype(BF16)
        vs[past:s_len, :] = v_ref[...]
        scores(q_ref[0:rows, :], s0)

    dl = dl_ref[...]
    lam = (jnp.exp(jnp.sum(dl[0:1] * dl[1:2], axis=-1, keepdims=True))
           - jnp.exp(jnp.sum(dl[2:3] * dl[3:4], axis=-1, keepdims=True)) + lam_init)

    def unit(src, out_lo, nxt_q, nxt_dst):
        stack = None if nxt_q is None else q_stack(nxt_q)
        inv_l1 = jnp.zeros((rows, D_GROUP), F32)
        for h in range(N_ATT_HEADS):
            if stack is not None:
                scores_chunk(stack, nxt_dst, h)
            es, ls = [], []
            for m in range(2):
                lo = (2 * h + m) * rows
                s = src[lo:lo + rows, :]
                e = jnp.exp2(s - jnp.max(s, axis=-1, keepdims=True))
                es.append(e)
                ls.append(jnp.sum(e, axis=-1, keepdims=True))
            beta = lam * ls[0] / ls[1]
            pb[h * rows:(h + 1) * rows, :] = (es[0] - beta * es[1]).astype(BF16)
            inv_l1 = jnp.where((lane // V_DIM) == h, 1.0 / ls[0], inv_l1)
        full = jnp.dot(pb[...], vs[...], preferred_element_type=F32)
        acc = full[0:rows]
        for h in range(1, N_ATT_HEADS):
            acc = jnp.where((lane // V_DIM) == h, full[h * rows:(h + 1) * rows], acc)
        acc = acc * inv_l1
        ms = jnp.dot(acc * acc, g_ref[...], preferred_element_type=F32,
                     precision=lax.Precision.HIGHEST)
        o_ref[out_lo:out_lo + rows, :] = (
            acc * lax.rsqrt(ms + EPS) * sw_ref[...] * (1.0 - lam_init)).astype(BF16)

    unit(s0, 0, q_ref[rows:2 * rows, :], s1)
    unit(s1, rows, qn_ref[0:rows, :] if n_steps > 1 else None, s0)


def _attn(z, ctx, layer, dl, subw, gmat, lam_init):
    b, t, _ = z.shape
    has_ctx = ctx is not None
    past = ctx[0].shape[3] if has_ctx else 0
    s_len = past + t
    tq = 2 * ATT_ROWS
    n_steps = t // tq
    in_specs = [
        pl.BlockSpec((None, tq, D_GROUP), lambda i, j: (i, j, 0)),
        pl.BlockSpec((None, tq, D_GROUP), lambda i, j: (i, jnp.minimum(j + 1, n_steps - 1), 0)),
        pl.BlockSpec((None, t, D_GROUP), lambda i, j: (i, 0, 1)),
        pl.BlockSpec((None, t, D_GROUP), lambda i, j: (i, 0, 2)),
    ]
    args = [z, z, z, z]
    if has_ctx:
        in_specs += [pl.BlockSpec((None, None, N_ATT_HEADS, past, V_DIM),
                                  lambda i, j: (i, layer, 0, 0, 0))] * 2
        args += list(ctx)
    in_specs += [
        _resident((None, 4, DK), lambda i, j: (layer, 0, 0)),
        _resident((None, 1, D_GROUP), lambda i, j: (layer, 0, 0)),
        _resident((D_GROUP, D_GROUP), lambda i, j: (0, 0)),
    ]
    args += [dl, subw, gmat]
    n_hm = 2 * N_ATT_HEADS
    return pl.pallas_call(
        functools.partial(_attn_kernel, has_ctx=has_ctx, lam_init=lam_init, t_len=t,
                          n_steps=n_steps),
        grid=(b, n_steps),
        in_specs=in_specs,
        out_specs=pl.BlockSpec((None, tq, D_GROUP), lambda i, j: (i, j, 0)),
        out_shape=jax.ShapeDtypeStruct((b, t, D_GROUP), BF16),
        scratch_shapes=[pltpu.VMEM((D_GROUP, s_len), BF16),
                        pltpu.VMEM((s_len, D_GROUP), BF16),
                        pltpu.VMEM((n_hm * ATT_ROWS, s_len), F32),
                        pltpu.VMEM((n_hm * ATT_ROWS, s_len), F32),
                        pltpu.VMEM((N_ATT_HEADS * ATT_ROWS, s_len), BF16)],
        compiler_params=_params("arbitrary", "arbitrary"),
        name="diff_attn",
    )(*args)


RG_ROWS = 256
RG_BLOCK = 64


def _strided8(ref, half, start):
    return ref[half, pl.ds(start, SUBLANES, stride=SUBLANES), :]


def _rg_gates(xc, wg, bg, c_dir):
    g = jnp.dot(xc.astype(BF16), wg, preferred_element_type=F32) + bg
    r = jax.nn.sigmoid(g[:, :D_GROUP])
    i = jax.nn.sigmoid(g[:, D_GROUP:])
    a = jnp.exp2(r * c_dir)
    om = 1.0 - a * a
    b = jnp.where(om > 0.0, om * lax.rsqrt(om), 0.0) * i * xc
    return a, b


def _scan_block(a, b, hp, reverse):
    n = SUBLANES
    order = list(range(n - 1, -1, -1)) if reverse else list(range(n))
    hs, cum = [None] * n, [None] * n
    prev = None
    for i in order:
        if prev is None:
            hs[i], cum[i] = b[i], a[i]
        else:
            hs[i], cum[i] = a[i] * hs[prev] + b[i], a[i] * cum[prev]
        prev = i
    p, f = cum[prev], hs[prev]
    row = lax.broadcasted_iota(jnp.int32, (n, LANES), 0)
    for d in (1, 2, 4):
        m = (row < n - d) if reverse else (row >= d)
        sh = n - d if reverse else d
        f = jnp.where(m, p * pltpu.roll(f, sh, 0) + f, f)
        p = jnp.where(m, p * pltpu.roll(p, sh, 0), p)
    first, last = (n - 1, 0) if reverse else (0, n - 1)
    sh = n - 1 if reverse else 1
    cin = jnp.where(row == first, hp, pltpu.roll(f, sh, 0) + pltpu.roll(p, sh, 0) * hp)
    out = [hs[i] + cum[i] * cin for i in range(n)]
    bc = lambda v: jnp.broadcast_to(v[last:last + 1, :], (n, LANES))
    return out, bc(f) + bc(p) * hp


def _rg_kernel(xr_ref, gr_ref, h0_ref, cw_ref, cb_ref, wg_ref, bg_ref, lam_ref,
               y_ref, last_ref, xs, gs, ys, xc_s, hf_s, *, t_len):
    n_chunks = t_len // RG_ROWS
    n_blk = RG_ROWS // RG_BLOCK
    zeros_halo = jnp.zeros((HALO, LANES), F32)
    for half in range(2):
        lo = half * LANES
        xs[half, 0:HALO, :] = zeros_halo
        xs[half, HALO + t_len:, :] = zeros_halo
        xs[half, HALO:HALO + t_len, :] = xr_ref[:, lo:lo + LANES].astype(F32)
        gs[half, :, :] = gr_ref[:, lo:lo + LANES].astype(F32)

    neg = -lam_ref[...]
    softplus = jnp.maximum(neg, 0.0) + jnp.log1p(jnp.exp(-jnp.abs(neg)))
    c_all = (-RG_C * LOG2E) * softplus
    cw = cw_ref[...]
    cb = cb_ref[...]

    def piece(v, blk, i, half):
        r0 = blk * RG_BLOCK + i * SUBLANES
        return v[r0:r0 + SUBLANES, half * LANES:(half + 1) * LANES]

    def scan_chunk(a, b, carry, reverse):
        carry = list(carry)
        out = {}
        for blk in (reversed(range(n_blk)) if reverse else range(n_blk)):
            for half in range(2):
                hs, carry[half] = _scan_block([piece(a, blk, i, half) for i in range(SUBLANES)],
                                              [piece(b, blk, i, half) for i in range(SUBLANES)],
                                              carry[half], reverse)
                for i in range(SUBLANES):
                    out[blk, i, half] = hs[i]
        rows = [jnp.concatenate([out[blk, i, 0], out[blk, i, 1]], axis=-1)
                for blk in range(n_blk) for i in range(SUBLANES)]
        return jnp.concatenate(rows, axis=0), tuple(carry)

    def start_state(d):
        return tuple(jnp.broadcast_to(h0_ref[d:d + 1, half * LANES:(half + 1) * LANES],
                                      (SUBLANES, LANES)) for half in range(2))

    def fwd_body(c, carry):
        s = pl.multiple_of(c * RG_ROWS, RG_ROWS)
        rows = []
        for blk in range(n_blk):
            base = s + blk * RG_BLOCK + HALO
            halves = []
            for half in range(2):
                lo = half * LANES
                x = {i: _strided8(xs, half, base + i) for i in range(-2, SUBLANES + 1)}
                halves.append([cw[0:1, lo:lo + LANES] * x[i - 2] + cw[1:2, lo:lo + LANES] * x[i - 1]
                               + cw[2:3, lo:lo + LANES] * x[i] + cw[3:4, lo:lo + LANES] * x[i + 1]
                               + cb[:, lo:lo + LANES] for i in range(SUBLANES)])
            rows += [jnp.concatenate([halves[0][i], halves[1][i]], axis=-1) for i in range(SUBLANES)]
        xc = jnp.concatenate(rows, axis=0)
        xc_s[pl.ds(s, RG_ROWS), :] = xc
        a, b = _rg_gates(xc, wg_ref[:, 0:2 * D_GROUP], bg_ref[:, 0:2 * D_GROUP], c_all[0:1])
        hf, carry = scan_chunk(a, b, carry, False)
        hf_s[pl.ds(s, RG_ROWS), :] = hf
        return carry

    h_f = lax.fori_loop(0, n_chunks, fwd_body, start_state(0))
    last_ref[0:1, :] = jnp.concatenate([h_f[0][0:1], h_f[1][0:1]], axis=-1)

    def bwd_body(ci, carry):
        c = n_chunks - 1 - ci
        s = pl.multiple_of(c * RG_ROWS, RG_ROWS)
        xc = xc_s[pl.ds(s, RG_ROWS), :]
        a, b = _rg_gates(xc, wg_ref[:, 2 * D_GROUP:], bg_ref[:, 2 * D_GROUP:], c_all[1:2])
        hb, carry = scan_chunk(a, b, carry, True)
        gr = jnp.concatenate(
            [jnp.concatenate([_strided8(gs, half, s + blk * RG_BLOCK + i) for half in range(2)], axis=-1)
             for blk in range(n_blk) for i in range(SUBLANES)], axis=0)
        k_gelu = math.sqrt(2.0 / math.pi)
        hg = 0.5 * gr
        gelu = hg + hg * jnp.tanh(gr * (k_gelu + (k_gelu * 0.044715) * (gr * gr)))
        y = (hf_s[pl.ds(s, RG_ROWS), :] + hb) * gelu
        for blk in range(n_blk):
            for i in range(SUBLANES):
                for half in range(2):
                    ys[half, pl.ds(s + blk * RG_BLOCK + i, SUBLANES, stride=SUBLANES), :] = (
                        piece(y, blk, i, half))
        y_ref[pl.ds(s, RG_ROWS), :] = jnp.concatenate(
            [ys[0, pl.ds(s, RG_ROWS), :], ys[1, pl.ds(s, RG_ROWS), :]], axis=-1).astype(BF16)
        return carry

    h_b = lax.fori_loop(0, n_chunks, bwd_body, start_state(1))
    last_ref[1:2, :] = jnp.concatenate([h_b[0][0:1], h_b[1][0:1]], axis=-1)


CP_ROWS = 256


def _window_sums(x, lo, hi, widths):
    out, prev, w = {}, {i: x[i] for i in range(lo, hi + 1)}, 1
    while w < max(widths):
        prev = {i: prev[i] + prev[i - w] for i in prev if i - w in prev}
        w *= 2
        out[w] = prev
    return out


def _convpool_kernel(gb_ref, gc_ref, xb_ref, xp_ref, cw_ref, pw_ref, ps_ref,
                     yb_ref, yd_ref, us, xs, cs, ds, *, t_len):
    n_chunks = t_len // CP_ROWS
    n_blk = CP_ROWS // RG_BLOCK
    zeros_halo = jnp.zeros((HALO, LANES), F32)
    for half in range(2):
        for pad in (us, xs):
            pad[half, 0:HALO, :] = zeros_halo
            pad[half, HALO + t_len:, :] = zeros_halo

    def fill(c, carry):
        s = pl.multiple_of(c * CP_ROWS, CP_ROWS)
        u = gc_ref[pl.ds(s, CP_ROWS), :].astype(F32) * xb_ref[pl.ds(s, CP_ROWS), :].astype(F32)
        x = xp_ref[pl.ds(s, CP_ROWS), :].astype(F32)
        for half in range(2):
            us[half, pl.ds(HALO + s, CP_ROWS), :] = u[:, half * LANES:(half + 1) * LANES]
            xs[half, pl.ds(HALO + s, CP_ROWS), :] = x[:, half * LANES:(half + 1) * LANES]
        return carry

    lax.fori_loop(0, n_chunks, fill, 0)

    cw = cw_ref[...]
    low_group = lax.broadcasted_iota(jnp.int32, (1, LANES), 1) < POOL_GW
    sub8 = lax.broadcasted_iota(jnp.int32, (SUBLANES, LANES), 0) * SUBLANES
    wins = [jnp.where(low_group, POOL_WINDOWS[2 * half], POOL_WINDOWS[2 * half + 1])
            for half in range(2)]

    def body(c, carry):
        s = pl.multiple_of(c * CP_ROWS, CP_ROWS)
        pm_rows = []
        for blk in range(n_blk):
            r0 = s + blk * RG_BLOCK
            halves = []
            for half in range(2):
                lo = half * LANES
                u = {i: _strided8(us, half, r0 + HALO + i) for i in range(-1, SUBLANES + 1)}
                for i in range(SUBLANES):
                    cs[half, pl.ds(r0 + i, SUBLANES, stride=SUBLANES), :] = (
                        cw[0:1, lo:lo + LANES] * u[i - 1] + cw[1:2, lo:lo + LANES] * u[i]
                        + cw[2:3, lo:lo + LANES] * u[i + 1])
                w_lo, w_hi = POOL_WINDOWS[2 * half], POOL_WINDOWS[2 * half + 1]
                p_lo, p_hi = -(w_hi // 2), SUBLANES - 1 + w_hi // 2 - 1
                x = {i: _strided8(xs, half, r0 + HALO + i) for i in range(p_lo, p_hi + 1)}
                sums = _window_sums(x, p_lo, p_hi, (w_lo, w_hi))
                win = wins[half]
                left = win // 2
                right = win - 1 - left
                pieces = []
                for i in range(SUBLANES):
                    tot = jnp.where(low_group, sums[w_lo][i + w_lo // 2 - 1], sums[w_hi][i + w_hi // 2 - 1])
                    t = sub8 + (r0 + i)
                    cnt = jnp.minimum(t + right, t_len - 1) - jnp.maximum(t - left, 0) + 1
                    pieces.append(tot / cnt.astype(F32) - x[i])
                halves.append(pieces)
            pm_rows += [jnp.concatenate([halves[0][i], halves[1][i]], axis=-1)
                        for i in range(SUBLANES)]
        pm = jnp.concatenate(pm_rows, axis=0)
        yd = jnp.dot(pm.astype(BF16), pw_ref[...], preferred_element_type=F32) * ps_ref[...]
        for blk in range(n_blk):
            for i in range(SUBLANES):
                k0 = blk * RG_BLOCK + i * SUBLANES
                for half in range(2):
                    ds[half, pl.ds(s + blk * RG_BLOCK + i, SUBLANES, stride=SUBLANES), :] = (
                        yd[k0:k0 + SUBLANES, half * LANES:(half + 1) * LANES])
        conv = jnp.concatenate([cs[0, pl.ds(s, CP_ROWS), :], cs[1, pl.ds(s, CP_ROWS), :]], axis=-1)
        yb_ref[pl.ds(s, CP_ROWS), :] = (gb_ref[pl.ds(s, CP_ROWS), :].astype(F32) * conv).astype(BF16)
        yd_ref[pl.ds(s, CP_ROWS), :] = jnp.concatenate(
            [ds[0, pl.ds(s, CP_ROWS), :], ds[1, pl.ds(s, CP_ROWS), :]], axis=-1).astype(BF16)
        return carry

    lax.fori_loop(0, n_chunks, body, 0)


def _seqmix_kernel(xr_ref, gr_ref, gb_ref, gc_ref, xb_ref, xp_ref, h0_ref,
                   ccw_ref, ccb_ref, wg_ref, bg_ref, lam_ref, bcw_ref, pw_ref, ps_ref,
                   yb_ref, yc_ref, yd_ref, last_ref,
                   xs, gs, ys, xc_s, hf_s, us, ps_s, ds, *, t_len):
    _rg_kernel(xr_ref, gr_ref, h0_ref, ccw_ref, ccb_ref, wg_ref, bg_ref, lam_ref,
               yc_ref, last_ref, xs, gs, ys, xc_s, hf_s, t_len=t_len)
    _convpool_kernel(gb_ref, gc_ref, xb_ref, xp_ref, bcw_ref, pw_ref, ps_ref,
                     yb_ref, yd_ref, us, ps_s, ys, ds, t_len=t_len)


def _seqmix_parts(z, h0, layer, conv_c_w, conv_c_b, wg, bg, lam, conv_b_w, pw, ps, seq_of):
    b, t, _ = z.shape
    col = lambda k: pl.BlockSpec((None, t, D_GROUP), lambda *g: (seq_of(*g), 0, k))
    out = pl.BlockSpec((None, t, D_GROUP), lambda *g: (seq_of(*g), 0, 0))
    per_layer = lambda *dims: _resident((None,) + dims, lambda *g: (layer,) + (0,) * len(dims))
    slab = lambda rows: pltpu.VMEM((2, rows, LANES), F32)
    if h0.ndim == 4:
        h0_spec = pl.BlockSpec((None, None, 2, D_GROUP), lambda *g: (seq_of(*g), layer, 0, 0))
    else:
        h0_spec = pl.BlockSpec((None, 2, D_GROUP), lambda *g: (seq_of(*g), 0, 0))
    return dict(
        args=[z, z, z, z, z, z, h0, conv_c_w, conv_c_b, wg, bg, lam, conv_b_w, pw, ps],
        in_specs=[col(6), col(7), col(3), col(4), col(5), col(8), h0_spec,
                  per_layer(4, D_GROUP), per_layer(1, D_GROUP), per_layer(D_GROUP, 4 * D_GROUP),
                  per_layer(1, 4 * D_GROUP), per_layer(2, D_GROUP), per_layer(3, D_GROUP),
                  per_layer(D_GROUP, D_GROUP), per_layer(1, D_GROUP)],
        out_specs=[out, out, out,
                   pl.BlockSpec((None, 2, D_GROUP), lambda *g: (seq_of(*g), 0, 0))],
        out_shape=[jax.ShapeDtypeStruct((b, t, D_GROUP), BF16)] * 3
        + [jax.ShapeDtypeStruct((b, 2, D_GROUP), F32)],
        scratch=[slab(t + 2 * HALO), slab(t), slab(t),
                 pltpu.VMEM((t, D_GROUP), F32), pltpu.VMEM((t, D_GROUP), F32),
                 slab(t + 2 * HALO), slab(t + 2 * HALO), slab(t)],
        t_len=t)


def _seqmix(z, h0, *weights):
    parts = _seqmix_parts(z, h0, *weights, seq_of=lambda i: i)
    yb, yc, yd, last = pl.pallas_call(
        functools.partial(_seqmix_kernel, t_len=parts["t_len"]),
        grid=(z.shape[0],),
        in_specs=parts["in_specs"],
        out_specs=parts["out_specs"],
        out_shape=parts["out_shape"],
        scratch_shapes=parts["scratch"],
        compiler_params=_params("arbitrary"),
        name="seqmix",
    )(*parts["args"])
    return (yb, yc, yd), last


FF_CHUNK = 1024


def _mlp_kernel(x_ref, ya_ref, yb_ref, yc_ref, yd_ref, mod_ref, nw_ref, wo_ref, w1_ref, w2_ref,
                fw_ref, o_ref, *, final):
    x = x_ref[...]
    y = jnp.zeros(x.shape, F32)
    for i, r in enumerate((ya_ref, yb_ref, yc_ref, yd_ref)):
        y = y + jnp.dot(r[...], wo_ref[i * D_GROUP:(i + 1) * D_GROUP, :],
                        preferred_element_type=F32)
    gate1 = mod_ref[:, 2 * D_MODEL:3 * D_MODEL]
    shift2 = mod_ref[:, 3 * D_MODEL:4 * D_MODEL]
    scale2 = mod_ref[:, 4 * D_MODEL:5 * D_MODEL]
    gate2 = mod_ref[:, 5 * D_MODEL:6 * D_MODEL]
    x1 = x + gate1 * y
    hn = (_rms(x1, nw_ref[...]) * (1.0 + scale2) + shift2).astype(BF16)
    acc = jnp.zeros(x.shape, F32)
    for f in range(D_FF // FF_CHUNK):
        u = jnp.dot(hn, w1_ref[:, f * FF_CHUNK:(f + 1) * FF_CHUNK], preferred_element_type=F32)
        u = jnp.square(jnp.maximum(u, 0.0)).astype(BF16)
        acc = acc + jnp.dot(u, w2_ref[f * FF_CHUNK:(f + 1) * FF_CHUNK, :], preferred_element_type=F32)
    x2 = x1 + gate2 * acc
    if final:
        x2 = _rms(x2, fw_ref[...])
    o_ref[...] = x2


def _mlp(x, ys, mod, layer, mod_row, norm_w, wo, w1, w2, fw, final, tt, shared_mod=False):
    b, t, _ = x.shape
    if shared_mod:
        x, ys = x.reshape(1, b * t, D_MODEL), [y.reshape(1, b * t, D_GROUP) for y in ys]
    nb, nt, _ = x.shape
    ytile = pl.BlockSpec((None, tt, D_GROUP), lambda i, j: (i, j, 0))
    out = pl.pallas_call(
        functools.partial(_mlp_kernel, final=final),
        grid=(nb, nt // tt),
        in_specs=[
            pl.BlockSpec((None, tt, D_MODEL), lambda i, j: (i, j, 0)),
            ytile, ytile, ytile, ytile,
            pl.BlockSpec((None, None, 1, N_MOD * D_MODEL), lambda i, j: (layer, mod_row(i), 0, 0)),
            _resident((None, None, 1, D_MODEL), lambda i, j: (layer, 1, 0, 0)),
            _resident((None, D_MODEL, D_MODEL), lambda i, j: (layer, 0, 0)),
            _resident((None, D_MODEL, D_FF), lambda i, j: (layer, 0, 0)),
            _resident((None, D_FF, D_MODEL), lambda i, j: (layer, 0, 0)),
            _resident((1, D_MODEL), lambda i, j: (0, 0)),
        ],
        out_specs=pl.BlockSpec((None, tt, D_MODEL), lambda i, j: (i, j, 0)),
        out_shape=jax.ShapeDtypeStruct((nb, nt, D_MODEL), F32),
        compiler_params=_params("arbitrary", "arbitrary"),
        name="outproj_mlp",
    )(x, *ys, mod, norm_w, wo, w1, w2, fw)
    return out.reshape(b, t, D_MODEL)


def _block_diag(w):
    *lead, n, k, _ = w.shape
    eye = jnp.eye(n, dtype=w.dtype)
    return (eye[:, None, :, None] * w[..., :, :, None, :]).reshape(*lead, n * k, n * k)


def _rope_tables(t_len):
    half = DK // 4
    freqs = ROPE_BASE ** (-np.arange(half, dtype=np.float64) / half)
    lane = np.arange(LANES)
    m = lane % DK
    use_col = m >= DK // 2
    fidx = m % half
    t = np.arange(t_len)
    pos = np.where(use_col[None, :], (t % GRID_W)[:, None], (t // GRID_W)[:, None])
    ang = pos * freqs[fidx][None, :]
    sign = np.where((lane % (2 * half)) < half, -1.0, 1.0)
    return (jnp.asarray(np.cos(ang), dtype=F32), jnp.asarray(np.sin(ang) * sign[None, :], dtype=F32))


def kernel(x_prompt, x_sample, cache_k, cache_v, state_rglru, c, c_ctx, w_ada, b_ada, norm_w, w_in, diff_lambda, subln_w, conv_b_w, conv_c_w, conv_c_b, rg_w, rg_b, rg_lambda, pool_w, pool_scale, w_out, w_mlp1, w_mlp2, final_norm_w):
    n_lat = c.shape[0]
    ctx_row = n_lat
    cc = jnp.zeros((MOD_ROWS, D_MODEL), F32).at[:n_lat].set(c).at[ctx_row].set(c_ctx)
    mod = _ada(cc, w_ada, b_ada).reshape(DEPTH, MOD_ROWS, 1, N_MOD * D_MODEL)

    w_in_b = w_in.astype(BF16)
    w_out_b = w_out.astype(BF16)
    w1_b = w_mlp1.astype(BF16)
    w2_b = w_mlp2.astype(BF16)
    norm_w4 = norm_w.reshape(DEPTH, 2, 1, D_MODEL)
    rope_tabs = _rope_tables(x_sample.shape[1])
    gmat = _block_diag(jnp.full((N_ATT_HEADS, V_DIM, V_DIM), 1.0 / V_DIM, F32))
    subw = jnp.tile(subln_w, (1, N_ATT_HEADS)).reshape(DEPTH, 1, D_GROUP)
    wg = _block_diag(rg_w).transpose(0, 3, 1, 2, 4).reshape(DEPTH, D_GROUP, 4 * D_GROUP).astype(BF16)
    bg = rg_b.reshape(DEPTH, 1, 4 * D_GROUP)
    pw = _block_diag(pool_w).astype(BF16)
    ps = pool_scale.reshape(DEPTH, 1, D_GROUP)
    cb = conv_c_b.reshape(DEPTH, 1, D_GROUP)
    fw = final_norm_w.reshape(1, D_MODEL)
    zero_state = jnp.zeros((x_prompt.shape[0], 2, D_GROUP), F32)

    xp, xs = x_prompt, x_sample
    kv_out, hs_out = [], []
    for l in range(DEPTH):
        lam_init = 0.8 - 0.6 * math.exp(-0.3 * l)
        final = l == DEPTH - 1

        mix_w = (l, conv_c_w, cb, wg, bg, rg_lambda, conv_b_w, pw, ps)
        ctx_rows = lambda i: ctx_row
        lat_rows = lambda i: i

        kv_prev = kv_out if final else ()
        zp, k_new, v_new = _inproj(xp, mod, l, ctx_rows, norm_w4, w_in_b, None, TOKEN_TILE, kv_prev)
        kv_out += [k_new, v_new]
        ya_p = _attn(zp, None, l, diff_lambda, subw, gmat, lam_init)

        def ctx_mixers(grid):
            assert grid[0] * grid[1] == zp.shape[0]
            return _seqmix_parts(zp, zero_state, *mix_w, seq_of=lambda i, j: i * grid[1] + j)

        zs, yb_p, yc_p, yd_p, last_p = _inproj(xs, mod, l, lat_rows, norm_w4, w_in_b, rope_tabs,
                                               TOKEN_TILE, rider=ctx_mixers)
        hs_out.append(last_p)
        xp = _mlp(xp, (ya_p, yb_p, yc_p, yd_p), mod, l, ctx_rows, norm_w4, w_out_b, w1_b, w2_b,
                  fw, final, MLP_TILE, shared_mod=True)

        ya_s = _attn(zs, (cache_k, cache_v), l, diff_lambda, subw, gmat, lam_init)
        ys_s, _ = _seqmix(zs, state_rglru, *mix_w)
        xs = _mlp(xs, (ya_s,) + ys_s, mod, l, lat_rows, norm_w4, w_out_b, w1_b, w2_b, fw, final,
                  MLP_TILE)

    return (xp, xs, kv_out[-2], kv_out[-1], jnp.stack(hs_out, axis=1))
```

```python
import functools
import math

import jax
import jax.numpy as jnp
import numpy as np
from jax import lax
from jax.experimental import pallas as pl
from jax.experimental.pallas import tpu as pltpu

D_MODEL = 1024
DEPTH = 2
GRID_W = 64
D_GROUP = 256
N_ATT_HEADS = 4
V_DIM = 64
DK = 32
ROPE_BASE = 10000.0
RG_BLOCKS = 4
RG_BW = 64
RG_C = 8.0
POOL_WINDOWS = (2, 4, 8, 16)
POOL_GW = 64
D_FF = 4 * D_MODEL
N_MOD = 6
D_IN = 9 * D_GROUP
EPS = 1e-6

LANES = 128
SUBLANES = 8
HALO = 8
VMEM_LIMIT = 56 * 1024 * 1024
MOD_ROWS = 16
TOKEN_TILE = 512
INPROJ_TILE = 1024
INPROJ_SPLIT = 4

BF16 = jnp.bfloat16
F32 = jnp.float32


def _params(*sem):
    return pltpu.CompilerParams(dimension_semantics=sem, vmem_limit_bytes=VMEM_LIMIT)


def _resident(shape, index_map):
    return pl.BlockSpec(shape, index_map, pipeline_mode=pl.Buffered(1))


def _rms(x, w):
    ms = jnp.mean(x * x, axis=-1, keepdims=True)
    return x * lax.rsqrt(ms + EPS) * w


def _ada_kernel(c_ref, w_ref, b_ref, o_ref):
    c = c_ref[...]
    s = (c * jax.nn.sigmoid(c)).astype(BF16)
    o_ref[...] = jnp.dot(s, w_ref[...].astype(BF16), preferred_element_type=F32) + b_ref[...]


def _ada(cc, w_ada, b_ada):
    tn = 1536
    n_out = N_MOD * D_MODEL
    return pl.pallas_call(
        _ada_kernel,
        grid=(DEPTH, n_out // tn),
        in_specs=[
            pl.BlockSpec((MOD_ROWS, D_MODEL), lambda l, n: (0, 0)),
            pl.BlockSpec((None, D_MODEL, tn), lambda l, n: (l, 0, n)),
            pl.BlockSpec((None, 1, tn), lambda l, n: (l, 0, n)),
        ],
        out_specs=pl.BlockSpec((None, MOD_ROWS, tn), lambda l, n: (l, 0, n)),
        out_shape=jax.ShapeDtypeStruct((DEPTH, MOD_ROWS, n_out), F32),
        compiler_params=_params("arbitrary", "arbitrary"),
        name="ada_mod",
    )(cc, w_ada, b_ada.reshape(DEPTH, 1, n_out))


def _inproj_kernel(*refs, rope, n_prev, seq_len):
    if rope:
        x_ref, mod_ref, nw_ref, w_ref, cos_ref, sin_ref, z_ref = refs
    else:
        x_ref, mod_ref, nw_ref, w_ref = refs[:4]
        prev_refs = refs[4:4 + 2 * n_prev]
        z_ref, ko_ref, vo_ref = refs[4 + 2 * n_prev:]
    shift = mod_ref[:, 0:D_MODEL]
    scale = mod_ref[:, D_MODEL:2 * D_MODEL]
    n_rows = x_ref.shape[0]
    sub = n_rows // INPROJ_SPLIT
    for r0 in range(0, n_rows, sub):
        x = x_ref[r0:r0 + sub, :]
        h = _rms(x, nw_ref[...]) * (1.0 + scale) + shift
        z = jnp.dot(h.astype(BF16), w_ref[...], preferred_element_type=F32)
        if not rope:
            z_ref[r0:r0 + sub, :] = z.astype(BF16)
            for q0 in range(0, sub, seq_len):
                sq = (r0 + q0) // seq_len
                if n_prev:
                    for lp in range(n_prev):
                        ko_ref[sq, lp] = prev_refs[2 * lp][sq]
                        vo_ref[sq, lp] = prev_refs[2 * lp + 1][sq]
                    ko_l, vo_l = ko_ref.at[sq, n_prev], vo_ref.at[sq, n_prev]
                else:
                    ko_l, vo_l = ko_ref.at[sq], vo_ref.at[sq]
                for hd in range(N_ATT_HEADS):
                    c0 = D_GROUP + hd * V_DIM
                    ko_l[hd] = z[q0:q0 + seq_len, c0:c0 + V_DIM]
                    vo_l[hd] = z[q0:q0 + seq_len, D_GROUP + c0:D_GROUP + c0 + V_DIM]
            continue
        cos = cos_ref[r0:r0 + sub, :]
        sin = sin_ref[r0:r0 + sub, :]
        first_half = (lax.broadcasted_iota(jnp.int32, (1, LANES), 1) % 16) < 8
        for j in range(2 * D_GROUP // LANES):
            zc = z[:, j * LANES:(j + 1) * LANES]
            partner = jnp.where(first_half,
                                pltpu.roll(zc, LANES - 8, 1),
                                pltpu.roll(zc, 8, 1))
            z_ref[r0:r0 + sub, j * LANES:(j + 1) * LANES] = (zc * cos + partner * sin).astype(BF16)
        z_ref[r0:r0 + sub, 2 * D_GROUP:] = z[:, 2 * D_GROUP:].astype(BF16)


def _inproj(x, mod, layer, mod_row, norm_w, w_in, rope_tabs, tt, kv_prev=()):
    b, t, _ = x.shape
    rope = rope_tabs is not None
    n_prev = len(kv_prev) // 2
    if rope:
        grid = (b, t // tt)
        tok = lambda i, j: (i, j, 0)
        x3 = x
    else:
        grid = (1, b * t // tt)
        tok = lambda i, j: (0, j, 0)
        x3 = x.reshape(1, b * t, D_MODEL)
    in_specs = [
        pl.BlockSpec((None, tt, D_MODEL), tok),
        pl.BlockSpec((None, None, 1, N_MOD * D_MODEL), lambda i, j: (layer, mod_row(i), 0, 0)),
        _resident((None, None, 1, D_MODEL), lambda i, j: (layer, 0, 0, 0)),
        _resident((None, D_MODEL, D_IN), lambda i, j: (layer, 0, 0)),
    ]
    args = [x3, mod, norm_w, w_in]
    out_specs = [pl.BlockSpec((None, tt, D_IN), tok)]
    out_shape = [jax.ShapeDtypeStruct(x3.shape[:2] + (D_IN,), BF16)]
    if rope:
        in_specs += [pl.BlockSpec((tt, LANES), lambda i, j: (j, 0))] * 2
        args += list(rope_tabs)
    else:
        n_seq = tt // t
        per_layer = pl.BlockSpec((n_seq, N_ATT_HEADS, t, V_DIM), lambda i, j: (j, 0, 0, 0))
        in_specs += [per_layer] * (2 * n_prev)
        args += list(kv_prev)
        if n_prev:
            out_specs += [pl.BlockSpec((n_seq, n_prev + 1, N_ATT_HEADS, t, V_DIM),
                                       lambda i, j: (j, 0, 0, 0, 0))] * 2
            out_shape += [jax.ShapeDtypeStruct((b, n_prev + 1, N_ATT_HEADS, t, V_DIM), F32)] * 2
        else:
            out_specs += [per_layer] * 2
            out_shape += [jax.ShapeDtypeStruct((b, N_ATT_HEADS, t, V_DIM), F32)] * 2
    outs = pl.pallas_call(
        functools.partial(_inproj_kernel, rope=rope, n_prev=n_prev, seq_len=t),
        grid=grid,
        in_specs=in_specs,
        out_specs=out_specs,
        out_shape=out_shape,
        compiler_params=_params("arbitrary", "arbitrary"),
        name="inproj",
    )(*args)
    return [outs[0].reshape(b, t, D_IN)] + list(outs[1:])


ATT_ROWS = 128
LOG2E = math.log2(math.e)


def _attn_kernel(*refs, has_ctx, lam_init, t_len, n_steps):
    if has_ctx:
        (q_ref, qn_ref, k_ref, v_ref, ck_ref, cv_ref, dl_ref, sw_ref, g_ref, o_ref,
         kt, vs, s0, s1, pb) = refs
    else:
        q_ref, qn_ref, k_ref, v_ref, dl_ref, sw_ref, g_ref, o_ref, kt, vs, s0, s1, pb = refs
    rows = ATT_ROWS
    past = ck_ref.shape[1] if has_ctx else 0
    s_len = past + t_len
    lane = lax.broadcasted_iota(jnp.int32, (1, D_GROUP), 1)

    def scores(q_rows, dst):
        q = q_rows.astype(F32) * (DK ** -0.5 * LOG2E)
        stack = jnp.concatenate(
            [jnp.where((lane // DK) == hm, q, 0.0).astype(BF16) for hm in range(2 * N_ATT_HEADS)],
            axis=0)
        dst[...] = jnp.dot(stack, kt[...], preferred_element_type=F32)

    @pl.when(pl.program_id(1) == 0)
    def _():
        if has_ctx:
            heads = range(N_ATT_HEADS)
            ck = jnp.concatenate([ck_ref[h] for h in heads], axis=-1)
            cv = jnp.concatenate([cv_ref[h] for h in heads], axis=-1)
            kt[:, 0:past] = ck.T.astype(BF16)
            vs[0:past, :] = cv.astype(BF16)
        kt[:, past:s_len] = k_ref[...].astype(F32).T.astype(BF16)
        vs[past:s_len, :] = v_ref[...]
        scores(q_ref[0:rows, :], s0)

    dl = dl_ref[...]
    lam = (jnp.exp(jnp.sum(dl[0:1] * dl[1:2], axis=-1, keepdims=True))
           - jnp.exp(jnp.sum(dl[2:3] * dl[3:4], axis=-1, keepdims=True)) + lam_init)

    def finish(src, out_lo):
        inv_l1 = jnp.zeros((rows, D_GROUP), F32)
        for h in range(N_ATT_HEADS):
            es, ls = [], []
            for m in range(2):
                lo = (2 * h + m) * rows
                s = src[lo:lo + rows, :]
                e = jnp.exp2(s - jnp.max(s, axis=-1, keepdims=True))
                es.append(e)
                ls.append(jnp.sum(e, axis=-1, keepdims=True))
            beta = lam * ls[0] / ls[1]
            pb[h * rows:(h + 1) * rows, :] = (es[0] - beta * es[1]).astype(BF16)
            inv_l1 = jnp.where((lane // V_DIM) == h, 1.0 / ls[0], inv_l1)
        full = jnp.dot(pb[...], vs[...], preferred_element_type=F32)
        acc = full[0:rows]
        for h in range(1, N_ATT_HEADS):
            acc = jnp.where((lane // V_DIM) == h, full[h * rows:(h + 1) * rows], acc)
        acc = acc * inv_l1
        ms = jnp.dot(acc * acc, g_ref[...], preferred_element_type=F32,
                     precision=lax.Precision.HIGHEST)
        o_ref[out_lo:out_lo + rows, :] = (
            acc * lax.rsqrt(ms + EPS) * sw_ref[...] * (1.0 - lam_init)).astype(BF16)

    scores(q_ref[rows:2 * rows, :], s1)
    finish(s0, 0)
    if n_steps > 1:
        scores(qn_ref[0:rows, :], s0)
    finish(s1, rows)


def _attn(z, ctx, layer, dl, subw, gmat, lam_init):
    b, t, _ = z.shape
    has_ctx = ctx is not None
    past = ctx[0].shape[3] if has_ctx else 0
    s_len = past + t
    tq = 2 * ATT_ROWS
    n_steps = t // tq
    in_specs = [
        pl.BlockSpec((None, tq, D_GROUP), lambda i, j: (i, j, 0)),
        pl.BlockSpec((None, tq, D_GROUP), lambda i, j: (i, jnp.minimum(j + 1, n_steps - 1), 0)),
        pl.BlockSpec((None, t, D_GROUP), lambda i, j: (i, 0, 1)),
        pl.BlockSpec((None, t, D_GROUP), lambda i, j: (i, 0, 2)),
    ]
    args = [z, z, z, z]
    if has_ctx:
        in_specs += [pl.BlockSpec((None, None, N_ATT_HEADS, past, V_DIM),
                                  lambda i, j: (i, layer, 0, 0, 0))] * 2
        args += list(ctx)
    in_specs += [
        _resident((None, 4, DK), lambda i, j: (layer, 0, 0)),
        _resident((None, 1, D_GROUP), lambda i, j: (layer, 0, 0)),
        _resident((D_GROUP, D_GROUP), lambda i, j: (0, 0)),
    ]
    args += [dl, subw, gmat]
    n_hm = 2 * N_ATT_HEADS
    return pl.pallas_call(
        functools.partial(_attn_kernel, has_ctx=has_ctx, lam_init=lam_init, t_len=t,
                          n_steps=n_steps),
        grid=(b, n_steps),
        in_specs=in_specs,
        out_specs=pl.BlockSpec((None, tq, D_GROUP), lambda i, j: (i, j, 0)),
        out_shape=jax.ShapeDtypeStruct((b, t, D_GROUP), BF16),
        scratch_shapes=[pltpu.VMEM((D_GROUP, s_len), BF16),
                        pltpu.VMEM((s_len, D_GROUP), BF16),
                        pltpu.VMEM((n_hm * ATT_ROWS, s_len), F32),
                        pltpu.VMEM((n_hm * ATT_ROWS, s_len), F32),
                        pltpu.VMEM((N_ATT_HEADS * ATT_ROWS, s_len), BF16)],
        compiler_params=_params("arbitrary", "arbitrary"),
        name="diff_attn",
    )(*args)


RG_ROWS = 256
RG_BLOCK = 64


def _strided8(ref, half, start):
    return ref[half, pl.ds(start, SUBLANES, stride=SUBLANES), :]


def _rg_gates(xc, wg, bg, c_dir):
    g = jnp.dot(xc.astype(BF16), wg, preferred_element_type=F32) + bg
    r = jax.nn.sigmoid(g[:, :D_GROUP])
    i = jax.nn.sigmoid(g[:, D_GROUP:])
    a = jnp.exp2(r * c_dir)
    om = 1.0 - a * a
    b = jnp.where(om > 0.0, om * lax.rsqrt(om), 0.0) * i * xc
    return a, b


def _scan_block(a, b, hp, reverse):
    n = SUBLANES
    order = list(range(n - 1, -1, -1)) if reverse else list(range(n))
    hs, cum = [None] * n, [None] * n
    prev = None
    for i in order:
        if prev is None:
            hs[i], cum[i] = b[i], a[i]
        else:
            hs[i], cum[i] = a[i] * hs[prev] + b[i], a[i] * cum[prev]
        prev = i
    p, f = cum[prev], hs[prev]
    row = lax.broadcasted_iota(jnp.int32, (n, LANES), 0)
    for d in (1, 2, 4):
        m = (row < n - d) if reverse else (row >= d)
        sh = n - d if reverse else d
        f = jnp.where(m, p * pltpu.roll(f, sh, 0) + f, f)
        p = jnp.where(m, p * pltpu.roll(p, sh, 0), p)
    first, last = (n - 1, 0) if reverse else (0, n - 1)
    sh = n - 1 if reverse else 1
    cin = jnp.where(row == first, hp, pltpu.roll(f, sh, 0) + pltpu.roll(p, sh, 0) * hp)
    out = [hs[i] + cum[i] * cin for i in range(n)]
    bc = lambda v: jnp.broadcast_to(v[last:last + 1, :], (n, LANES))
    return out, bc(f) + bc(p) * hp


def _rg_kernel(xr_ref, gr_ref, h0_ref, cw_ref, cb_ref, wg_ref, bg_ref, lam_ref,
               y_ref, last_ref, xs, gs, ys, xc_s, hf_s, *, t_len):
    n_chunks = t_len // RG_ROWS
    n_blk = RG_ROWS // RG_BLOCK
    zeros_halo = jnp.zeros((HALO, LANES), F32)
    for half in range(2):
        lo = half * LANES
        xs[half, 0:HALO, :] = zeros_halo
        xs[half, HALO + t_len:, :] = zeros_halo
        xs[half, HALO:HALO + t_len, :] = xr_ref[:, lo:lo + LANES].astype(F32)
        gs[half, :, :] = gr_ref[:, lo:lo + LANES].astype(F32)

    neg = -lam_ref[...]
    softplus = jnp.maximum(neg, 0.0) + jnp.log1p(jnp.exp(-jnp.abs(neg)))
    c_all = (-RG_C * LOG2E) * softplus
    cw = cw_ref[...]
    cb = cb_ref[...]

    def piece(v, blk, i, half):
        r0 = blk * RG_BLOCK + i * SUBLANES
        return v[r0:r0 + SUBLANES, half * LANES:(half + 1) * LANES]

    def scan_chunk(a, b, carry, reverse):
        carry = list(carry)
        out = {}
        for blk in (reversed(range(n_blk)) if reverse else range(n_blk)):
            for half in range(2):
                hs, carry[half] = _scan_block([piece(a, blk, i, half) for i in range(SUBLANES)],
                                              [piece(b, blk, i, half) for i in range(SUBLANES)],
                                              carry[half], reverse)
                for i in range(SUBLANES):
                    out[blk, i, half] = hs[i]
        rows = [jnp.concatenate([out[blk, i, 0], out[blk, i, 1]], axis=-1)
                for blk in range(n_blk) for i in range(SUBLANES)]
        return jnp.concatenate(rows, axis=0), tuple(carry)

    def start_state(d):
        return tuple(jnp.broadcast_to(h0_ref[d:d + 1, half * LANES:(half + 1) * LANES],
                                      (SUBLANES, LANES)) for half in range(2))

    def fwd_body(c, carry):
        s = pl.multiple_of(c * RG_ROWS, RG_ROWS)
        rows = []
        for blk in range(n_blk):
            base = s + blk * RG_BLOCK + HALO
            halves = []
            for half in range(2):
                lo = half * LANES
                x = {i: _strided8(xs, half, base + i) for i in range(-2, SUBLANES + 1)}
                halves.append([cw[0:1, lo:lo + LANES] * x[i - 2] + cw[1:2, lo:lo + LANES] * x[i - 1]
                               + cw[2:3, lo:lo + LANES] * x[i] + cw[3:4, lo:lo + LANES] * x[i + 1]
                               + cb[:, lo:lo + LANES] for i in range(SUBLANES)])
            rows += [jnp.concatenate([halves[0][i], halves[1][i]], axis=-1) for i in range(SUBLANES)]
        xc = jnp.concatenate(rows, axis=0)
        xc_s[pl.ds(s, RG_ROWS), :] = xc
        a, b = _rg_gates(xc, wg_ref[:, 0:2 * D_GROUP], bg_ref[:, 0:2 * D_GROUP], c_all[0:1])
        hf, carry = scan_chunk(a, b, carry, False)
        hf_s[pl.ds(s, RG_ROWS), :] = hf
        return carry

    h_f = lax.fori_loop(0, n_chunks, fwd_body, start_state(0))
    last_ref[0:1, :] = jnp.concatenate([h_f[0][0:1], h_f[1][0:1]], axis=-1)

    def bwd_body(ci, carry):
        c = n_chunks - 1 - ci
        s = pl.multiple_of(c * RG_ROWS, RG_ROWS)
        xc = xc_s[pl.ds(s, RG_ROWS), :]
        a, b = _rg_gates(xc, wg_ref[:, 2 * D_GROUP:], bg_ref[:, 2 * D_GROUP:], c_all[1:2])
        hb, carry = scan_chunk(a, b, carry, True)
        gr = jnp.concatenate(
            [jnp.concatenate([_strided8(gs, half, s + blk * RG_BLOCK + i) for half in range(2)], axis=-1)
             for blk in range(n_blk) for i in range(SUBLANES)], axis=0)
        k_gelu = math.sqrt(2.0 / math.pi)
        hg = 0.5 * gr
        gelu = hg + hg * jnp.tanh(gr * (k_gelu + (k_gelu * 0.044715) * (gr * gr)))
        y = (hf_s[pl.ds(s, RG_ROWS), :] + hb) * gelu
        for blk in range(n_blk):
            for i in range(SUBLANES):
                for half in range(2):
                    ys[half, pl.ds(s + blk * RG_BLOCK + i, SUBLANES, stride=SUBLANES), :] = (
                        piece(y, blk, i, half))
        y_ref[pl.ds(s, RG_ROWS), :] = jnp.concatenate(
            [ys[0, pl.ds(s, RG_ROWS), :], ys[1, pl.ds(s, RG_ROWS), :]], axis=-1).astype(BF16)
        return carry

    h_b = lax.fori_loop(0, n_chunks, bwd_body, start_state(1))
    last_ref[1:2, :] = jnp.concatenate([h_b[0][0:1], h_b[1][0:1]], axis=-1)


CP_ROWS = 256


def _window_sums(x, lo, hi, widths):
    out, prev, w = {}, {i: x[i] for i in range(lo, hi + 1)}, 1
    while w < max(widths):
        prev = {i: prev[i] + prev[i - w] for i in prev if i - w in prev}
        w *= 2
        out[w] = prev
    return out


def _convpool_kernel(gb_ref, gc_ref, xb_ref, xp_ref, cw_ref, pw_ref, ps_ref,
                     yb_ref, yd_ref, us, xs, cs, ds, *, t_len):
    n_chunks = t_len // CP_ROWS
    n_blk = CP_ROWS // RG_BLOCK
    zeros_halo = jnp.zeros((HALO, LANES), F32)
    for half in range(2):
        for pad in (us, xs):
            pad[half, 0:HALO, :] = zeros_halo
            pad[half, HALO + t_len:, :] = zeros_halo

    def fill(c, carry):
        s = pl.multiple_of(c * CP_ROWS, CP_ROWS)
        u = gc_ref[pl.ds(s, CP_ROWS), :].astype(F32) * xb_ref[pl.ds(s, CP_ROWS), :].astype(F32)
        x = xp_ref[pl.ds(s, CP_ROWS), :].astype(F32)
        for half in range(2):
            us[half, pl.ds(HALO + s, CP_ROWS), :] = u[:, half * LANES:(half + 1) * LANES]
            xs[half, pl.ds(HALO + s, CP_ROWS), :] = x[:, half * LANES:(half + 1) * LANES]
        return carry

    lax.fori_loop(0, n_chunks, fill, 0)

    cw = cw_ref[...]
    low_group = lax.broadcasted_iota(jnp.int32, (1, LANES), 1) < POOL_GW
    sub8 = lax.broadcasted_iota(jnp.int32, (SUBLANES, LANES), 0) * SUBLANES
    wins = [jnp.where(low_group, POOL_WINDOWS[2 * half], POOL_WINDOWS[2 * half + 1])
            for half in range(2)]

    def body(c, carry):
        s = pl.multiple_of(c * CP_ROWS, CP_ROWS)
        pm_rows = []
        for blk in range(n_blk):
            r0 = s + blk * RG_BLOCK
            halves = []
            for half in range(2):
                lo = half * LANES
                u = {i: _strided8(us, half, r0 + HALO + i) for i in range(-1, SUBLANES + 1)}
                for i in range(SUBLANES):
                    cs[half, pl.ds(r0 + i, SUBLANES, stride=SUBLANES), :] = (
                        cw[0:1, lo:lo + LANES] * u[i - 1] + cw[1:2, lo:lo + LANES] * u[i]
                        + cw[2:3, lo:lo + LANES] * u[i + 1])
                w_lo, w_hi = POOL_WINDOWS[2 * half], POOL_WINDOWS[2 * half + 1]
                p_lo, p_hi = -(w_hi // 2), SUBLANES - 1 + w_hi // 2 - 1
                x = {i: _strided8(xs, half, r0 + HALO + i) for i in range(p_lo, p_hi + 1)}
                sums = _window_sums(x, p_lo, p_hi, (w_lo, w_hi))
                win = wins[half]
                left = win // 2
                right = win - 1 - left
                pieces = []
                for i in range(SUBLANES):
                    tot = jnp.where(low_group, sums[w_lo][i + w_lo // 2 - 1], sums[w_hi][i + w_hi // 2 - 1])
                    t = sub8 + (r0 + i)
                    cnt = jnp.minimum(t + right, t_len - 1) - jnp.maximum(t - left, 0) + 1
                    pieces.append(tot / cnt.astype(F32) - x[i])
                halves.append(pieces)
            pm_rows += [jnp.concatenate([halves[0][i], halves[1][i]], axis=-1)
                        for i in range(SUBLANES)]
        pm = jnp.concatenate(pm_rows, axis=0)
        yd = jnp.dot(pm.astype(BF16), pw_ref[...], preferred_element_type=F32) * ps_ref[...]
        for blk in range(n_blk):
            for i in range(SUBLANES):
                k0 = blk * RG_BLOCK + i * SUBLANES
                for half in range(2):
                    ds[half, pl.ds(s + blk * RG_BLOCK + i, SUBLANES, stride=SUBLANES), :] = (
                        yd[k0:k0 + SUBLANES, half * LANES:(half + 1) * LANES])
        conv = jnp.concatenate([cs[0, pl.ds(s, CP_ROWS), :], cs[1, pl.ds(s, CP_ROWS), :]], axis=-1)
        yb_ref[pl.ds(s, CP_ROWS), :] = (gb_ref[pl.ds(s, CP_ROWS), :].astype(F32) * conv).astype(BF16)
        yd_ref[pl.ds(s, CP_ROWS), :] = jnp.concatenate(
            [ds[0, pl.ds(s, CP_ROWS), :], ds[1, pl.ds(s, CP_ROWS), :]], axis=-1).astype(BF16)
        return carry

    lax.fori_loop(0, n_chunks, body, 0)


def _seqmix_kernel(xr_ref, gr_ref, gb_ref, gc_ref, xb_ref, xp_ref, h0_ref,
                   ccw_ref, ccb_ref, wg_ref, bg_ref, lam_ref, bcw_ref, pw_ref, ps_ref,
                   yb_ref, yc_ref, yd_ref, last_ref,
                   xs, gs, ys, xc_s, hf_s, us, ps_s, ds, *, t_len):
    _rg_kernel(xr_ref, gr_ref, h0_ref, ccw_ref, ccb_ref, wg_ref, bg_ref, lam_ref,
               yc_ref, last_ref, xs, gs, ys, xc_s, hf_s, t_len=t_len)
    _convpool_kernel(gb_ref, gc_ref, xb_ref, xp_ref, bcw_ref, pw_ref, ps_ref,
                     yb_ref, yd_ref, us, ps_s, ys, ds, t_len=t_len)


def _seqmix(z, h0, layer, conv_c_w, conv_c_b, wg, bg, lam, conv_b_w, pw, ps):
    b, t, _ = z.shape
    col = lambda k: pl.BlockSpec((None, t, D_GROUP), lambda i: (i, 0, k))
    out = pl.BlockSpec((None, t, D_GROUP), lambda i: (i, 0, 0))
    slab = lambda rows: pltpu.VMEM((2, rows, LANES), F32)
    if h0.ndim == 4:
        h0_spec = pl.BlockSpec((None, None, 2, D_GROUP), lambda i: (i, layer, 0, 0))
    else:
        h0_spec = pl.BlockSpec((None, 2, D_GROUP), lambda i: (i, 0, 0))
    yb, yc, yd, last = pl.pallas_call(
        functools.partial(_seqmix_kernel, t_len=t),
        grid=(b,),
        in_specs=[col(6), col(7), col(3), col(4), col(5), col(8), h0_spec,
                  _resident((None, 4, D_GROUP), lambda i: (layer, 0, 0)),
                  _resident((None, 1, D_GROUP), lambda i: (layer, 0, 0)),
                  _resident((None, D_GROUP, 4 * D_GROUP), lambda i: (layer, 0, 0)),
                  _resident((None, 1, 4 * D_GROUP), lambda i: (layer, 0, 0)),
                  _resident((None, 2, D_GROUP), lambda i: (layer, 0, 0)),
                  _resident((None, 3, D_GROUP), lambda i: (layer, 0, 0)),
                  _resident((None, D_GROUP, D_GROUP), lambda i: (layer, 0, 0)),
                  _resident((None, 1, D_GROUP), lambda i: (layer, 0, 0))],
        out_specs=[out, out, out, pl.BlockSpec((None, 2, D_GROUP), lambda i: (i, 0, 0))],
        out_shape=[jax.ShapeDtypeStruct((b, t, D_GROUP), BF16)] * 3
        + [jax.ShapeDtypeStruct((b, 2, D_GROUP), F32)],
        scratch_shapes=[slab(t + 2 * HALO), slab(t), slab(t),
                        pltpu.VMEM((t, D_GROUP), F32), pltpu.VMEM((t, D_GROUP), F32),
                        slab(t + 2 * HALO), slab(t + 2 * HALO), slab(t)],
        compiler_params=_params("arbitrary"),
        name="seqmix",
    )(z, z, z, z, z, z, h0, conv_c_w, conv_c_b, wg, bg, lam, conv_b_w, pw, ps)
    return (yb, yc, yd), last


FF_CHUNK = 1024


def _mlp_kernel(x_ref, ya_ref, yb_ref, yc_ref, yd_ref, mod_ref, nw_ref, wo_ref, w1_ref, w2_ref,
                fw_ref, o_ref, *, final):
    x = x_ref[...]
    y = jnp.zeros(x.shape, F32)
    for i, r in enumerate((ya_ref, yb_ref, yc_ref, yd_ref)):
        y = y + jnp.dot(r[...], wo_ref[i * D_GROUP:(i + 1) * D_GROUP, :],
                        preferred_element_type=F32)
    gate1 = mod_ref[:, 2 * D_MODEL:3 * D_MODEL]
    shift2 = mod_ref[:, 3 * D_MODEL:4 * D_MODEL]
    scale2 = mod_ref[:, 4 * D_MODEL:5 * D_MODEL]
    gate2 = mod_ref[:, 5 * D_MODEL:6 * D_MODEL]
    x1 = x + gate1 * y
    hn = (_rms(x1, nw_ref[...]) * (1.0 + scale2) + shift2).astype(BF16)
    acc = jnp.zeros(x.shape, F32)
    for f in range(D_FF // FF_CHUNK):
        u = jnp.dot(hn, w1_ref[:, f * FF_CHUNK:(f + 1) * FF_CHUNK], preferred_element_type=F32)
        u = jnp.square(jnp.maximum(u, 0.0)).astype(BF16)
        acc = acc + jnp.dot(u, w2_ref[f * FF_CHUNK:(f + 1) * FF_CHUNK, :], preferred_element_type=F32)
    x2 = x1 + gate2 * acc
    if final:
        x2 = _rms(x2, fw_ref[...])
    o_ref[...] = x2


def _mlp(x, ys, mod, layer, mod_row, norm_w, wo, w1, w2, fw, final, tt, shared_mod=False):
    b, t, _ = x.shape
    if shared_mod:
        x, ys = x.reshape(1, b * t, D_MODEL), [y.reshape(1, b * t, D_GROUP) for y in ys]
    nb, nt, _ = x.shape
    ytile = pl.BlockSpec((None, tt, D_GROUP), lambda i, j: (i, j, 0))
    out = pl.pallas_call(
        functools.partial(_mlp_kernel, final=final),
        grid=(nb, nt // tt),
        in_specs=[
            pl.BlockSpec((None, tt, D_MODEL), lambda i, j: (i, j, 0)),
            ytile, ytile, ytile, ytile,
            pl.BlockSpec((None, None, 1, N_MOD * D_MODEL), lambda i, j: (layer, mod_row(i), 0, 0)),
            _resident((None, None, 1, D_MODEL), lambda i, j: (layer, 1, 0, 0)),
            _resident((None, D_MODEL, D_MODEL), lambda i, j: (layer, 0, 0)),
            _resident((None, D_MODEL, D_FF), lambda i, j: (layer, 0, 0)),
            _resident((None, D_FF, D_MODEL), lambda i, j: (layer, 0, 0)),
            _resident((1, D_MODEL), lambda i, j: (0, 0)),
        ],
        out_specs=pl.BlockSpec((None, tt, D_MODEL), lambda i, j: (i, j, 0)),
        out_shape=jax.ShapeDtypeStruct((nb, nt, D_MODEL), F32),
        compiler_params=_params("arbitrary", "arbitrary"),
        name="outproj_mlp",
    )(x, *ys, mod, norm_w, wo, w1, w2, fw)
    return out.reshape(b, t, D_MODEL)


def _block_diag(w):
    *lead, n, k, _ = w.shape
    eye = jnp.eye(n, dtype=w.dtype)
    return (eye[:, None, :, None] * w[..., :, :, None, :]).reshape(*lead, n * k, n * k)


def _rope_tables(t_len):
    half = DK // 4
    freqs = ROPE_BASE ** (-np.arange(half, dtype=np.float64) / half)
    lane = np.arange(LANES)
    m = lane % DK
    use_col = m >= DK // 2
    fidx = m % half
    t = np.arange(t_len)
    pos = np.where(use_col[None, :], (t % GRID_W)[:, None], (t // GRID_W)[:, None])
    ang = pos * freqs[fidx][None, :]
    sign = np.where((lane % (2 * half)) < half, -1.0, 1.0)
    return (jnp.asarray(np.cos(ang), dtype=F32), jnp.asarray(np.sin(ang) * sign[None, :], dtype=F32))


def kernel(x_prompt, x_sample, cache_k, cache_v, state_rglru, c, c_ctx, w_ada, b_ada, norm_w, w_in, diff_lambda, subln_w, conv_b_w, conv_c_w, conv_c_b, rg_w, rg_b, rg_lambda, pool_w, pool_scale, w_out, w_mlp1, w_mlp2, final_norm_w):
    n_lat = c.shape[0]
    ctx_row = n_lat
    cc = jnp.zeros((MOD_ROWS, D_MODEL), F32).at[:n_lat].set(c).at[ctx_row].set(c_ctx)
    mod = _ada(cc, w_ada, b_ada).reshape(DEPTH, MOD_ROWS, 1, N_MOD * D_MODEL)

    w_in_b = w_in.astype(BF16)
    w_out_b = w_out.astype(BF16)
    w1_b = w_mlp1.astype(BF16)
    w2_b = w_mlp2.astype(BF16)
    norm_w4 = norm_w.reshape(DEPTH, 2, 1, D_MODEL)
    rope_tabs = _rope_tables(x_sample.shape[1])
    gmat = _block_diag(jnp.full((N_ATT_HEADS, V_DIM, V_DIM), 1.0 / V_DIM, F32))
    subw = jnp.tile(subln_w, (1, N_ATT_HEADS)).reshape(DEPTH, 1, D_GROUP)
    wg = _block_diag(rg_w).transpose(0, 3, 1, 2, 4).reshape(DEPTH, D_GROUP, 4 * D_GROUP).astype(BF16)
    bg = rg_b.reshape(DEPTH, 1, 4 * D_GROUP)
    pw = _block_diag(pool_w).astype(BF16)
    ps = pool_scale.reshape(DEPTH, 1, D_GROUP)
    cb = conv_c_b.reshape(DEPTH, 1, D_GROUP)
    fw = final_norm_w.reshape(1, D_MODEL)
    zero_state = jnp.zeros((x_prompt.shape[0], 2, D_GROUP), F32)

    xp, xs = x_prompt, x_sample
    kv_out, hs_out = [], []
    for l in range(DEPTH):
        lam_init = 0.8 - 0.6 * math.exp(-0.3 * l)
        final = l == DEPTH - 1

        def mixers(z, ctx, h0):
            ya = _attn(z, ctx, l, diff_lambda, subw, gmat, lam_init)
            (yb, yc, yd), last = _seqmix(z, h0, l, conv_c_w, cb, wg, bg, rg_lambda,
                                         conv_b_w, pw, ps)
            return (ya, yb, yc, yd), last

        ctx_rows = lambda i: ctx_row
        kv_prev = kv_out if final else ()
        zp, k_new, v_new = _inproj(xp, mod, l, ctx_rows, norm_w4, w_in_b, None, INPROJ_TILE, kv_prev)
        ys, last_p = mixers(zp, None, zero_state)
        xp = _mlp(xp, ys, mod, l, ctx_rows, norm_w4, w_out_b, w1_b, w2_b, fw, final, TOKEN_TILE,
                  shared_mod=True)
        kv_out += [k_new, v_new]
        hs_out.append(last_p)

        lat_rows = lambda i: i
        (zs,) = _inproj(xs, mod, l, lat_rows, norm_w4, w_in_b, rope_tabs, INPROJ_TILE)
        ys, _ = mixers(zs, (cache_k, cache_v), state_rglru)
        xs = _mlp(xs, ys, mod, l, lat_rows, norm_w4, w_out_b, w1_b, w2_b, fw, final, TOKEN_TILE)

    return (xp, xs, kv_out[-2], kv_out[-1], jnp.stack(hs_out, axis=1))
```

```python
import functools
import math

import jax
import jax.numpy as jnp
import numpy as np
from jax import lax
from jax.experimental import pallas as pl
from jax.experimental.pallas import tpu as pltpu

D_MODEL = 1024
DEPTH = 2
GRID_W = 64
D_GROUP = 256
N_ATT_HEADS = 4
V_DIM = 64
DK = 32
ROPE_BASE = 10000.0
RG_BLOCKS = 4
RG_BW = 64
RG_C = 8.0
POOL_WINDOWS = (2, 4, 8, 16)
POOL_GW = 64
D_FF = 4 * D_MODEL
N_MOD = 6
D_IN = 9 * D_GROUP
EPS = 1e-6

LANES = 128
SUBLANES = 8
HALO = 8
VMEM_LIMIT = 56 * 1024 * 1024
MOD_ROWS = 16
TOKEN_TILE = 512
INPROJ_TILE = 1024
INPROJ_SPLIT = 4

BF16 = jnp.bfloat16
F32 = jnp.float32


def _params(*sem):
    return pltpu.CompilerParams(dimension_semantics=sem, vmem_limit_bytes=VMEM_LIMIT)


def _resident(shape, index_map):
    return pl.BlockSpec(shape, index_map, pipeline_mode=pl.Buffered(1))


def _rms(x, w):
    ms = jnp.mean(x * x, axis=-1, keepdims=True)
    return x * lax.rsqrt(ms + EPS) * w


def _ada_kernel(c_ref, w_ref, b_ref, o_ref):
    c = c_ref[...]
    s = (c * jax.nn.sigmoid(c)).astype(BF16)
    o_ref[...] = jnp.dot(s, w_ref[...].astype(BF16), preferred_element_type=F32) + b_ref[...]


def _ada(cc, w_ada, b_ada):
    tn = 1536
    n_out = N_MOD * D_MODEL
    return pl.pallas_call(
        _ada_kernel,
        grid=(DEPTH, n_out // tn),
        in_specs=[
            pl.BlockSpec((MOD_ROWS, D_MODEL), lambda l, n: (0, 0)),
            pl.BlockSpec((None, D_MODEL, tn), lambda l, n: (l, 0, n)),
            pl.BlockSpec((None, 1, tn), lambda l, n: (l, 0, n)),
        ],
        out_specs=pl.BlockSpec((None, MOD_ROWS, tn), lambda l, n: (l, 0, n)),
        out_shape=jax.ShapeDtypeStruct((DEPTH, MOD_ROWS, n_out), F32),
        compiler_params=_params("arbitrary", "arbitrary"),
        name="ada_mod",
    )(cc, w_ada, b_ada.reshape(DEPTH, 1, n_out))


def _inproj_kernel(*refs, rope, n_prev, seq_len):
    if rope:
        x_ref, mod_ref, nw_ref, w_ref, cos_ref, sin_ref, z_ref = refs
    else:
        x_ref, mod_ref, nw_ref, w_ref = refs[:4]
        prev_refs = refs[4:4 + 2 * n_prev]
        z_ref, ko_ref, vo_ref = refs[4 + 2 * n_prev:]
    shift = mod_ref[:, 0:D_MODEL]
    scale = mod_ref[:, D_MODEL:2 * D_MODEL]
    n_rows = x_ref.shape[0]
    sub = n_rows // INPROJ_SPLIT
    for r0 in range(0, n_rows, sub):
        x = x_ref[r0:r0 + sub, :]
        h = _rms(x, nw_ref[...]) * (1.0 + scale) + shift
        z = jnp.dot(h.astype(BF16), w_ref[...], preferred_element_type=F32)
        if not rope:
            z_ref[r0:r0 + sub, :] = z.astype(BF16)
            for q0 in range(0, sub, seq_len):
                sq = (r0 + q0) // seq_len
                if n_prev:
                    for lp in range(n_prev):
                        ko_ref[sq, lp] = prev_refs[2 * lp][sq]
                        vo_ref[sq, lp] = prev_refs[2 * lp + 1][sq]
                    ko_l, vo_l = ko_ref.at[sq, n_prev], vo_ref.at[sq, n_prev]
                else:
                    ko_l, vo_l = ko_ref.at[sq], vo_ref.at[sq]
                for hd in range(N_ATT_HEADS):
                    c0 = D_GROUP + hd * V_DIM
                    ko_l[hd] = z[q0:q0 + seq_len, c0:c0 + V_DIM]
                    vo_l[hd] = z[q0:q0 + seq_len, D_GROUP + c0:D_GROUP + c0 + V_DIM]
            continue
        cos = cos_ref[r0:r0 + sub, :]
        sin = sin_ref[r0:r0 + sub, :]
        first_half = (lax.broadcasted_iota(jnp.int32, (1, LANES), 1) % 16) < 8
        for j in range(2 * D_GROUP // LANES):
            zc = z[:, j * LANES:(j + 1) * LANES]
            partner = jnp.where(first_half,
                                pltpu.roll(zc, LANES - 8, 1),
                                pltpu.roll(zc, 8, 1))
            z_ref[r0:r0 + sub, j * LANES:(j + 1) * LANES] = (zc * cos + partner * sin).astype(BF16)
        z_ref[r0:r0 + sub, 2 * D_GROUP:] = z[:, 2 * D_GROUP:].astype(BF16)


def _inproj(x, mod, layer, mod_row, norm_w, w_in, rope_tabs, tt, kv_prev=()):
    b, t, _ = x.shape
    rope = rope_tabs is not None
    n_prev = len(kv_prev) // 2
    if rope:
        grid = (b, t // tt)
        tok = lambda i, j: (i, j, 0)
        x3 = x
    else:
        grid = (1, b * t // tt)
        tok = lambda i, j: (0, j, 0)
        x3 = x.reshape(1, b * t, D_MODEL)
    in_specs = [
        pl.BlockSpec((None, tt, D_MODEL), tok),
        pl.BlockSpec((None, None, 1, N_MOD * D_MODEL), lambda i, j: (layer, mod_row(i), 0, 0)),
        _resident((None, None, 1, D_MODEL), lambda i, j: (layer, 0, 0, 0)),
        _resident((None, D_MODEL, D_IN), lambda i, j: (layer, 0, 0)),
    ]
    args = [x3, mod, norm_w, w_in]
    out_specs = [pl.BlockSpec((None, tt, D_IN), tok)]
    out_shape = [jax.ShapeDtypeStruct(x3.shape[:2] + (D_IN,), BF16)]
    if rope:
        in_specs += [pl.BlockSpec((tt, LANES), lambda i, j: (j, 0))] * 2
        args += list(rope_tabs)
    else:
        n_seq = tt // t
        per_layer = pl.BlockSpec((n_seq, N_ATT_HEADS, t, V_DIM), lambda i, j: (j, 0, 0, 0))
        in_specs += [per_layer] * (2 * n_prev)
        args += list(kv_prev)
        if n_prev:
            out_specs += [pl.BlockSpec((n_seq, n_prev + 1, N_ATT_HEADS, t, V_DIM),
                                       lambda i, j: (j, 0, 0, 0, 0))] * 2
            out_shape += [jax.ShapeDtypeStruct((b, n_prev + 1, N_ATT_HEADS, t, V_DIM), F32)] * 2
        else:
            out_specs += [per_layer] * 2
            out_shape += [jax.ShapeDtypeStruct((b, N_ATT_HEADS, t, V_DIM), F32)] * 2
    outs = pl.pallas_call(
        functools.partial(_inproj_kernel, rope=rope, n_prev=n_prev, seq_len=t),
        grid=grid,
        in_specs=in_specs,
        out_specs=out_specs,
        out_shape=out_shape,
        compiler_params=_params("arbitrary", "arbitrary"),
        name="inproj",
    )(*args)
    return [outs[0].reshape(b, t, D_IN)] + list(outs[1:])


ATT_ROWS = 128
LOG2E = math.log2(math.e)


def _attn_kernel(*refs, has_ctx, lam_init, t_len, n_steps):
    if has_ctx:
        (q_ref, qn_ref, k_ref, v_ref, ck_ref, cv_ref, dl_ref, sw_ref, g_ref, o_ref,
         kt, vs, s0, s1, pb) = refs
    else:
        q_ref, qn_ref, k_ref, v_ref, dl_ref, sw_ref, g_ref, o_ref, kt, vs, s0, s1, pb = refs
    rows = ATT_ROWS
    past = ck_ref.shape[1] if has_ctx else 0
    s_len = past + t_len
    lane = lax.broadcasted_iota(jnp.int32, (1, D_GROUP), 1)

    def scores(q_rows, dst):
        q = q_rows.astype(F32) * (DK ** -0.5 * LOG2E)
        stack = jnp.concatenate(
            [jnp.where((lane // DK) == hm, q, 0.0).astype(BF16) for hm in range(2 * N_ATT_HEADS)],
            axis=0)
        dst[...] = jnp.dot(stack, kt[...], preferred_element_type=F32)

    @pl.when(pl.program_id(1) == 0)
    def _():
        if has_ctx:
            heads = range(N_ATT_HEADS)
            ck = jnp.concatenate([ck_ref[h] for h in heads], axis=-1)
            cv = jnp.concatenate([cv_ref[h] for h in heads], axis=-1)
            kt[:, 0:past] = ck.T.astype(BF16)
            vs[0:past, :] = cv.astype(BF16)
        kt[:, past:s_len] = k_ref[...].astype(F32).T.astype(BF16)
        vs[past:s_len, :] = v_ref[...]
        scores(q_ref[0:rows, :], s0)

    dl = dl_ref[...]
    lam = (jnp.exp(jnp.sum(dl[0:1] * dl[1:2], axis=-1, keepdims=True))
           - jnp.exp(jnp.sum(dl[2:3] * dl[3:4], axis=-1, keepdims=True)) + lam_init)

    def finish(src, out_lo):
        inv_l1 = jnp.zeros((rows, D_GROUP), F32)
        for h in range(N_ATT_HEADS):
            es, ls = [], []
            for m in range(2):
                lo = (2 * h + m) * rows
                s = src[lo:lo + rows, :]
                e = jnp.exp2(s - jnp.max(s, axis=-1, keepdims=True))
                es.append(e)
                ls.append(jnp.sum(e, axis=-1, keepdims=True))
            beta = lam * ls[0] / ls[1]
            pb[h * rows:(h + 1) * rows, :] = (es[0] - beta * es[1]).astype(BF16)
            inv_l1 = jnp.where((lane // V_DIM) == h, 1.0 / ls[0], inv_l1)
        full = jnp.dot(pb[...], vs[...], preferred_element_type=F32)
        acc = full[0:rows]
        for h in range(1, N_ATT_HEADS):
            acc = jnp.where((lane // V_DIM) == h, full[h * rows:(h + 1) * rows], acc)
        acc = acc * inv_l1
        ms = jnp.dot(acc * acc, g_ref[...], preferred_element_type=F32,
                     precision=lax.Precision.HIGHEST)
        o_ref[out_lo:out_lo + rows, :] = (
            acc * lax.rsqrt(ms + EPS) * sw_ref[...] * (1.0 - lam_init)).astype(BF16)

    scores(q_ref[rows:2 * rows, :], s1)
    finish(s0, 0)
    if n_steps > 1:
        scores(qn_ref[0:rows, :], s0)
    finish(s1, rows)


def _attn(z, ctx, layer, dl, subw, gmat, lam_init):
    b, t, _ = z.shape
    has_ctx = ctx is not None
    past = ctx[0].shape[3] if has_ctx else 0
    s_len = past + t
    tq = 2 * ATT_ROWS
    n_steps = t // tq
    in_specs = [
        pl.BlockSpec((None, tq, D_GROUP), lambda i, j: (i, j, 0)),
        pl.BlockSpec((None, tq, D_GROUP), lambda i, j: (i, jnp.minimum(j + 1, n_steps - 1), 0)),
        pl.BlockSpec((None, t, D_GROUP), lambda i, j: (i, 0, 1)),
        pl.BlockSpec((None, t, D_GROUP), lambda i, j: (i, 0, 2)),
    ]
    args = [z, z, z, z]
    if has_ctx:
        in_specs += [pl.BlockSpec((None, None, N_ATT_HEADS, past, V_DIM),
                                  lambda i, j: (i, layer, 0, 0, 0))] * 2
        args += list(ctx)
    in_specs += [
        _resident((None, 4, DK), lambda i, j: (layer, 0, 0)),
        _resident((None, 1, D_GROUP), lambda i, j: (layer, 0, 0)),
        _resident((D_GROUP, D_GROUP), lambda i, j: (0, 0)),
    ]
    args += [dl, subw, gmat]
    n_hm = 2 * N_ATT_HEADS
    return pl.pallas_call(
        functools.partial(_attn_kernel, has_ctx=has_ctx, lam_init=lam_init, t_len=t,
                          n_steps=n_steps),
        grid=(b, n_steps),
        in_specs=in_specs,
        out_specs=pl.BlockSpec((None, tq, D_GROUP), lambda i, j: (i, j, 0)),
        out_shape=jax.ShapeDtypeStruct((b, t, D_GROUP), BF16),
        scratch_shapes=[pltpu.VMEM((D_GROUP, s_len), BF16),
                        pltpu.VMEM((s_len, D_GROUP), BF16),
                        pltpu.VMEM((n_hm * ATT_ROWS, s_len), F32),
                        pltpu.VMEM((n_hm * ATT_ROWS, s_len), F32),
                        pltpu.VMEM((N_ATT_HEADS * ATT_ROWS, s_len), BF16)],
        compiler_params=_params("arbitrary", "arbitrary"),
        name="diff_attn",
    )(*args)


RG_ROWS = 512
RG_BLOCK = 64


def _strided8(ref, half, start):
    return ref[half, pl.ds(start, SUBLANES, stride=SUBLANES), :]


def _rg_gates(xc, wg, bg, c_dir):
    g = jnp.dot(xc.astype(BF16), wg, preferred_element_type=F32) + bg
    r = jax.nn.sigmoid(g[:, :D_GROUP])
    i = jax.nn.sigmoid(g[:, D_GROUP:])
    a = jnp.exp2(r * c_dir)
    om = 1.0 - a * a
    b = jnp.where(om > 0.0, om * lax.rsqrt(om), 0.0) * i * xc
    return a, b


def _scan_block(a, b, hp, reverse):
    n = SUBLANES
    order = list(range(n - 1, -1, -1)) if reverse else list(range(n))
    hs, cum = [None] * n, [None] * n
    prev = None
    for i in order:
        if prev is None:
            hs[i], cum[i] = b[i], a[i]
        else:
            hs[i], cum[i] = a[i] * hs[prev] + b[i], a[i] * cum[prev]
        prev = i
    p, f = cum[prev], hs[prev]
    row = lax.broadcasted_iota(jnp.int32, (n, LANES), 0)
    for d in (1, 2, 4):
        m = (row < n - d) if reverse else (row >= d)
        sh = n - d if reverse else d
        f = jnp.where(m, p * pltpu.roll(f, sh, 0) + f, f)
        p = jnp.where(m, p * pltpu.roll(p, sh, 0), p)
    first, last = (n - 1, 0) if reverse else (0, n - 1)
    sh = n - 1 if reverse else 1
    cin = jnp.where(row == first, hp, pltpu.roll(f, sh, 0) + pltpu.roll(p, sh, 0) * hp)
    out = [hs[i] + cum[i] * cin for i in range(n)]
    bc = lambda v: jnp.broadcast_to(v[last:last + 1, :], (n, LANES))
    return out, bc(f) + bc(p) * hp


def _rg_kernel(xr_ref, gr_ref, h0_ref, cw_ref, cb_ref, wg_ref, bg_ref, lam_ref,
               y_ref, last_ref, xs, gs, ys, xc_s, hf_s, *, t_len):
    rg_rows = min(RG_ROWS, t_len)
    n_chunks = t_len // rg_rows
    n_blk = rg_rows // RG_BLOCK
    zeros_halo = jnp.zeros((HALO, LANES), F32)
    for half in range(2):
        lo = half * LANES
        xs[half, 0:HALO, :] = zeros_halo
        xs[half, HALO + t_len:, :] = zeros_halo
        xs[half, HALO:HALO + t_len, :] = xr_ref[:, lo:lo + LANES].astype(F32)
        gs[half, :, :] = gr_ref[:, lo:lo + LANES].astype(F32)

    neg = -lam_ref[...]
    softplus = jnp.maximum(neg, 0.0) + jnp.log1p(jnp.exp(-jnp.abs(neg)))
    c_all = (-RG_C * LOG2E) * softplus
    cw = cw_ref[...]
    cb = cb_ref[...]

    def piece(v, blk, i, half):
        r0 = blk * RG_BLOCK + i * SUBLANES
        return v[r0:r0 + SUBLANES, half * LANES:(half + 1) * LANES]

    def scan_chunk(a, b, carry, reverse):
        carry = list(carry)
        out = {}
        for blk in (reversed(range(n_blk)) if reverse else range(n_blk)):
            for half in range(2):
                hs, carry[half] = _scan_block([piece(a, blk, i, half) for i in range(SUBLANES)],
                                              [piece(b, blk, i, half) for i in range(SUBLANES)],
                                              carry[half], reverse)
                for i in range(SUBLANES):
                    out[blk, i, half] = hs[i]
        rows = [jnp.concatenate([out[blk, i, 0], out[blk, i, 1]], axis=-1)
                for blk in range(n_blk) for i in range(SUBLANES)]
        return jnp.concatenate(rows, axis=0), tuple(carry)

    def start_state(d):
        return tuple(jnp.broadcast_to(h0_ref[d:d + 1, half * LANES:(half + 1) * LANES],
                                      (SUBLANES, LANES)) for half in range(2))

    def fwd_body(c, carry):
        s = pl.multiple_of(c * rg_rows, rg_rows)
        rows = []
        for blk in range(n_blk):
            base = s + blk * RG_BLOCK + HALO
            halves = []
            for half in range(2):
                lo = half * LANES
                x = {i: _strided8(xs, half, base + i) for i in range(-2, SUBLANES + 1)}
                halves.append([cw[0:1, lo:lo + LANES] * x[i - 2] + cw[1:2, lo:lo + LANES] * x[i - 1]
                               + cw[2:3, lo:lo + LANES] * x[i] + cw[3:4, lo:lo + LANES] * x[i + 1]
                               + cb[:, lo:lo + LANES] for i in range(SUBLANES)])
            rows += [jnp.concatenate([halves[0][i], halves[1][i]], axis=-1) for i in range(SUBLANES)]
        xc = jnp.concatenate(rows, axis=0)
        xc_s[pl.ds(s, rg_rows), :] = xc
        a, b = _rg_gates(xc, wg_ref[:, 0:2 * D_GROUP], bg_ref[:, 0:2 * D_GROUP], c_all[0:1])
        hf, carry = scan_chunk(a, b, carry, False)
        hf_s[pl.ds(s, rg_rows), :] = hf
        return carry

    h_f = lax.fori_loop(0, n_chunks, fwd_body, start_state(0))
    last_ref[0:1, :] = jnp.concatenate([h_f[0][0:1], h_f[1][0:1]], axis=-1)

    def bwd_body(ci, carry):
        c = n_chunks - 1 - ci
        s = pl.multiple_of(c * rg_rows, rg_rows)
        xc = xc_s[pl.ds(s, rg_rows), :]
        a, b = _rg_gates(xc, wg_ref[:, 2 * D_GROUP:], bg_ref[:, 2 * D_GROUP:], c_all[1:2])
        hb, carry = scan_chunk(a, b, carry, True)
        gr = jnp.concatenate(
            [jnp.concatenate([_strided8(gs, half, s + blk * RG_BLOCK + i) for half in range(2)], axis=-1)
             for blk in range(n_blk) for i in range(SUBLANES)], axis=0)
        k_gelu = math.sqrt(2.0 / math.pi)
        hg = 0.5 * gr
        gelu = hg + hg * jnp.tanh(gr * (k_gelu + (k_gelu * 0.044715) * (gr * gr)))
        y = (hf_s[pl.ds(s, rg_rows), :] + hb) * gelu
        for blk in range(n_blk):
            for i in range(SUBLANES):
                for half in range(2):
                    ys[half, pl.ds(s + blk * RG_BLOCK + i, SUBLANES, stride=SUBLANES), :] = (
                        piece(y, blk, i, half))
        y_ref[pl.ds(s, rg_rows), :] = jnp.concatenate(
            [ys[0, pl.ds(s, rg_rows), :], ys[1, pl.ds(s, rg_rows), :]], axis=-1).astype(BF16)
        return carry

    h_b = lax.fori_loop(0, n_chunks, bwd_body, start_state(1))
    last_ref[1:2, :] = jnp.concatenate([h_b[0][0:1], h_b[1][0:1]], axis=-1)


CP_ROWS = 512


def _window_sums(x, lo, hi, widths):
    out, prev, w = {}, {i: x[i] for i in range(lo, hi + 1)}, 1
    while w < max(widths):
        prev = {i: prev[i] + prev[i - w] for i in prev if i - w in prev}
        w *= 2
        out[w] = prev
    return out


def _convpool_kernel(gb_ref, gc_ref, xb_ref, xp_ref, cw_ref, pw_ref, ps_ref,
                     yb_ref, yd_ref, us, xs, cs, ds, *, t_len):
    cp_rows = min(CP_ROWS, t_len)
    n_chunks = t_len // cp_rows
    n_blk = cp_rows // RG_BLOCK
    zeros_halo = jnp.zeros((HALO, LANES), F32)
    for half in range(2):
        for pad in (us, xs):
            pad[half, 0:HALO, :] = zeros_halo
            pad[half, HALO + t_len:, :] = zeros_halo

    def fill(c, carry):
        s = pl.multiple_of(c * cp_rows, cp_rows)
        u = gc_ref[pl.ds(s, cp_rows), :].astype(F32) * xb_ref[pl.ds(s, cp_rows), :].astype(F32)
        x = xp_ref[pl.ds(s, cp_rows), :].astype(F32)
        for half in range(2):
            us[half, pl.ds(HALO + s, cp_rows), :] = u[:, half * LANES:(half + 1) * LANES]
            xs[half, pl.ds(HALO + s, cp_rows), :] = x[:, half * LANES:(half + 1) * LANES]
        return carry

    lax.fori_loop(0, n_chunks, fill, 0)

    cw = cw_ref[...]
    low_group = lax.broadcasted_iota(jnp.int32, (1, LANES), 1) < POOL_GW
    sub8 = lax.broadcasted_iota(jnp.int32, (SUBLANES, LANES), 0) * SUBLANES
    wins = [jnp.where(low_group, POOL_WINDOWS[2 * half], POOL_WINDOWS[2 * half + 1])
            for half in range(2)]

    def body(c, carry):
        s = pl.multiple_of(c * cp_rows, cp_rows)
        pm_rows = []
        for blk in range(n_blk):
            r0 = s + blk * RG_BLOCK
            halves = []
            for half in range(2):
                lo = half * LANES
                u = {i: _strided8(us, half, r0 + HALO + i) for i in range(-1, SUBLANES + 1)}
                for i in range(SUBLANES):
                    cs[half, pl.ds(r0 + i, SUBLANES, stride=SUBLANES), :] = (
                        cw[0:1, lo:lo + LANES] * u[i - 1] + cw[1:2, lo:lo + LANES] * u[i]
                        + cw[2:3, lo:lo + LANES] * u[i + 1])
                w_lo, w_hi = POOL_WINDOWS[2 * half], POOL_WINDOWS[2 * half + 1]
                p_lo, p_hi = -(w_hi // 2), SUBLANES - 1 + w_hi // 2 - 1
                x = {i: _strided8(xs, half, r0 + HALO + i) for i in range(p_lo, p_hi + 1)}
                sums = _window_sums(x, p_lo, p_hi, (w_lo, w_hi))
                win = wins[half]
                left = win // 2
                right = win - 1 - left
                pieces = []
                for i in range(SUBLANES):
                    tot = jnp.where(low_group, sums[w_lo][i + w_lo // 2 - 1], sums[w_hi][i + w_hi // 2 - 1])
                    t = sub8 + (r0 + i)
                    cnt = jnp.minimum(t + right, t_len - 1) - jnp.maximum(t - left, 0) + 1
                    pieces.append(tot / cnt.astype(F32) - x[i])
                halves.append(pieces)
            pm_rows += [jnp.concatenate([halves[0][i], halves[1][i]], axis=-1)
                        for i in range(SUBLANES)]
        pm = jnp.concatenate(pm_rows, axis=0)
        yd = jnp.dot(pm.astype(BF16), pw_ref[...], preferred_element_type=F32) * ps_ref[...]
        for blk in range(n_blk):
            for i in range(SUBLANES):
                k0 = blk * RG_BLOCK + i * SUBLANES
                for half in range(2):
                    ds[half, pl.ds(s + blk * RG_BLOCK + i, SUBLANES, stride=SUBLANES), :] = (
                        yd[k0:k0 + SUBLANES, half * LANES:(half + 1) * LANES])
        conv = jnp.concatenate([cs[0, pl.ds(s, cp_rows), :], cs[1, pl.ds(s, cp_rows), :]], axis=-1)
        yb_ref[pl.ds(s, cp_rows), :] = (gb_ref[pl.ds(s, cp_rows), :].astype(F32) * conv).astype(BF16)
        yd_ref[pl.ds(s, cp_rows), :] = jnp.concatenate(
            [ds[0, pl.ds(s, cp_rows), :], ds[1, pl.ds(s, cp_rows), :]], axis=-1).astype(BF16)
        return carry

    lax.fori_loop(0, n_chunks, body, 0)


def _seqmix_kernel(xr_ref, gr_ref, gb_ref, gc_ref, xb_ref, xp_ref, h0_ref,
                   ccw_ref, ccb_ref, wg_ref, bg_ref, lam_ref, bcw_ref, pw_ref, ps_ref,
                   yb_ref, yc_ref, yd_ref, last_ref,
                   xs, gs, ys, xc_s, hf_s, us, ps_s, ds, *, t_len):
    _rg_kernel(xr_ref, gr_ref, h0_ref, ccw_ref, ccb_ref, wg_ref, bg_ref, lam_ref,
               yc_ref, last_ref, xs, gs, ys, xc_s, hf_s, t_len=t_len)
    _convpool_kernel(gb_ref, gc_ref, xb_ref, xp_ref, bcw_ref, pw_ref, ps_ref,
                     yb_ref, yd_ref, us, ps_s, ys, ds, t_len=t_len)


def _seqmix(z, h0, layer, conv_c_w, conv_c_b, wg, bg, lam, conv_b_w, pw, ps):
    b, t, _ = z.shape
    col = lambda k: pl.BlockSpec((None, t, D_GROUP), lambda i: (i, 0, k))
    out = pl.BlockSpec((None, t, D_GROUP), lambda i: (i, 0, 0))
    slab = lambda rows: pltpu.VMEM((2, rows, LANES), F32)
    if h0.ndim == 4:
        h0_spec = pl.BlockSpec((None, None, 2, D_GROUP), lambda i: (i, layer, 0, 0))
    else:
        h0_spec = pl.BlockSpec((None, 2, D_GROUP), lambda i: (i, 0, 0))
    yb, yc, yd, last = pl.pallas_call(
        functools.partial(_seqmix_kernel, t_len=t),
        grid=(b,),
        in_specs=[col(6), col(7), col(3), col(4), col(5), col(8), h0_spec,
                  _resident((None, 4, D_GROUP), lambda i: (layer, 0, 0)),
                  _resident((None, 1, D_GROUP), lambda i: (layer, 0, 0)),
                  _resident((None, D_GROUP, 4 * D_GROUP), lambda i: (layer, 0, 0)),
                  _resident((None, 1, 4 * D_GROUP), lambda i: (layer, 0, 0)),
                  _resident((None, 2, D_GROUP), lambda i: (layer, 0, 0)),
                  _resident((None, 3, D_GROUP), lambda i: (layer, 0, 0)),
                  _resident((None, D_GROUP, D_GROUP), lambda i: (layer, 0, 0)),
                  _resident((None, 1, D_GROUP), lambda i: (layer, 0, 0))],
        out_specs=[out, out, out, pl.BlockSpec((None, 2, D_GROUP), lambda i: (i, 0, 0))],
        out_shape=[jax.ShapeDtypeStruct((b, t, D_GROUP), BF16)] * 3
        + [jax.ShapeDtypeStruct((b, 2, D_GROUP), F32)],
        scratch_shapes=[slab(t + 2 * HALO), slab(t), slab(t),
                        pltpu.VMEM((t, D_GROUP), F32), pltpu.VMEM((t, D_GROUP), F32),
                        slab(t + 2 * HALO), slab(t + 2 * HALO), slab(t)],
        compiler_params=_params("arbitrary"),
        name="seqmix",
    )(z, z, z, z, z, z, h0, conv_c_w, conv_c_b, wg, bg, lam, conv_b_w, pw, ps)
    return (yb, yc, yd), last


FF_CHUNK = 1024


def _mlp_kernel(x_ref, ya_ref, yb_ref, yc_ref, yd_ref, mod_ref, nw_ref, wo_ref, w1_ref, w2_ref,
                fw_ref, o_ref, *, final):
    x = x_ref[...]
    y = jnp.zeros(x.shape, F32)
    for i, r in enumerate((ya_ref, yb_ref, yc_ref, yd_ref)):
        y = y + jnp.dot(r[...], wo_ref[i * D_GROUP:(i + 1) * D_GROUP, :],
                        preferred_element_type=F32)
    gate1 = mod_ref[:, 2 * D_MODEL:3 * D_MODEL]
    shift2 = mod_ref[:, 3 * D_MODEL:4 * D_MODEL]
    scale2 = mod_ref[:, 4 * D_MODEL:5 * D_MODEL]
    gate2 = mod_ref[:, 5 * D_MODEL:6 * D_MODEL]
    x1 = x + gate1 * y
    hn = (_rms(x1, nw_ref[...]) * (1.0 + scale2) + shift2).astype(BF16)
    acc = jnp.zeros(x.shape, F32)
    for f in range(D_FF // FF_CHUNK):
        u = jnp.dot(hn, w1_ref[:, f * FF_CHUNK:(f + 1) * FF_CHUNK], preferred_element_type=F32)
        u = jnp.square(jnp.maximum(u, 0.0)).astype(BF16)
        acc = acc + jnp.dot(u, w2_ref[f * FF_CHUNK:(f + 1) * FF_CHUNK, :], preferred_element_type=F32)
    x2 = x1 + gate2 * acc
    if final:
        x2 = _rms(x2, fw_ref[...])
    o_ref[...] = x2


def _mlp(x, ys, mod, layer, mod_row, norm_w, wo, w1, w2, fw, final, tt, shared_mod=False):
    b, t, _ = x.shape
    if shared_mod:
        x, ys = x.reshape(1, b * t, D_MODEL), [y.reshape(1, b * t, D_GROUP) for y in ys]
    nb, nt, _ = x.shape
    ytile = pl.BlockSpec((None, tt, D_GROUP), lambda i, j: (i, j, 0))
    out = pl.pallas_call(
        functools.partial(_mlp_kernel, final=final),
        grid=(nb, nt // tt),
        in_specs=[
            pl.BlockSpec((None, tt, D_MODEL), lambda i, j: (i, j, 0)),
            ytile, ytile, ytile, ytile,
            pl.BlockSpec((None, None, 1, N_MOD * D_MODEL), lambda i, j: (layer, mod_row(i), 0, 0)),
            _resident((None, None, 1, D_MODEL), lambda i, j: (layer, 1, 0, 0)),
            _resident((None, D_MODEL, D_MODEL), lambda i, j: (layer, 0, 0)),
            _resident((None, D_MODEL, D_FF), lambda i, j: (layer, 0, 0)),
            _resident((None, D_FF, D_MODEL), lambda i, j: (layer, 0, 0)),
            _resident((1, D_MODEL), lambda i, j: (0, 0)),
        ],
        out_specs=pl.BlockSpec((None, tt, D_MODEL), lambda i, j: (i, j, 0)),
        out_shape=jax.ShapeDtypeStruct((nb, nt, D_MODEL), F32),
        compiler_params=_params("arbitrary", "arbitrary"),
        name="outproj_mlp",
    )(x, *ys, mod, norm_w, wo, w1, w2, fw)
    return out.reshape(b, t, D_MODEL)


def _block_diag(w):
    *lead, n, k, _ = w.shape
    eye = jnp.eye(n, dtype=w.dtype)
    return (eye[:, None, :, None] * w[..., :, :, None, :]).reshape(*lead, n * k, n * k)


def _rope_tables(t_len):
    half = DK // 4
    freqs = ROPE_BASE ** (-np.arange(half, dtype=np.float64) / half)
    lane = np.arange(LANES)
    m = lane % DK
    use_col = m >= DK // 2
    fidx = m % half
    t = np.arange(t_len)
    pos = np.where(use_col[None, :], (t % GRID_W)[:, None], (t // GRID_W)[:, None])
    ang = pos * freqs[fidx][None, :]
    sign = np.where((lane % (2 * half)) < half, -1.0, 1.0)
    return (jnp.asarray(np.cos(ang), dtype=F32), jnp.asarray(np.sin(ang) * sign[None, :], dtype=F32))


def kernel(x_prompt, x_sample, cache_k, cache_v, state_rglru, c, c_ctx, w_ada, b_ada, norm_w, w_in, diff_lambda, subln_w, conv_b_w, conv_c_w, conv_c_b, rg_w, rg_b, rg_lambda, pool_w, pool_scale, w_out, w_mlp1, w_mlp2, final_norm_w):
    n_lat = c.shape[0]
    ctx_row = n_lat
    cc = jnp.zeros((MOD_ROWS, D_MODEL), F32).at[:n_lat].set(c).at[ctx_row].set(c_ctx)
    mod = _ada(cc, w_ada, b_ada).reshape(DEPTH, MOD_ROWS, 1, N_MOD * D_MODEL)

    w_in_b = w_in.astype(BF16)
    w_out_b = w_out.astype(BF16)
    w1_b = w_mlp1.astype(BF16)
    w2_b = w_mlp2.astype(BF16)
    norm_w4 = norm_w.reshape(DEPTH, 2, 1, D_MODEL)
    rope_tabs = _rope_tables(x_sample.shape[1])
    gmat = _block_diag(jnp.full((N_ATT_HEADS, V_DIM, V_DIM), 1.0 / V_DIM, F32))
    subw = jnp.tile(subln_w, (1, N_ATT_HEADS)).reshape(DEPTH, 1, D_GROUP)
    wg = _block_diag(rg_w).transpose(0, 3, 1, 2, 4).reshape(DEPTH, D_GROUP, 4 * D_GROUP).astype(BF16)
    bg = rg_b.reshape(DEPTH, 1, 4 * D_GROUP)
    pw = _block_diag(pool_w).astype(BF16)
    ps = pool_scale.reshape(DEPTH, 1, D_GROUP)
    cb = conv_c_b.reshape(DEPTH, 1, D_GROUP)
    fw = final_norm_w.reshape(1, D_MODEL)
    zero_state = jnp.zeros((x_prompt.shape[0], 2, D_GROUP), F32)

    xp, xs = x_prompt, x_sample
    kv_out, hs_out = [], []
    for l in range(DEPTH):
        lam_init = 0.8 - 0.6 * math.exp(-0.3 * l)
        final = l == DEPTH - 1

        def mixers(z, ctx, h0):
            ya = _attn(z, ctx, l, diff_lambda, subw, gmat, lam_init)
            (yb, yc, yd), last = _seqmix(z, h0, l, conv_c_w, cb, wg, bg, rg_lambda,
                                         conv_b_w, pw, ps)
            return (ya, yb, yc, yd), last

        ctx_rows = lambda i: ctx_row
        kv_prev = kv_out if final else ()
        zp, k_new, v_new = _inproj(xp, mod, l, ctx_rows, norm_w4, w_in_b, None, INPROJ_TILE, kv_prev)
        ys, last_p = mixers(zp, None, zero_state)
        xp = _mlp(xp, ys, mod, l, ctx_rows, norm_w4, w_out_b, w1_b, w2_b, fw, final, TOKEN_TILE,
                  shared_mod=True)
        kv_out += [k_new, v_new]
        hs_out.append(last_p)

        lat_rows = lambda i: i
        (zs,) = _inproj(xs, mod, l, lat_rows, norm_w4, w_in_b, rope_tabs, INPROJ_TILE)
        ys, _ = mixers(zs, (cache_k, cache_v), state_rglru)
        xs = _mlp(xs, ys, mod, l, lat_rows, norm_w4, w_out_b, w1_b, w2_b, fw, final, TOKEN_TILE)

    return (xp, xs, kv_out[-2], kv_out[-1], jnp.stack(hs_out, axis=1))
```

```python
import functools
import math

import jax
import jax.numpy as jnp
import numpy as np
from jax import lax
from jax.experimental import pallas as pl
from jax.experimental.pallas import tpu as pltpu

D_MODEL = 1024
DEPTH = 2
GRID_W = 64
D_GROUP = 256
N_ATT_HEADS = 4
V_DIM = 64
DK = 32
ROPE_BASE = 10000.0
RG_BLOCKS = 4
RG_BW = 64
RG_C = 8.0
POOL_WINDOWS = (2, 4, 8, 16)
POOL_GW = 64
D_FF = 4 * D_MODEL
N_MOD = 6
D_IN = 9 * D_GROUP
EPS = 1e-6

LANES = 128
SUBLANES = 8
HALO = 8
VMEM_LIMIT = 56 * 1024 * 1024
MOD_ROWS = 16
TOKEN_TILE = 512
INPROJ_TILE = 1024
INPROJ_SPLIT = 4

BF16 = jnp.bfloat16
F32 = jnp.float32


def _params(*sem):
    return pltpu.CompilerParams(dimension_semantics=sem, vmem_limit_bytes=VMEM_LIMIT)


def _resident(shape, index_map):
    return pl.BlockSpec(shape, index_map, pipeline_mode=pl.Buffered(1))


def _rms(x, w):
    ms = jnp.mean(x * x, axis=-1, keepdims=True)
    return x * lax.rsqrt(ms + EPS) * w


def _ada_kernel(c_ref, w_ref, b_ref, o_ref):
    c = c_ref[...]
    s = (c * jax.nn.sigmoid(c)).astype(BF16)
    o_ref[...] = jnp.dot(s, w_ref[...].astype(BF16), preferred_element_type=F32) + b_ref[...]


def _ada(cc, w_ada, b_ada):
    tn = 1536
    n_out = N_MOD * D_MODEL
    return pl.pallas_call(
        _ada_kernel,
        grid=(DEPTH, n_out // tn),
        in_specs=[
            pl.BlockSpec((MOD_ROWS, D_MODEL), lambda l, n: (0, 0)),
            pl.BlockSpec((None, D_MODEL, tn), lambda l, n: (l, 0, n)),
            pl.BlockSpec((None, 1, tn), lambda l, n: (l, 0, n)),
        ],
        out_specs=pl.BlockSpec((None, MOD_ROWS, tn), lambda l, n: (l, 0, n)),
        out_shape=jax.ShapeDtypeStruct((DEPTH, MOD_ROWS, n_out), F32),
        compiler_params=_params("arbitrary", "arbitrary"),
        name="ada_mod",
    )(cc, w_ada, b_ada.reshape(DEPTH, 1, n_out))


def _inproj_kernel(*refs, rope, n_prev, seq_len):
    if rope:
        x_ref, mod_ref, nw_ref, w_ref, cos_ref, sin_ref, z_ref = refs
    else:
        x_ref, mod_ref, nw_ref, w_ref = refs[:4]
        prev_refs = refs[4:4 + 2 * n_prev]
        z_ref, ko_ref, vo_ref = refs[4 + 2 * n_prev:]
    shift = mod_ref[:, 0:D_MODEL]
    scale = mod_ref[:, D_MODEL:2 * D_MODEL]
    n_rows = x_ref.shape[0]
    sub = n_rows // INPROJ_SPLIT
    for r0 in range(0, n_rows, sub):
        x = x_ref[r0:r0 + sub, :]
        h = _rms(x, nw_ref[...]) * (1.0 + scale) + shift
        z = jnp.dot(h.astype(BF16), w_ref[...], preferred_element_type=F32)
        if not rope:
            z_ref[r0:r0 + sub, :] = z.astype(BF16)
            for q0 in range(0, sub, seq_len):
                sq = (r0 + q0) // seq_len
                if n_prev:
                    for lp in range(n_prev):
                        ko_ref[sq, lp] = prev_refs[2 * lp][sq]
                        vo_ref[sq, lp] = prev_refs[2 * lp + 1][sq]
                    ko_l, vo_l = ko_ref.at[sq, n_prev], vo_ref.at[sq, n_prev]
                else:
                    ko_l, vo_l = ko_ref.at[sq], vo_ref.at[sq]
                for hd in range(N_ATT_HEADS):
                    c0 = D_GROUP + hd * V_DIM
                    ko_l[hd] = z[q0:q0 + seq_len, c0:c0 + V_DIM]
                    vo_l[hd] = z[q0:q0 + seq_len, D_GROUP + c0:D_GROUP + c0 + V_DIM]
            continue
        cos = cos_ref[r0:r0 + sub, :]
        sin = sin_ref[r0:r0 + sub, :]
        first_half = (lax.broadcasted_iota(jnp.int32, (1, LANES), 1) % 16) < 8
        for j in range(2 * D_GROUP // LANES):
            zc = z[:, j * LANES:(j + 1) * LANES]
            partner = jnp.where(first_half,
                                pltpu.roll(zc, LANES - 8, 1),
                                pltpu.roll(zc, 8, 1))
            z_ref[r0:r0 + sub, j * LANES:(j + 1) * LANES] = (zc * cos + partner * sin).astype(BF16)
        z_ref[r0:r0 + sub, 2 * D_GROUP:] = z[:, 2 * D_GROUP:].astype(BF16)


def _inproj(x, mod, layer, mod_row, norm_w, w_in, rope_tabs, tt, kv_prev=()):
    b, t, _ = x.shape
    rope = rope_tabs is not None
    n_prev = len(kv_prev) // 2
    if rope:
        grid = (b, t // tt)
        tok = lambda i, j: (i, j, 0)
        x3 = x
    else:
        grid = (1, b * t // tt)
        tok = lambda i, j: (0, j, 0)
        x3 = x.reshape(1, b * t, D_MODEL)
    in_specs = [
        pl.BlockSpec((None, tt, D_MODEL), tok),
        pl.BlockSpec((None, None, 1, N_MOD * D_MODEL), lambda i, j: (layer, mod_row(i), 0, 0)),
        _resident((None, None, 1, D_MODEL), lambda i, j: (layer, 0, 0, 0)),
        _resident((None, D_MODEL, D_IN), lambda i, j: (layer, 0, 0)),
    ]
    args = [x3, mod, norm_w, w_in]
    out_specs = [pl.BlockSpec((None, tt, D_IN), tok)]
    out_shape = [jax.ShapeDtypeStruct(x3.shape[:2] + (D_IN,), BF16)]
    if rope:
        in_specs += [pl.BlockSpec((tt, LANES), lambda i, j: (j, 0))] * 2
        args += list(rope_tabs)
    else:
        n_seq = tt // t
        per_layer = pl.BlockSpec((n_seq, N_ATT_HEADS, t, V_DIM), lambda i, j: (j, 0, 0, 0))
        in_specs += [per_layer] * (2 * n_prev)
        args += list(kv_prev)
        if n_prev:
            out_specs += [pl.BlockSpec((n_seq, n_prev + 1, N_ATT_HEADS, t, V_DIM),
                                       lambda i, j: (j, 0, 0, 0, 0))] * 2
            out_shape += [jax.ShapeDtypeStruct((b, n_prev + 1, N_ATT_HEADS, t, V_DIM), F32)] * 2
        else:
            out_specs += [per_layer] * 2
            out_shape += [jax.ShapeDtypeStruct((b, N_ATT_HEADS, t, V_DIM), F32)] * 2
    outs = pl.pallas_call(
        functools.partial(_inproj_kernel, rope=rope, n_prev=n_prev, seq_len=t),
        grid=grid,
        in_specs=in_specs,
        out_specs=out_specs,
        out_shape=out_shape,
        compiler_params=_params("arbitrary", "arbitrary"),
        name="inproj",
    )(*args)
    return [outs[0].reshape(b, t, D_IN)] + list(outs[1:])


ATT_ROWS = 128
LOG2E = math.log2(math.e)


def _attn_kernel(*refs, has_ctx, lam_init, t_len, n_steps):
    if has_ctx:
        (q_ref, qn_ref, k_ref, v_ref, ck_ref, cv_ref, dl_ref, sw_ref, g_ref, o_ref,
         kt, vs, s0, s1, pb) = refs
    else:
        q_ref, qn_ref, k_ref, v_ref, dl_ref, sw_ref, g_ref, o_ref, kt, vs, s0, s1, pb = refs
    rows = ATT_ROWS
    past = ck_ref.shape[1] if has_ctx else 0
    s_len = past + t_len
    lane = lax.broadcasted_iota(jnp.int32, (1, D_GROUP), 1)

    def scores(q_rows, dst):
        q = q_rows.astype(F32) * (DK ** -0.5 * LOG2E)
        stack = jnp.concatenate(
            [jnp.where((lane // DK) == hm, q, 0.0).astype(BF16) for hm in range(2 * N_ATT_HEADS)],
            axis=0)
        dst[...] = jnp.dot(stack, kt[...], preferred_element_type=F32)

    @pl.when(pl.program_id(1) == 0)
    def _():
        if has_ctx:
            heads = range(N_ATT_HEADS)
            ck = jnp.concatenate([ck_ref[h] for h in heads], axis=-1)
            cv = jnp.concatenate([cv_ref[h] for h in heads], axis=-1)
            kt[:, 0:past] = ck.T.astype(BF16)
            vs[0:past, :] = cv.astype(BF16)
        kt[:, past:s_len] = k_ref[...].astype(F32).T.astype(BF16)
        vs[past:s_len, :] = v_ref[...]
        scores(q_ref[0:rows, :], s0)

    dl = dl_ref[...]
    lam = (jnp.exp(jnp.sum(dl[0:1] * dl[1:2], axis=-1, keepdims=True))
           - jnp.exp(jnp.sum(dl[2:3] * dl[3:4], axis=-1, keepdims=True)) + lam_init)

    def finish(src, out_lo):
        inv_l1 = jnp.zeros((rows, D_GROUP), F32)
        for h in range(N_ATT_HEADS):
            es, ls = [], []
            for m in range(2):
                lo = (2 * h + m) * rows
                s = src[lo:lo + rows, :]
                e = jnp.exp2(s - jnp.max(s, axis=-1, keepdims=True))
                es.append(e)
                ls.append(jnp.sum(e, axis=-1, keepdims=True))
            beta = lam * ls[0] / ls[1]
            pb[h * rows:(h + 1) * rows, :] = (es[0] - beta * es[1]).astype(BF16)
            inv_l1 = jnp.where((lane // V_DIM) == h, 1.0 / ls[0], inv_l1)
        full = jnp.dot(pb[...], vs[...], preferred_element_type=F32)
        acc = full[0:rows]
        for h in range(1, N_ATT_HEADS):
            acc = jnp.where((lane // V_DIM) == h, full[h * rows:(h + 1) * rows], acc)
        acc = acc * inv_l1
        ms = jnp.dot(acc * acc, g_ref[...], preferred_element_type=F32,
                     precision=lax.Precision.HIGHEST)
        o_ref[out_lo:out_lo + rows, :] = (
            acc * lax.rsqrt(ms + EPS) * sw_ref[...] * (1.0 - lam_init)).astype(BF16)

    scores(q_ref[rows:2 * rows, :], s1)
    finish(s0, 0)
    if n_steps > 1:
        scores(qn_ref[0:rows, :], s0)
    finish(s1, rows)


def _attn(z, ctx, layer, dl, subw, gmat, lam_init):
    b, t, _ = z.shape
    has_ctx = ctx is not None
    past = ctx[0].shape[3] if has_ctx else 0
    s_len = past + t
    tq = 2 * ATT_ROWS
    n_steps = t // tq
    in_specs = [
        pl.BlockSpec((None, tq, D_GROUP), lambda i, j: (i, j, 0)),
        pl.BlockSpec((None, tq, D_GROUP), lambda i, j: (i, jnp.minimum(j + 1, n_steps - 1), 0)),
        pl.BlockSpec((None, t, D_GROUP), lambda i, j: (i, 0, 1)),
        pl.BlockSpec((None, t, D_GROUP), lambda i, j: (i, 0, 2)),
    ]
    args = [z, z, z, z]
    if has_ctx:
        in_specs += [pl.BlockSpec((None, None, N_ATT_HEADS, past, V_DIM),
                                  lambda i, j: (i, layer, 0, 0, 0))] * 2
        args += list(ctx)
    in_specs += [
        _resident((None, 4, DK), lambda i, j: (layer, 0, 0)),
        _resident((None, 1, D_GROUP), lambda i, j: (layer, 0, 0)),
        _resident((D_GROUP, D_GROUP), lambda i, j: (0, 0)),
    ]
    args += [dl, subw, gmat]
    n_hm = 2 * N_ATT_HEADS
    return pl.pallas_call(
        functools.partial(_attn_kernel, has_ctx=has_ctx, lam_init=lam_init, t_len=t,
                          n_steps=n_steps),
        grid=(b, n_steps),
        in_specs=in_specs,
        out_specs=pl.BlockSpec((None, tq, D_GROUP), lambda i, j: (i, j, 0)),
        out_shape=jax.ShapeDtypeStruct((b, t, D_GROUP), BF16),
        scratch_shapes=[pltpu.VMEM((D_GROUP, s_len), BF16),
                        pltpu.VMEM((s_len, D_GROUP), BF16),
                        pltpu.VMEM((n_hm * ATT_ROWS, s_len), F32),
                        pltpu.VMEM((n_hm * ATT_ROWS, s_len), F32),
                        pltpu.VMEM((N_ATT_HEADS * ATT_ROWS, s_len), BF16)],
        compiler_params=_params("arbitrary", "arbitrary"),
        name="diff_attn",
    )(*args)


RG_ROWS = 512
RG_BLOCK = 64


def _strided8(ref, half, start):
    return ref[half, pl.ds(start, SUBLANES, stride=SUBLANES), :]


def _rg_gates(xc, wg, bg, c_dir):
    g = jnp.dot(xc.astype(BF16), wg, preferred_element_type=F32) + bg
    r = jax.nn.sigmoid(g[:, :D_GROUP])
    i = jax.nn.sigmoid(g[:, D_GROUP:])
    a = jnp.exp2(r * c_dir)
    om = 1.0 - a * a
    b = jnp.where(om > 0.0, om * lax.rsqrt(om), 0.0) * i * xc
    return a, b


def _scan_block(a, b, hp, reverse):
    n = SUBLANES
    order = list(range(n - 1, -1, -1)) if reverse else list(range(n))
    hs, cum = [None] * n, [None] * n
    prev = None
    for i in order:
        if prev is None:
            hs[i], cum[i] = b[i], a[i]
        else:
            hs[i], cum[i] = a[i] * hs[prev] + b[i], a[i] * cum[prev]
        prev = i
    p, f = cum[prev], hs[prev]
    row = lax.broadcasted_iota(jnp.int32, (n, LANES), 0)
    for d in (1, 2, 4):
        m = (row < n - d) if reverse else (row >= d)
        sh = n - d if reverse else d
        f = jnp.where(m, p * pltpu.roll(f, sh, 0) + f, f)
        p = jnp.where(m, p * pltpu.roll(p, sh, 0), p)
    first, last = (n - 1, 0) if reverse else (0, n - 1)
    sh = n - 1 if reverse else 1
    cin = jnp.where(row == first, hp, pltpu.roll(f, sh, 0) + pltpu.roll(p, sh, 0) * hp)
    out = [hs[i] + cum[i] * cin for i in range(n)]
    bc = lambda v: jnp.broadcast_to(v[last:last + 1, :], (n, LANES))
    return out, bc(f) + bc(p) * hp


def _rg_kernel(xr_ref, gr_ref, h0_ref, cw_ref, cb_ref, wg_ref, bg_ref, lam_ref,
               y_ref, last_ref, xs, gs, ys, xc_s, hf_s, *, t_len):
    rg_rows = min(RG_ROWS, t_len)
    n_chunks = t_len // rg_rows
    n_blk = rg_rows // RG_BLOCK
    zeros_halo = jnp.zeros((HALO, LANES), F32)
    for half in range(2):
        lo = half * LANES
        xs[half, 0:HALO, :] = zeros_halo
        xs[half, HALO + t_len:, :] = zeros_halo
        xs[half, HALO:HALO + t_len, :] = xr_ref[:, lo:lo + LANES].astype(F32)
        gs[half, :, :] = gr_ref[:, lo:lo + LANES].astype(F32)

    neg = -lam_ref[...]
    softplus = jnp.maximum(neg, 0.0) + jnp.log1p(jnp.exp(-jnp.abs(neg)))
    c_all = (-RG_C * LOG2E) * softplus
    cw = cw_ref[...]
    cb = cb_ref[...]

    def piece(v, blk, i, half):
        r0 = blk * RG_BLOCK + i * SUBLANES
        return v[r0:r0 + SUBLANES, half * LANES:(half + 1) * LANES]

    def scan_chunk(a, b, carry, reverse):
        carry = list(carry)
        out = {}
        for blk in (reversed(range(n_blk)) if reverse else range(n_blk)):
            for half in range(2):
                hs, carry[half] = _scan_block([piece(a, blk, i, half) for i in range(SUBLANES)],
                                              [piece(b, blk, i, half) for i in range(SUBLANES)],
                                              carry[half], reverse)
                for i in range(SUBLANES):
                    out[blk, i, half] = hs[i]
        rows = [jnp.concatenate([out[blk, i, 0], out[blk, i, 1]], axis=-1)
                for blk in range(n_blk) for i in range(SUBLANES)]
        return jnp.concatenate(rows, axis=0), tuple(carry)

    def start_state(d):
        return tuple(jnp.broadcast_to(h0_ref[d:d + 1, half * LANES:(half + 1) * LANES],
                                      (SUBLANES, LANES)) for half in range(2))

    def fwd_body(c, carry):
        s = pl.multiple_of(c * rg_rows, rg_rows)
        rows = []
        for blk in range(n_blk):
            base = s + blk * RG_BLOCK + HALO
            halves = []
            for half in range(2):
                lo = half * LANES
                x = {i: _strided8(xs, half, base + i) for i in range(-2, SUBLANES + 1)}
                halves.append([cw[0:1, lo:lo + LANES] * x[i - 2] + cw[1:2, lo:lo + LANES] * x[i - 1]
                               + cw[2:3, lo:lo + LANES] * x[i] + cw[3:4, lo:lo + LANES] * x[i + 1]
                               + cb[:, lo:lo + LANES] for i in range(SUBLANES)])
            rows += [jnp.concatenate([halves[0][i], halves[1][i]], axis=-1) for i in range(SUBLANES)]
        xc = jnp.concatenate(rows, axis=0)
        xc_s[pl.ds(s, rg_rows), :] = xc
        a, b = _rg_gates(xc, wg_ref[:, 0:2 * D_GROUP], bg_ref[:, 0:2 * D_GROUP], c_all[0:1])
        hf, carry = scan_chunk(a, b, carry, False)
        hf_s[pl.ds(s, rg_rows), :] = hf
        return carry

    h_f = lax.fori_loop(0, n_chunks, fwd_body, start_state(0))
    last_ref[0:1, :] = jnp.concatenate([h_f[0][0:1], h_f[1][0:1]], axis=-1)

    def bwd_body(ci, carry):
        c = n_chunks - 1 - ci
        s = pl.multiple_of(c * rg_rows, rg_rows)
        xc = xc_s[pl.ds(s, rg_rows), :]
        a, b = _rg_gates(xc, wg_ref[:, 2 * D_GROUP:], bg_ref[:, 2 * D_GROUP:], c_all[1:2])
        hb, carry = scan_chunk(a, b, carry, True)
        gr = jnp.concatenate(
            [jnp.concatenate([_strided8(gs, half, s + blk * RG_BLOCK + i) for half in range(2)], axis=-1)
             for blk in range(n_blk) for i in range(SUBLANES)], axis=0)
        k_gelu = math.sqrt(2.0 / math.pi)
        hg = 0.5 * gr
        gelu = hg + hg * jnp.tanh(gr * (k_gelu + (k_gelu * 0.044715) * (gr * gr)))
        y = (hf_s[pl.ds(s, rg_rows), :] + hb) * gelu
        for blk in range(n_blk):
            for i in range(SUBLANES):
                for half in range(2):
                    ys[half, pl.ds(s + blk * RG_BLOCK + i, SUBLANES, stride=SUBLANES), :] = (
                        piece(y, blk, i, half))
        y_ref[pl.ds(s, rg_rows), :] = jnp.concatenate(
            [ys[0, pl.ds(s, rg_rows), :], ys[1, pl.ds(s, rg_rows), :]], axis=-1).astype(BF16)
        return carry

    h_b = lax.fori_loop(0, n_chunks, bwd_body, start_state(1))
    last_ref[1:2, :] = jnp.concatenate([h_b[0][0:1], h_b[1][0:1]], axis=-1)


CP_ROWS = 512


def _window_sums(x, lo, hi, widths):
    out, prev, w = {}, {i: x[i] for i in range(lo, hi + 1)}, 1
    while w < max(widths):
        prev = {i: prev[i] + prev[i - w] for i in prev if i - w in prev}
        w *= 2
        out[w] = prev
    return out


def _convpool_kernel(gb_ref, gc_ref, xb_ref, xp_ref, cw_ref, pw_ref, ps_ref,
                     yb_ref, yd_ref, us, xs, cs, ds, *, t_len):
    cp_rows = min(CP_ROWS, t_len)
    n_chunks = t_len // cp_rows
    n_blk = cp_rows // RG_BLOCK
    zeros_halo = jnp.zeros((HALO, LANES), F32)
    for half in range(2):
        for pad in (us, xs):
            pad[half, 0:HALO, :] = zeros_halo
            pad[half, HALO + t_len:, :] = zeros_halo

    def fill(c, carry):
        s = pl.multiple_of(c * cp_rows, cp_rows)
        u = gc_ref[pl.ds(s, cp_rows), :].astype(F32) * xb_ref[pl.ds(s, cp_rows), :].astype(F32)
        x = xp_ref[pl.ds(s, cp_rows), :].astype(F32)
        for half in range(2):
            us[half, pl.ds(HALO + s, cp_rows), :] = u[:, half * LANES:(half + 1) * LANES]
            xs[half, pl.ds(HALO + s, cp_rows), :] = x[:, half * LANES:(half + 1) * LANES]
        return carry

    lax.fori_loop(0, n_chunks, fill, 0)

    cw = cw_ref[...]
    low_group = lax.broadcasted_iota(jnp.int32, (1, LANES), 1) < POOL_GW
    sub8 = lax.broadcasted_iota(jnp.int32, (SUBLANES, LANES), 0) * SUBLANES
    wins = [jnp.where(low_group, POOL_WINDOWS[2 * half], POOL_WINDOWS[2 * half + 1])
            for half in range(2)]

    def body(c, carry):
        s = pl.multiple_of(c * cp_rows, cp_rows)
        pm_rows = []
        for blk in range(n_blk):
            r0 = s + blk * RG_BLOCK
            halves = []
            for half in range(2):
                lo = half * LANES
                u = {i: _strided8(us, half, r0 + HALO + i) for i in range(-1, SUBLANES + 1)}
                for i in range(SUBLANES):
                    cs[half, pl.ds(r0 + i, SUBLANES, stride=SUBLANES), :] = (
                        cw[0:1, lo:lo + LANES] * u[i - 1] + cw[1:2, lo:lo + LANES] * u[i]
                        + cw[2:3, lo:lo + LANES] * u[i + 1])
                w_lo, w_hi = POOL_WINDOWS[2 * half], POOL_WINDOWS[2 * half + 1]
                p_lo, p_hi = -(w_hi // 2), SUBLANES - 1 + w_hi // 2 - 1
                x = {i: _strided8(xs, half, r0 + HALO + i) for i in range(p_lo, p_hi + 1)}
                sums = _window_sums(x, p_lo, p_hi, (w_lo, w_hi))
                win = wins[half]
                left = win // 2
                right = win - 1 - left
                pieces = []
                for i in range(SUBLANES):
                    tot = jnp.where(low_group, sums[w_lo][i + w_lo // 2 - 1], sums[w_hi][i + w_hi // 2 - 1])
                    t = sub8 + (r0 + i)
                    cnt = jnp.minimum(t + right, t_len - 1) - jnp.maximum(t - left, 0) + 1
                    pieces.append(tot / cnt.astype(F32) - x[i])
                halves.append(pieces)
            pm_rows += [jnp.concatenate([halves[0][i], halves[1][i]], axis=-1)
                        for i in range(SUBLANES)]
        pm = jnp.concatenate(pm_rows, axis=0)
        yd = jnp.dot(pm.astype(BF16), pw_ref[...], preferred_element_type=F32) * ps_ref[...]
        for blk in range(n_blk):
            for i in range(SUBLANES):
                k0 = blk * RG_BLOCK + i * SUBLANES
                for half in range(2):
                    ds[half, pl.ds(s + blk * RG_BLOCK + i, SUBLANES, stride=SUBLANES), :] = (
                        yd[k0:k0 + SUBLANES, half * LANES:(half + 1) * LANES])
        conv = jnp.concatenate([cs[0, pl.ds(s, cp_rows), :], cs[1, pl.ds(s, cp_rows), :]], axis=-1)
        yb_ref[pl.ds(s, cp_rows), :] = (gb_ref[pl.ds(s, cp_rows), :].astype(F32) * conv).astype(BF16)
        yd_ref[pl.ds(s, cp_rows), :] = jnp.concatenate(
            [ds[0, pl.ds(s, cp_rows), :], ds[1, pl.ds(s, cp_rows), :]], axis=-1).astype(BF16)
        return carry

    lax.fori_loop(0, n_chunks, body, 0)


def _seqmix_kernel(*refs, t_len, n_seq):
    seq_in, weights, seq_out, scratch = refs[:7], refs[7:15], refs[15:19], refs[19:]
    ccw_ref, ccb_ref, wg_ref, bg_ref, lam_ref, bcw_ref, pw_ref, ps_ref = weights
    for sq in range(n_seq):
        xr_ref, gr_ref, gb_ref, gc_ref, xb_ref, xp_ref, h0_ref = [r.at[sq] for r in seq_in]
        yb_ref, yc_ref, yd_ref, last_ref = [r.at[sq] for r in seq_out]
        xs, gs, ys, xc_s, hf_s, us, ps_s, ds = [r.at[sq] for r in scratch]
        _rg_kernel(xr_ref, gr_ref, h0_ref, ccw_ref, ccb_ref, wg_ref, bg_ref, lam_ref,
                   yc_ref, last_ref, xs, gs, ys, xc_s, hf_s, t_len=t_len)
        _convpool_kernel(gb_ref, gc_ref, xb_ref, xp_ref, bcw_ref, pw_ref, ps_ref,
                         yb_ref, yd_ref, us, ps_s, ys, ds, t_len=t_len)


SEQMIX_ROWS = 1024


def _seqmix(z, h0, layer, conv_c_w, conv_c_b, wg, bg, lam, conv_b_w, pw, ps):
    b, t, _ = z.shape
    n_seq = max(1, SEQMIX_ROWS // t)
    col = lambda k: pl.BlockSpec((n_seq, t, D_GROUP), lambda i: (i, 0, k))
    out = pl.BlockSpec((n_seq, t, D_GROUP), lambda i: (i, 0, 0))
    slab = lambda rows: pltpu.VMEM((n_seq, 2, rows, LANES), F32)
    flat = pltpu.VMEM((n_seq, t, D_GROUP), F32)
    if h0.ndim == 4:
        h0_spec = pl.BlockSpec((n_seq, None, 2, D_GROUP), lambda i: (i, layer, 0, 0))
    else:
        h0_spec = pl.BlockSpec((n_seq, 2, D_GROUP), lambda i: (i, 0, 0))
    yb, yc, yd, last = pl.pallas_call(
        functools.partial(_seqmix_kernel, t_len=t, n_seq=n_seq),
        grid=(b // n_seq,),
        in_specs=[col(6), col(7), col(3), col(4), col(5), col(8), h0_spec,
                  _resident((None, 4, D_GROUP), lambda i: (layer, 0, 0)),
                  _resident((None, 1, D_GROUP), lambda i: (layer, 0, 0)),
                  _resident((None, D_GROUP, 4 * D_GROUP), lambda i: (layer, 0, 0)),
                  _resident((None, 1, 4 * D_GROUP), lambda i: (layer, 0, 0)),
                  _resident((None, 2, D_GROUP), lambda i: (layer, 0, 0)),
                  _resident((None, 3, D_GROUP), lambda i: (layer, 0, 0)),
                  _resident((None, D_GROUP, D_GROUP), lambda i: (layer, 0, 0)),
                  _resident((None, 1, D_GROUP), lambda i: (layer, 0, 0))],
        out_specs=[out, out, out, pl.BlockSpec((n_seq, 2, D_GROUP), lambda i: (i, 0, 0))],
        out_shape=[jax.ShapeDtypeStruct((b, t, D_GROUP), BF16)] * 3
        + [jax.ShapeDtypeStruct((b, 2, D_GROUP), F32)],
        scratch_shapes=[slab(t + 2 * HALO), slab(t), slab(t), flat, flat,
                        slab(t + 2 * HALO), slab(t + 2 * HALO), slab(t)],
        compiler_params=_params("arbitrary"),
        name="seqmix",
    )(z, z, z, z, z, z, h0, conv_c_w, conv_c_b, wg, bg, lam, conv_b_w, pw, ps)
    return (yb, yc, yd), last


FF_CHUNK = 1024


def _mlp_kernel(x_ref, ya_ref, yb_ref, yc_ref, yd_ref, mod_ref, nw_ref, wo_ref, w1_ref, w2_ref,
                fw_ref, o_ref, *, final):
    x = x_ref[...]
    y = jnp.zeros(x.shape, F32)
    for i, r in enumerate((ya_ref, yb_ref, yc_ref, yd_ref)):
        y = y + jnp.dot(r[...], wo_ref[i * D_GROUP:(i + 1) * D_GROUP, :],
                        preferred_element_type=F32)
    gate1 = mod_ref[:, 2 * D_MODEL:3 * D_MODEL]
    shift2 = mod_ref[:, 3 * D_MODEL:4 * D_MODEL]
    scale2 = mod_ref[:, 4 * D_MODEL:5 * D_MODEL]
    gate2 = mod_ref[:, 5 * D_MODEL:6 * D_MODEL]
    x1 = x + gate1 * y
    hn = (_rms(x1, nw_ref[...]) * (1.0 + scale2) + shift2).astype(BF16)
    acc = jnp.zeros(x.shape, F32)
    for f in range(D_FF // FF_CHUNK):
        u = jnp.dot(hn, w1_ref[:, f * FF_CHUNK:(f + 1) * FF_CHUNK], preferred_element_type=F32)
        u = jnp.square(jnp.maximum(u, 0.0)).astype(BF16)
        acc = acc + jnp.dot(u, w2_ref[f * FF_CHUNK:(f + 1) * FF_CHUNK, :], preferred_element_type=F32)
    x2 = x1 + gate2 * acc
    if final:
        x2 = _rms(x2, fw_ref[...])
    o_ref[...] = x2


def _mlp(x, ys, mod, layer, mod_row, norm_w, wo, w1, w2, fw, final, tt, shared_mod=False):
    b, t, _ = x.shape
    if shared_mod:
        x, ys = x.reshape(1, b * t, D_MODEL), [y.reshape(1, b * t, D_GROUP) for y in ys]
    nb, nt, _ = x.shape
    ytile = pl.BlockSpec((None, tt, D_GROUP), lambda i, j: (i, j, 0))
    out = pl.pallas_call(
        functools.partial(_mlp_kernel, final=final),
        grid=(nb, nt // tt),
        in_specs=[
            pl.BlockSpec((None, tt, D_MODEL), lambda i, j: (i, j, 0)),
            ytile, ytile, ytile, ytile,
            pl.BlockSpec((None, None, 1, N_MOD * D_MODEL), lambda i, j: (layer, mod_row(i), 0, 0)),
            _resident((None, None, 1, D_MODEL), lambda i, j: (layer, 1, 0, 0)),
            _resident((None, D_MODEL, D_MODEL), lambda i, j: (layer, 0, 0)),
            _resident((None, D_MODEL, D_FF), lambda i, j: (layer, 0, 0)),
            _resident((None, D_FF, D_MODEL), lambda i, j: (layer, 0, 0)),
            _resident((1, D_MODEL), lambda i, j: (0, 0)),
        ],
        out_specs=pl.BlockSpec((None, tt, D_MODEL), lambda i, j: (i, j, 0)),
        out_shape=jax.ShapeDtypeStruct((nb, nt, D_MODEL), F32),
        compiler_params=_params("arbitrary", "arbitrary"),
        name="outproj_mlp",
    )(x, *ys, mod, norm_w, wo, w1, w2, fw)
    return out.reshape(b, t, D_MODEL)


def _block_diag(w):
    *lead, n, k, _ = w.shape
    eye = jnp.eye(n, dtype=w.dtype)
    return (eye[:, None, :, None] * w[..., :, :, None, :]).reshape(*lead, n * k, n * k)


def _rope_tables(t_len):
    half = DK // 4
    freqs = ROPE_BASE ** (-np.arange(half, dtype=np.float64) / half)
    lane = np.arange(LANES)
    m = lane % DK
    use_col = m >= DK // 2
    fidx = m % half
    t = np.arange(t_len)
    pos = np.where(use_col[None, :], (t % GRID_W)[:, None], (t // GRID_W)[:, None])
    ang = pos * freqs[fidx][None, :]
    sign = np.where((lane % (2 * half)) < half, -1.0, 1.0)
    return (jnp.asarray(np.cos(ang), dtype=F32), jnp.asarray(np.sin(ang) * sign[None, :], dtype=F32))


def kernel(x_prompt, x_sample, cache_k, cache_v, state_rglru, c, c_ctx, w_ada, b_ada, norm_w, w_in, diff_lambda, subln_w, conv_b_w, conv_c_w, conv_c_b, rg_w, rg_b, rg_lambda, pool_w, pool_scale, w_out, w_mlp1, w_mlp2, final_norm_w):
    n_lat = c.shape[0]
    ctx_row = n_lat
    cc = jnp.zeros((MOD_ROWS, D_MODEL), F32).at[:n_lat].set(c).at[ctx_row].set(c_ctx)
    mod = _ada(cc, w_ada, b_ada).reshape(DEPTH, MOD_ROWS, 1, N_MOD * D_MODEL)

    w_in_b = w_in.astype(BF16)
    w_out_b = w_out.astype(BF16)
    w1_b = w_mlp1.astype(BF16)
    w2_b = w_mlp2.astype(BF16)
    norm_w4 = norm_w.reshape(DEPTH, 2, 1, D_MODEL)
    rope_tabs = _rope_tables(x_sample.shape[1])
    gmat = _block_diag(jnp.full((N_ATT_HEADS, V_DIM, V_DIM), 1.0 / V_DIM, F32))
    subw = jnp.tile(subln_w, (1, N_ATT_HEADS)).reshape(DEPTH, 1, D_GROUP)
    wg = _block_diag(rg_w).transpose(0, 3, 1, 2, 4).reshape(DEPTH, D_GROUP, 4 * D_GROUP).astype(BF16)
    bg = rg_b.reshape(DEPTH, 1, 4 * D_GROUP)
    pw = _block_diag(pool_w).astype(BF16)
    ps = pool_scale.reshape(DEPTH, 1, D_GROUP)
    cb = conv_c_b.reshape(DEPTH, 1, D_GROUP)
    fw = final_norm_w.reshape(1, D_MODEL)
    zero_state = jnp.zeros((x_prompt.shape[0], 2, D_GROUP), F32)

    xp, xs = x_prompt, x_sample
    kv_out, hs_out = [], []
    for l in range(DEPTH):
        lam_init = 0.8 - 0.6 * math.exp(-0.3 * l)
        final = l == DEPTH - 1

        def mixers(z, ctx, h0):
            ya = _attn(z, ctx, l, diff_lambda, subw, gmat, lam_init)
            (yb, yc, yd), last = _seqmix(z, h0, l, conv_c_w, cb, wg, bg, rg_lambda,
                                         conv_b_w, pw, ps)
            return (ya, yb, yc, yd), last

        ctx_rows = lambda i: ctx_row
        kv_prev = kv_out if final else ()
        zp, k_new, v_new = _inproj(xp, mod, l, ctx_rows, norm_w4, w_in_b, None, INPROJ_TILE, kv_prev)
        ys, last_p = mixers(zp, None, zero_state)
        xp = _mlp(xp, ys, mod, l, ctx_rows, norm_w4, w_out_b, w1_b, w2_b, fw, final, TOKEN_TILE,
                  shared_mod=True)
        kv_out += [k_new, v_new]
        hs_out.append(last_p)

        lat_rows = lambda i: i
        (zs,) = _inproj(xs, mod, l, lat_rows, norm_w4, w_in_b, rope_tabs, INPROJ_TILE)
        ys, _ = mixers(zs, (cache_k, cache_v), state_rglru)
        xs = _mlp(xs, ys, mod, l, lat_rows, norm_w4, w_out_b, w1_b, w2_b, fw, final, TOKEN_TILE)

    return (xp, xs, kv_out[-2], kv_out[-1], jnp.stack(hs_out, axis=1))
```

```python
import functools
import math

import jax
import jax.numpy as jnp
import numpy as np
from jax import lax
from jax.experimental import pallas as pl
from jax.experimental.pallas import tpu as pltpu

D_MODEL = 1024
DEPTH = 2
GRID_W = 64
D_GROUP = 256
N_ATT_HEADS = 4
V_DIM = 64
DK = 32
ROPE_BASE = 10000.0
RG_BLOCKS = 4
RG_BW = 64
RG_C = 8.0
POOL_WINDOWS = (2, 4, 8, 16)
POOL_GW = 64
D_FF = 4 * D_MODEL
N_MOD = 6
D_IN = 9 * D_GROUP
EPS = 1e-6

LANES = 128
SUBLANES = 8
HALO = 8
VMEM_LIMIT = 56 * 1024 * 1024
MOD_ROWS = 16
TOKEN_TILE = 512
INPROJ_TILE = 1024
INPROJ_SPLIT = 4

BF16 = jnp.bfloat16
F32 = jnp.float32


def _params(*sem):
    return pltpu.CompilerParams(dimension_semantics=sem, vmem_limit_bytes=VMEM_LIMIT)


def _resident(shape, index_map):
    return pl.BlockSpec(shape, index_map, pipeline_mode=pl.Buffered(1))


def _rms(x, w):
    ms = jnp.mean(x * x, axis=-1, keepdims=True)
    return x * lax.rsqrt(ms + EPS) * w


def _ada_kernel(c_ref, w_ref, b_ref, o_ref):
    c = c_ref[...]
    s = (c * jax.nn.sigmoid(c)).astype(BF16)
    o_ref[...] = jnp.dot(s, w_ref[...].astype(BF16), preferred_element_type=F32) + b_ref[...]


def _ada(cc, w_ada, b_ada):
    tn = 1536
    n_out = N_MOD * D_MODEL
    return pl.pallas_call(
        _ada_kernel,
        grid=(DEPTH, n_out // tn),
        in_specs=[
            pl.BlockSpec((MOD_ROWS, D_MODEL), lambda l, n: (0, 0)),
            pl.BlockSpec((None, D_MODEL, tn), lambda l, n: (l, 0, n)),
            pl.BlockSpec((None, 1, tn), lambda l, n: (l, 0, n)),
        ],
        out_specs=pl.BlockSpec((None, MOD_ROWS, tn), lambda l, n: (l, 0, n)),
        out_shape=jax.ShapeDtypeStruct((DEPTH, MOD_ROWS, n_out), F32),
        compiler_params=_params("arbitrary", "arbitrary"),
        name="ada_mod",
    )(cc, w_ada, b_ada.reshape(DEPTH, 1, n_out))


def _inproj_kernel(*refs, rope, n_prev, seq_len):
    if rope:
        x_ref, mod_ref, nw_ref, w_ref, cos_ref, sin_ref, z_ref = refs
    else:
        x_ref, mod_ref, nw_ref, w_ref = refs[:4]
        prev_refs = refs[4:4 + 2 * n_prev]
        z_ref, ko_ref, vo_ref = refs[4 + 2 * n_prev:]
    shift = mod_ref[:, 0:D_MODEL]
    scale = mod_ref[:, D_MODEL:2 * D_MODEL]
    n_rows = x_ref.shape[0]
    sub = n_rows // INPROJ_SPLIT
    for r0 in range(0, n_rows, sub):
        x = x_ref[r0:r0 + sub, :]
        h = _rms(x, nw_ref[...]) * (1.0 + scale) + shift
        z = jnp.dot(h.astype(BF16), w_ref[...], preferred_element_type=F32)
        if not rope:
            z_ref[r0:r0 + sub, :] = z.astype(BF16)
            for q0 in range(0, sub, seq_len):
                sq = (r0 + q0) // seq_len
                if n_prev:
                    for lp in range(n_prev):
                        ko_ref[sq, lp] = prev_refs[2 * lp][sq]
                        vo_ref[sq, lp] = prev_refs[2 * lp + 1][sq]
                    ko_l, vo_l = ko_ref.at[sq, n_prev], vo_ref.at[sq, n_prev]
                else:
                    ko_l, vo_l = ko_ref.at[sq], vo_ref.at[sq]
                for hd in range(N_ATT_HEADS):
                    c0 = D_GROUP + hd * V_DIM
                    ko_l[hd] = z[q0:q0 + seq_len, c0:c0 + V_DIM]
                    vo_l[hd] = z[q0:q0 + seq_len, D_GROUP + c0:D_GROUP + c0 + V_DIM]
            continue
        cos = cos_ref[r0:r0 + sub, :]
        sin = sin_ref[r0:r0 + sub, :]
        first_half = (lax.broadcasted_iota(jnp.int32, (1, LANES), 1) % 16) < 8
        for j in range(2 * D_GROUP // LANES):
            zc = z[:, j * LANES:(j + 1) * LANES]
            partner = jnp.where(first_half,
                                pltpu.roll(zc, LANES - 8, 1),
                                pltpu.roll(zc, 8, 1))
            z_ref[r0:r0 + sub, j * LANES:(j + 1) * LANES] = (zc * cos + partner * sin).astype(BF16)
        z_ref[r0:r0 + sub, 2 * D_GROUP:] = z[:, 2 * D_GROUP:].astype(BF16)


def _inproj(x, mod, layer, mod_row, norm_w, w_in, rope_tabs, tt, kv_prev=()):
    b, t, _ = x.shape
    rope = rope_tabs is not None
    n_prev = len(kv_prev) // 2
    if rope:
        grid = (b, t // tt)
        tok = lambda i, j: (i, j, 0)
        x3 = x
    else:
        grid = (1, b * t // tt)
        tok = lambda i, j: (0, j, 0)
        x3 = x.reshape(1, b * t, D_MODEL)
    in_specs = [
        pl.BlockSpec((None, tt, D_MODEL), tok),
        pl.BlockSpec((None, None, 1, N_MOD * D_MODEL), lambda i, j: (layer, mod_row(i), 0, 0)),
        _resident((None, None, 1, D_MODEL), lambda i, j: (layer, 0, 0, 0)),
        _resident((None, D_MODEL, D_IN), lambda i, j: (layer, 0, 0)),
    ]
    args = [x3, mod, norm_w, w_in]
    out_specs = [pl.BlockSpec((None, tt, D_IN), tok)]
    out_shape = [jax.ShapeDtypeStruct(x3.shape[:2] + (D_IN,), BF16)]
    if rope:
        in_specs += [pl.BlockSpec((tt, LANES), lambda i, j: (j, 0))] * 2
        args += list(rope_tabs)
    else:
        n_seq = tt // t
        per_layer = pl.BlockSpec((n_seq, N_ATT_HEADS, t, V_DIM), lambda i, j: (j, 0, 0, 0))
        in_specs += [per_layer] * (2 * n_prev)
        args += list(kv_prev)
        if n_prev:
            out_specs += [pl.BlockSpec((n_seq, n_prev + 1, N_ATT_HEADS, t, V_DIM),
                                       lambda i, j: (j, 0, 0, 0, 0))] * 2
            out_shape += [jax.ShapeDtypeStruct((b, n_prev + 1, N_ATT_HEADS, t, V_DIM), F32)] * 2
        else:
            out_specs += [per_layer] * 2
            out_shape += [jax.ShapeDtypeStruct((b, N_ATT_HEADS, t, V_DIM), F32)] * 2
    outs = pl.pallas_call(
        functools.partial(_inproj_kernel, rope=rope, n_prev=n_prev, seq_len=t),
        grid=grid,
        in_specs=in_specs,
        out_specs=out_specs,
        out_shape=out_shape,
        compiler_params=_params("arbitrary", "arbitrary"),
        name="inproj",
    )(*args)
    return [outs[0].reshape(b, t, D_IN)] + list(outs[1:])


ATT_ROWS = 128
LOG2E = math.log2(math.e)


def _attn_kernel(*refs, has_ctx, lam_init, t_len, n_steps):
    if has_ctx:
        (q_ref, qn_ref, k_ref, v_ref, ck_ref, cv_ref, dl_ref, sw_ref, g_ref, o_ref,
         kt, vs, s0, s1, pb) = refs
    else:
        q_ref, qn_ref, k_ref, v_ref, dl_ref, sw_ref, g_ref, o_ref, kt, vs, s0, s1, pb = refs
    rows = ATT_ROWS
    past = ck_ref.shape[1] if has_ctx else 0
    s_len = past + t_len
    lane = lax.broadcasted_iota(jnp.int32, (1, D_GROUP), 1)

    def scores(q_rows, dst):
        q = q_rows.astype(F32) * (DK ** -0.5 * LOG2E)
        stack = jnp.concatenate(
            [jnp.where((lane // DK) == hm, q, 0.0).astype(BF16) for hm in range(2 * N_ATT_HEADS)],
            axis=0)
        dst[...] = jnp.dot(stack, kt[...], preferred_element_type=F32)

    @pl.when(pl.program_id(1) == 0)
    def _():
        if has_ctx:
            heads = range(N_ATT_HEADS)
            ck = jnp.concatenate([ck_ref[h] for h in heads], axis=-1)
            cv = jnp.concatenate([cv_ref[h] for h in heads], axis=-1)
            kt[:, 0:past] = ck.T.astype(BF16)
            vs[0:past, :] = cv.astype(BF16)
        kt[:, past:s_len] = k_ref[...].T
        vs[past:s_len, :] = v_ref[...]
        scores(q_ref[0:rows, :], s0)

    dl = dl_ref[...]
    lam = (jnp.exp(jnp.sum(dl[0:1] * dl[1:2], axis=-1, keepdims=True))
           - jnp.exp(jnp.sum(dl[2:3] * dl[3:4], axis=-1, keepdims=True)) + lam_init)

    def finish(src, out_lo):
        inv_l1 = jnp.zeros((rows, D_GROUP), F32)
        for h in range(N_ATT_HEADS):
            es, ls = [], []
            for m in range(2):
                lo = (2 * h + m) * rows
                s = src[lo:lo + rows, :]
                e = jnp.exp2(s - jnp.max(s, axis=-1, keepdims=True))
                es.append(e)
                ls.append(jnp.sum(e, axis=-1, keepdims=True))
            beta = lam * ls[0] / ls[1]
            pb[h * rows:(h + 1) * rows, :] = (es[0] - beta * es[1]).astype(BF16)
            inv_l1 = jnp.where((lane // V_DIM) == h, 1.0 / ls[0], inv_l1)
        full = jnp.dot(pb[...], vs[...], preferred_element_type=F32)
        acc = full[0:rows]
        for h in range(1, N_ATT_HEADS):
            acc = jnp.where((lane // V_DIM) == h, full[h * rows:(h + 1) * rows], acc)
        acc = acc * inv_l1
        ms = jnp.dot(acc * acc, g_ref[...], preferred_element_type=F32,
                     precision=lax.Precision.HIGHEST)
        o_ref[out_lo:out_lo + rows, :] = (
            acc * lax.rsqrt(ms + EPS) * sw_ref[...] * (1.0 - lam_init)).astype(BF16)

    scores(q_ref[rows:2 * rows, :], s1)
    finish(s0, 0)
    if n_steps > 1:
        scores(qn_ref[0:rows, :], s0)
    finish(s1, rows)


def _attn(z, ctx, layer, dl, subw, gmat, lam_init):
    b, t, _ = z.shape
    has_ctx = ctx is not None
    past = ctx[0].shape[3] if has_ctx else 0
    s_len = past + t
    tq = 2 * ATT_ROWS
    n_steps = t // tq
    in_specs = [
        pl.BlockSpec((None, tq, D_GROUP), lambda i, j: (i, j, 0)),
        pl.BlockSpec((None, tq, D_GROUP), lambda i, j: (i, jnp.minimum(j + 1, n_steps - 1), 0)),
        pl.BlockSpec((None, t, D_GROUP), lambda i, j: (i, 0, 1)),
        pl.BlockSpec((None, t, D_GROUP), lambda i, j: (i, 0, 2)),
    ]
    args = [z, z, z, z]
    if has_ctx:
        in_specs += [pl.BlockSpec((None, None, N_ATT_HEADS, past, V_DIM),
                                  lambda i, j: (i, layer, 0, 0, 0))] * 2
        args += list(ctx)
    in_specs += [
        _resident((None, 4, DK), lambda i, j: (layer, 0, 0)),
        _resident((None, 1, D_GROUP), lambda i, j: (layer, 0, 0)),
        _resident((D_GROUP, D_GROUP), lambda i, j: (0, 0)),
    ]
    args += [dl, subw, gmat]
    n_hm = 2 * N_ATT_HEADS
    return pl.pallas_call(
        functools.partial(_attn_kernel, has_ctx=has_ctx, lam_init=lam_init, t_len=t,
                          n_steps=n_steps),
        grid=(b, n_steps),
        in_specs=in_specs,
        out_specs=pl.BlockSpec((None, tq, D_GROUP), lambda i, j: (i, j, 0)),
        out_shape=jax.ShapeDtypeStruct((b, t, D_GROUP), BF16),
        scratch_shapes=[pltpu.VMEM((D_GROUP, s_len), BF16),
                        pltpu.VMEM((s_len, D_GROUP), BF16),
                        pltpu.VMEM((n_hm * ATT_ROWS, s_len), F32),
                        pltpu.VMEM((n_hm * ATT_ROWS, s_len), F32),
                        pltpu.VMEM((N_ATT_HEADS * ATT_ROWS, s_len), BF16)],
        compiler_params=_params("arbitrary", "arbitrary"),
        name="diff_attn",
    )(*args)


RG_ROWS = 512
RG_BLOCK = 64


def _strided8(ref, half, start):
    return ref[half, pl.ds(start, SUBLANES, stride=SUBLANES), :]


def _rg_gates(xc, wg, bg, c_dir):
    g = jnp.dot(xc.astype(BF16), wg, preferred_element_type=F32) + bg
    r = jax.nn.sigmoid(g[:, :D_GROUP])
    i = jax.nn.sigmoid(g[:, D_GROUP:])
    a = jnp.exp2(r * c_dir)
    om = 1.0 - a * a
    b = jnp.where(om > 0.0, om * lax.rsqrt(om), 0.0) * i * xc
    return a, b


def _scan_block(a, b, hp, reverse):
    n = SUBLANES
    order = list(range(n - 1, -1, -1)) if reverse else list(range(n))
    hs, cum = [None] * n, [None] * n
    prev = None
    for i in order:
        if prev is None:
            hs[i], cum[i] = b[i], a[i]
        else:
            hs[i], cum[i] = a[i] * hs[prev] + b[i], a[i] * cum[prev]
        prev = i
    p, f = cum[prev], hs[prev]
    row = lax.broadcasted_iota(jnp.int32, (n, LANES), 0)
    for d in (1, 2, 4):
        m = (row < n - d) if reverse else (row >= d)
        sh = n - d if reverse else d
        f = jnp.where(m, p * pltpu.roll(f, sh, 0) + f, f)
        p = jnp.where(m, p * pltpu.roll(p, sh, 0), p)
    first, last = (n - 1, 0) if reverse else (0, n - 1)
    sh = n - 1 if reverse else 1
    cin = jnp.where(row == first, hp, pltpu.roll(f, sh, 0) + pltpu.roll(p, sh, 0) * hp)
    out = [hs[i] + cum[i] * cin for i in range(n)]
    bc = lambda v: jnp.broadcast_to(v[last:last + 1, :], (n, LANES))
    return out, bc(f) + bc(p) * hp


def _rg_kernel(xr_ref, gr_ref, h0_ref, cw_ref, cb_ref, wg_ref, bg_ref, lam_ref,
               y_ref, last_ref, xs, gs, ys, xc_s, hf_s, *, t_len):
    rg_rows = min(RG_ROWS, t_len)
    n_chunks = t_len // rg_rows
    n_blk = rg_rows // RG_BLOCK
    zeros_halo = jnp.zeros((HALO, LANES), F32)
    for half in range(2):
        lo = half * LANES
        xs[half, 0:HALO, :] = zeros_halo
        xs[half, HALO + t_len:, :] = zeros_halo
        xs[half, HALO:HALO + t_len, :] = xr_ref[:, lo:lo + LANES].astype(F32)
        gs[half, :, :] = gr_ref[:, lo:lo + LANES].astype(F32)

    neg = -lam_ref[...]
    softplus = jnp.maximum(neg, 0.0) + jnp.log1p(jnp.exp(-jnp.abs(neg)))
    c_all = (-RG_C * LOG2E) * softplus
    cw = cw_ref[...]
    cb = cb_ref[...]

    def piece(v, blk, i, half):
        r0 = blk * RG_BLOCK + i * SUBLANES
        return v[r0:r0 + SUBLANES, half * LANES:(half + 1) * LANES]

    def scan_chunk(a, b, carry, reverse):
        carry = list(carry)
        out = {}
        for blk in (reversed(range(n_blk)) if reverse else range(n_blk)):
            for half in range(2):
                hs, carry[half] = _scan_block([piece(a, blk, i, half) for i in range(SUBLANES)],
                                              [piece(b, blk, i, half) for i in range(SUBLANES)],
                                              carry[half], reverse)
                for i in range(SUBLANES):
                    out[blk, i, half] = hs[i]
        rows = [jnp.concatenate([out[blk, i, 0], out[blk, i, 1]], axis=-1)
                for blk in range(n_blk) for i in range(SUBLANES)]
        return jnp.concatenate(rows, axis=0), tuple(carry)

    def start_state(d):
        return tuple(jnp.broadcast_to(h0_ref[d:d + 1, half * LANES:(half + 1) * LANES],
                                      (SUBLANES, LANES)) for half in range(2))

    def fwd_body(c, carry):
        s = pl.multiple_of(c * rg_rows, rg_rows)
        rows = []
        for blk in range(n_blk):
            base = s + blk * RG_BLOCK + HALO
            halves = []
            for half in range(2):
                lo = half * LANES
                x = {i: _strided8(xs, half, base + i) for i in range(-2, SUBLANES + 1)}
                halves.append([cw[0:1, lo:lo + LANES] * x[i - 2] + cw[1:2, lo:lo + LANES] * x[i - 1]
                               + cw[2:3, lo:lo + LANES] * x[i] + cw[3:4, lo:lo + LANES] * x[i + 1]
                               + cb[:, lo:lo + LANES] for i in range(SUBLANES)])
            rows += [jnp.concatenate([halves[0][i], halves[1][i]], axis=-1) for i in range(SUBLANES)]
        xc = jnp.concatenate(rows, axis=0)
        xc_s[pl.ds(s, rg_rows), :] = xc
        a, b = _rg_gates(xc, wg_ref[:, 0:2 * D_GROUP], bg_ref[:, 0:2 * D_GROUP], c_all[0:1])
        hf, carry = scan_chunk(a, b, carry, False)
        hf_s[pl.ds(s, rg_rows), :] = hf
        return carry

    h_f = lax.fori_loop(0, n_chunks, fwd_body, start_state(0))
    last_ref[0:1, :] = jnp.concatenate([h_f[0][0:1], h_f[1][0:1]], axis=-1)

    def bwd_body(ci, carry):
        c = n_chunks - 1 - ci
        s = pl.multiple_of(c * rg_rows, rg_rows)
        xc = xc_s[pl.ds(s, rg_rows), :]
        a, b = _rg_gates(xc, wg_ref[:, 2 * D_GROUP:], bg_ref[:, 2 * D_GROUP:], c_all[1:2])
        hb, carry = scan_chunk(a, b, carry, True)
        gr = jnp.concatenate(
            [jnp.concatenate([_strided8(gs, half, s + blk * RG_BLOCK + i) for half in range(2)], axis=-1)
             for blk in range(n_blk) for i in range(SUBLANES)], axis=0)
        k_gelu = math.sqrt(2.0 / math.pi)
        hg = 0.5 * gr
        gelu = hg + hg * jnp.tanh(gr * (k_gelu + (k_gelu * 0.044715) * (gr * gr)))
        y = (hf_s[pl.ds(s, rg_rows), :] + hb) * gelu
        for blk in range(n_blk):
            for i in range(SUBLANES):
                for half in range(2):
                    ys[half, pl.ds(s + blk * RG_BLOCK + i, SUBLANES, stride=SUBLANES), :] = (
                        piece(y, blk, i, half))
        y_ref[pl.ds(s, rg_rows), :] = jnp.concatenate(
            [ys[0, pl.ds(s, rg_rows), :], ys[1, pl.ds(s, rg_rows), :]], axis=-1).astype(BF16)
        return carry

    h_b = lax.fori_loop(0, n_chunks, bwd_body, start_state(1))
    last_ref[1:2, :] = jnp.concatenate([h_b[0][0:1], h_b[1][0:1]], axis=-1)


CP_ROWS = 512


def _window_sums(x, lo, hi, widths):
    out, prev, w = {}, {i: x[i] for i in range(lo, hi + 1)}, 1
    while w < max(widths):
        prev = {i: prev[i] + prev[i - w] for i in prev if i - w in prev}
        w *= 2
        out[w] = prev
    return out


def _convpool_kernel(gb_ref, gc_ref, xb_ref, xp_ref, cw_ref, pw_ref, ps_ref,
                     yb_ref, yd_ref, us, xs, cs, ds, *, t_len):
    cp_rows = min(CP_ROWS, t_len)
    n_chunks = t_len // cp_rows
    n_blk = cp_rows // RG_BLOCK
    zeros_halo = jnp.zeros((HALO, LANES), F32)
    for half in range(2):
        for pad in (us, xs):
            pad[half, 0:HALO, :] = zeros_halo
            pad[half, HALO + t_len:, :] = zeros_halo

    def fill(c, carry):
        s = pl.multiple_of(c * cp_rows, cp_rows)
        u = gc_ref[pl.ds(s, cp_rows), :].astype(F32) * xb_ref[pl.ds(s, cp_rows), :].astype(F32)
        x = xp_ref[pl.ds(s, cp_rows), :].astype(F32)
        for half in range(2):
            us[half, pl.ds(HALO + s, cp_rows), :] = u[:, half * LANES:(half + 1) * LANES]
            xs[half, pl.ds(HALO + s, cp_rows), :] = x[:, half * LANES:(half + 1) * LANES]
        return carry

    lax.fori_loop(0, n_chunks, fill, 0)

    cw = cw_ref[...]
    low_group = lax.broadcasted_iota(jnp.int32, (1, LANES), 1) < POOL_GW
    sub8 = lax.broadcasted_iota(jnp.int32, (SUBLANES, LANES), 0) * SUBLANES
    wins = [jnp.where(low_group, POOL_WINDOWS[2 * half], POOL_WINDOWS[2 * half + 1])
            for half in range(2)]

    def body(c, carry):
        s = pl.multiple_of(c * cp_rows, cp_rows)
        pm_rows = []
        for blk in range(n_blk):
            r0 = s + blk * RG_BLOCK
            halves = []
            for half in range(2):
                lo = half * LANES
                u = {i: _strided8(us, half, r0 + HALO + i) for i in range(-1, SUBLANES + 1)}
                for i in range(SUBLANES):
                    cs[half, pl.ds(r0 + i, SUBLANES, stride=SUBLANES), :] = (
                        cw[0:1, lo:lo + LANES] * u[i - 1] + cw[1:2, lo:lo + LANES] * u[i]
                        + cw[2:3, lo:lo + LANES] * u[i + 1])
                w_lo, w_hi = POOL_WINDOWS[2 * half], POOL_WINDOWS[2 * half + 1]
                p_lo, p_hi = -(w_hi // 2), SUBLANES - 1 + w_hi // 2 - 1
                x = {i: _strided8(xs, half, r0 + HALO + i) for i in range(p_lo, p_hi + 1)}
                sums = _window_sums(x, p_lo, p_hi, (w_lo, w_hi))
                win = wins[half]
                left = win // 2
                right = win - 1 - left
                pieces = []
                for i in range(SUBLANES):
                    tot = jnp.where(low_group, sums[w_lo][i + w_lo // 2 - 1], sums[w_hi][i + w_hi // 2 - 1])
                    t = sub8 + (r0 + i)
                    cnt = jnp.minimum(t + right, t_len - 1) - jnp.maximum(t - left, 0) + 1
                    pieces.append(tot / cnt.astype(F32) - x[i])
                halves.append(pieces)
            pm_rows += [jnp.concatenate([halves[0][i], halves[1][i]], axis=-1)
                        for i in range(SUBLANES)]
        pm = jnp.concatenate(pm_rows, axis=0)
        yd = jnp.dot(pm.astype(BF16), pw_ref[...], preferred_element_type=F32) * ps_ref[...]
        for blk in range(n_blk):
            for i in range(SUBLANES):
                k0 = blk * RG_BLOCK + i * SUBLANES
                for half in range(2):
                    ds[half, pl.ds(s + blk * RG_BLOCK + i, SUBLANES, stride=SUBLANES), :] = (
                        yd[k0:k0 + SUBLANES, half * LANES:(half + 1) * LANES])
        conv = jnp.concatenate([cs[0, pl.ds(s, cp_rows), :], cs[1, pl.ds(s, cp_rows), :]], axis=-1)
        yb_ref[pl.ds(s, cp_rows), :] = (gb_ref[pl.ds(s, cp_rows), :].astype(F32) * conv).astype(BF16)
        yd_ref[pl.ds(s, cp_rows), :] = jnp.concatenate(
            [ds[0, pl.ds(s, cp_rows), :], ds[1, pl.ds(s, cp_rows), :]], axis=-1).astype(BF16)
        return carry

    lax.fori_loop(0, n_chunks, body, 0)


def _seqmix_kernel(*refs, t_len, n_seq):
    seq_in, weights, seq_out, scratch = refs[:7], refs[7:15], refs[15:19], refs[19:]
    ccw_ref, ccb_ref, wg_ref, bg_ref, lam_ref, bcw_ref, pw_ref, ps_ref = weights
    for sq in range(n_seq):
        xr_ref, gr_ref, gb_ref, gc_ref, xb_ref, xp_ref, h0_ref = [r.at[sq] for r in seq_in]
        yb_ref, yc_ref, yd_ref, last_ref = [r.at[sq] for r in seq_out]
        xs, gs, ys, xc_s, hf_s, us, ps_s, ds = [r.at[sq] for r in scratch]
        _rg_kernel(xr_ref, gr_ref, h0_ref, ccw_ref, ccb_ref, wg_ref, bg_ref, lam_ref,
                   yc_ref, last_ref, xs, gs, ys, xc_s, hf_s, t_len=t_len)
        _convpool_kernel(gb_ref, gc_ref, xb_ref, xp_ref, bcw_ref, pw_ref, ps_ref,
                         yb_ref, yd_ref, us, ps_s, ys, ds, t_len=t_len)


SEQMIX_ROWS = 1024


def _seqmix(z, h0, layer, conv_c_w, conv_c_b, wg, bg, lam, conv_b_w, pw, ps):
    b, t, _ = z.shape
    n_seq = max(1, SEQMIX_ROWS // t)
    col = lambda k: pl.BlockSpec((n_seq, t, D_GROUP), lambda i: (i, 0, k))
    out = pl.BlockSpec((n_seq, t, D_GROUP), lambda i: (i, 0, 0))
    slab = lambda rows: pltpu.VMEM((n_seq, 2, rows, LANES), F32)
    flat = pltpu.VMEM((n_seq, t, D_GROUP), F32)
    if h0.ndim == 4:
        h0_spec = pl.BlockSpec((n_seq, None, 2, D_GROUP), lambda i: (i, layer, 0, 0))
    else:
        h0_spec = pl.BlockSpec((n_seq, 2, D_GROUP), lambda i: (i, 0, 0))
    yb, yc, yd, last = pl.pallas_call(
        functools.partial(_seqmix_kernel, t_len=t, n_seq=n_seq),
        grid=(b // n_seq,),
        in_specs=[col(6), col(7), col(3), col(4), col(5), col(8), h0_spec,
                  _resident((None, 4, D_GROUP), lambda i: (layer, 0, 0)),
                  _resident((None, 1, D_GROUP), lambda i: (layer, 0, 0)),
                  _resident((None, D_GROUP, 4 * D_GROUP), lambda i: (layer, 0, 0)),
                  _resident((None, 1, 4 * D_GROUP), lambda i: (layer, 0, 0)),
                  _resident((None, 2, D_GROUP), lambda i: (layer, 0, 0)),
                  _resident((None, 3, D_GROUP), lambda i: (layer, 0, 0)),
                  _resident((None, D_GROUP, D_GROUP), lambda i: (layer, 0, 0)),
                  _resident((None, 1, D_GROUP), lambda i: (layer, 0, 0))],
        out_specs=[out, out, out, pl.BlockSpec((n_seq, 2, D_GROUP), lambda i: (i, 0, 0))],
        out_shape=[jax.ShapeDtypeStruct((b, t, D_GROUP), BF16)] * 3
        + [jax.ShapeDtypeStruct((b, 2, D_GROUP), F32)],
        scratch_shapes=[slab(t + 2 * HALO), slab(t), slab(t), flat, flat,
                        slab(t + 2 * HALO), slab(t + 2 * HALO), slab(t)],
        compiler_params=_params("arbitrary"),
        name="seqmix",
    )(z, z, z, z, z, z, h0, conv_c_w, conv_c_b, wg, bg, lam, conv_b_w, pw, ps)
    return (yb, yc, yd), last


FF_CHUNK = 1024


def _mlp_kernel(x_ref, ya_ref, yb_ref, yc_ref, yd_ref, mod_ref, nw_ref, wo_ref, w1_ref, w2_ref,
                fw_ref, o_ref, *, final):
    x = x_ref[...]
    y = jnp.zeros(x.shape, F32)
    for i, r in enumerate((ya_ref, yb_ref, yc_ref, yd_ref)):
        y = y + jnp.dot(r[...], wo_ref[i * D_GROUP:(i + 1) * D_GROUP, :],
                        preferred_element_type=F32)
    gate1 = mod_ref[:, 2 * D_MODEL:3 * D_MODEL]
    shift2 = mod_ref[:, 3 * D_MODEL:4 * D_MODEL]
    scale2 = mod_ref[:, 4 * D_MODEL:5 * D_MODEL]
    gate2 = mod_ref[:, 5 * D_MODEL:6 * D_MODEL]
    x1 = x + gate1 * y
    hn = (_rms(x1, nw_ref[...]) * (1.0 + scale2) + shift2).astype(BF16)
    acc = jnp.zeros(x.shape, F32)
    for f in range(D_FF // FF_CHUNK):
        u = jnp.dot(hn, w1_ref[:, f * FF_CHUNK:(f + 1) * FF_CHUNK], preferred_element_type=F32)
        u = jnp.square(jnp.maximum(u, 0.0)).astype(BF16)
        acc = acc + jnp.dot(u, w2_ref[f * FF_CHUNK:(f + 1) * FF_CHUNK, :], preferred_element_type=F32)
    x2 = x1 + gate2 * acc
    if final:
        x2 = _rms(x2, fw_ref[...])
    o_ref[...] = x2


def _mlp(x, ys, mod, layer, mod_row, norm_w, wo, w1, w2, fw, final, tt, shared_mod=False):
    b, t, _ = x.shape
    if shared_mod:
        x, ys = x.reshape(1, b * t, D_MODEL), [y.reshape(1, b * t, D_GROUP) for y in ys]
    nb, nt, _ = x.shape
    ytile = pl.BlockSpec((None, tt, D_GROUP), lambda i, j: (i, j, 0))
    out = pl.pallas_call(
        functools.partial(_mlp_kernel, final=final),
        grid=(nb, nt // tt),
        in_specs=[
            pl.BlockSpec((None, tt, D_MODEL), lambda i, j: (i, j, 0)),
            ytile, ytile, ytile, ytile,
            pl.BlockSpec((None, None, 1, N_MOD * D_MODEL), lambda i, j: (layer, mod_row(i), 0, 0)),
            _resident((None, None, 1, D_MODEL), lambda i, j: (layer, 1, 0, 0)),
            _resident((None, D_MODEL, D_MODEL), lambda i, j: (layer, 0, 0)),
            _resident((None, D_MODEL, D_FF), lambda i, j: (layer, 0, 0)),
            _resident((None, D_FF, D_MODEL), lambda i, j: (layer, 0, 0)),
            _resident((1, D_MODEL), lambda i, j: (0, 0)),
        ],
        out_specs=pl.BlockSpec((None, tt, D_MODEL), lambda i, j: (i, j, 0)),
        out_shape=jax.ShapeDtypeStruct((nb, nt, D_MODEL), F32),
        compiler_params=_params("arbitrary", "arbitrary"),
        name="outproj_mlp",
    )(x, *ys, mod, norm_w, wo, w1, w2, fw)
    return out.reshape(b, t, D_MODEL)


def _block_diag(w):
    *lead, n, k, _ = w.shape
    eye = jnp.eye(n, dtype=w.dtype)
    return (eye[:, None, :, None] * w[..., :, :, None, :]).reshape(*lead, n * k, n * k)


def _rope_tables(t_len):
    half = DK // 4
    freqs = ROPE_BASE ** (-np.arange(half, dtype=np.float64) / half)
    lane = np.arange(LANES)
    m = lane % DK
    use_col = m >= DK // 2
    fidx = m % half
    t = np.arange(t_len)
    pos = np.where(use_col[None, :], (t % GRID_W)[:, None], (t // GRID_W)[:, None])
    ang = pos * freqs[fidx][None, :]
    sign = np.where((lane % (2 * half)) < half, -1.0, 1.0)
    return (jnp.asarray(np.cos(ang), dtype=F32), jnp.asarray(np.sin(ang) * sign[None, :], dtype=F32))


def kernel(x_prompt, x_sample, cache_k, cache_v, state_rglru, c, c_ctx, w_ada, b_ada, norm_w, w_in, diff_lambda, subln_w, conv_b_w, conv_c_w, conv_c_b, rg_w, rg_b, rg_lambda, pool_w, pool_scale, w_out, w_mlp1, w_mlp2, final_norm_w):
    n_lat = c.shape[0]
    ctx_row = n_lat
    cc = jnp.zeros((MOD_ROWS, D_MODEL), F32).at[:n_lat].set(c).at[ctx_row].set(c_ctx)
    mod = _ada(cc, w_ada, b_ada).reshape(DEPTH, MOD_ROWS, 1, N_MOD * D_MODEL)

    w_in_b = w_in.astype(BF16)
    w_out_b = w_out.astype(BF16)
    w1_b = w_mlp1.astype(BF16)
    w2_b = w_mlp2.astype(BF16)
    norm_w4 = norm_w.reshape(DEPTH, 2, 1, D_MODEL)
    rope_tabs = _rope_tables(x_sample.shape[1])
    gmat = _block_diag(jnp.full((N_ATT_HEADS, V_DIM, V_DIM), 1.0 / V_DIM, F32))
    subw = jnp.tile(subln_w, (1, N_ATT_HEADS)).reshape(DEPTH, 1, D_GROUP)
    wg = _block_diag(rg_w).transpose(0, 3, 1, 2, 4).reshape(DEPTH, D_GROUP, 4 * D_GROUP).astype(BF16)
    bg = rg_b.reshape(DEPTH, 1, 4 * D_GROUP)
    pw = _block_diag(pool_w).astype(BF16)
    ps = pool_scale.reshape(DEPTH, 1, D_GROUP)
    cb = conv_c_b.reshape(DEPTH, 1, D_GROUP)
    fw = final_norm_w.reshape(1, D_MODEL)
    zero_state = jnp.zeros((x_prompt.shape[0], 2, D_GROUP), F32)

    xp, xs = x_prompt, x_sample
    kv_out, hs_out = [], []
    for l in range(DEPTH):
        lam_init = 0.8 - 0.6 * math.exp(-0.3 * l)
        final = l == DEPTH - 1

        def mixers(z, ctx, h0):
            ya = _attn(z, ctx, l, diff_lambda, subw, gmat, lam_init)
            (yb, yc, yd), last = _seqmix(z, h0, l, conv_c_w, cb, wg, bg, rg_lambda,
                                         conv_b_w, pw, ps)
            return (ya, yb, yc, yd), last

        ctx_rows = lambda i: ctx_row
        kv_prev = kv_out if final else ()
        zp, k_new, v_new = _inproj(xp, mod, l, ctx_rows, norm_w4, w_in_b, None, INPROJ_TILE, kv_prev)
        ys, last_p = mixers(zp, None, zero_state)
        xp = _mlp(xp, ys, mod, l, ctx_rows, norm_w4, w_out_b, w1_b, w2_b, fw, final, TOKEN_TILE,
                  shared_mod=True)
        kv_out += [k_new, v_new]
        hs_out.append(last_p)

        lat_rows = lambda i: i
        (zs,) = _inproj(xs, mod, l, lat_rows, norm_w4, w_in_b, rope_tabs, INPROJ_TILE)
        ys, _ = mixers(zs, (cache_k, cache_v), state_rglru)
        xs = _mlp(xs, ys, mod, l, lat_rows, norm_w4, w_out_b, w1_b, w2_b, fw, final, TOKEN_TILE)

    return (xp, xs, kv_out[-2], kv_out[-1], jnp.stack(hs_out, axis=1))
```

```python
import functools
import math

import jax
import jax.numpy as jnp
import numpy as np
from jax import lax
from jax.experimental import pallas as pl
from jax.experimental.pallas import tpu as pltpu

D_MODEL = 1024
DEPTH = 2
GRID_W = 64
D_GROUP = 256
N_ATT_HEADS = 4
V_DIM = 64
DK = 32
ROPE_BASE = 10000.0
RG_BLOCKS = 4
RG_BW = 64
RG_C = 8.0
POOL_WINDOWS = (2, 4, 8, 16)
POOL_GW = 64
D_FF = 4 * D_MODEL
N_MOD = 6
D_IN = 9 * D_GROUP
EPS = 1e-6

LANES = 128
SUBLANES = 8
HALO = 8
VMEM_LIMIT = 56 * 1024 * 1024
MOD_ROWS = 16
TOKEN_TILE = 512
INPROJ_TILE = 1024
INPROJ_SPLIT = 4

BF16 = jnp.bfloat16
F32 = jnp.float32


def _params(*sem):
    return pltpu.CompilerParams(dimension_semantics=sem, vmem_limit_bytes=VMEM_LIMIT)


def _resident(shape, index_map):
    return pl.BlockSpec(shape, index_map, pipeline_mode=pl.Buffered(1))


def _rms(x, w):
    ms = jnp.mean(x * x, axis=-1, keepdims=True)
    return x * lax.rsqrt(ms + EPS) * w


def _ada_kernel(c_ref, w_ref, b_ref, o_ref):
    c = c_ref[...]
    s = (c * jax.nn.sigmoid(c)).astype(BF16)
    o_ref[...] = jnp.dot(s, w_ref[...].astype(BF16), preferred_element_type=F32) + b_ref[...]


def _ada(cc, w_ada, b_ada):
    tn = 1536
    n_out = N_MOD * D_MODEL
    return pl.pallas_call(
        _ada_kernel,
        grid=(DEPTH, n_out // tn),
        in_specs=[
            pl.BlockSpec((MOD_ROWS, D_MODEL), lambda l, n: (0, 0)),
            pl.BlockSpec((None, D_MODEL, tn), lambda l, n: (l, 0, n)),
            pl.BlockSpec((None, 1, tn), lambda l, n: (l, 0, n)),
        ],
        out_specs=pl.BlockSpec((None, MOD_ROWS, tn), lambda l, n: (l, 0, n)),
        out_shape=jax.ShapeDtypeStruct((DEPTH, MOD_ROWS, n_out), F32),
        compiler_params=_params("arbitrary", "arbitrary"),
        name="ada_mod",
    )(cc, w_ada, b_ada.reshape(DEPTH, 1, n_out))


def _inproj_kernel(*refs, rope, n_prev, seq_len):
    if rope:
        x_ref, mod_ref, nw_ref, w_ref, cos_ref, sin_ref, z_ref = refs
    else:
        x_ref, mod_ref, nw_ref, w_ref = refs[:4]
        prev_refs = refs[4:4 + 2 * n_prev]
        z_ref, ko_ref, vo_ref = refs[4 + 2 * n_prev:]
    shift = mod_ref[:, 0:D_MODEL]
    scale = mod_ref[:, D_MODEL:2 * D_MODEL]
    n_rows = x_ref.shape[0]
    sub = n_rows // INPROJ_SPLIT
    for r0 in range(0, n_rows, sub):
        x = x_ref[r0:r0 + sub, :]
        h = _rms(x, nw_ref[...]) * (1.0 + scale) + shift
        z = jnp.dot(h.astype(BF16), w_ref[...], preferred_element_type=F32)
        if not rope:
            z_ref[r0:r0 + sub, :] = z.astype(BF16)
            for q0 in range(0, sub, seq_len):
                sq = (r0 + q0) // seq_len
                if n_prev:
                    for lp in range(n_prev):
                        ko_ref[sq, lp] = prev_refs[2 * lp][sq]
                        vo_ref[sq, lp] = prev_refs[2 * lp + 1][sq]
                    ko_l, vo_l = ko_ref.at[sq, n_prev], vo_ref.at[sq, n_prev]
                else:
                    ko_l, vo_l = ko_ref.at[sq], vo_ref.at[sq]
                for hd in range(N_ATT_HEADS):
                    c0 = D_GROUP + hd * V_DIM
                    ko_l[hd] = z[q0:q0 + seq_len, c0:c0 + V_DIM]
                    vo_l[hd] = z[q0:q0 + seq_len, D_GROUP + c0:D_GROUP + c0 + V_DIM]
            continue
        cos = cos_ref[r0:r0 + sub, :]
        sin = sin_ref[r0:r0 + sub, :]
        first_half = (lax.broadcasted_iota(jnp.int32, (1, LANES), 1) % 16) < 8
        for j in range(2 * D_GROUP // LANES):
            zc = z[:, j * LANES:(j + 1) * LANES]
            partner = jnp.where(first_half,
                                pltpu.roll(zc, LANES - 8, 1),
                                pltpu.roll(zc, 8, 1))
            z_ref[r0:r0 + sub, j * LANES:(j + 1) * LANES] = (zc * cos + partner * sin).astype(BF16)
        z_ref[r0:r0 + sub, 2 * D_GROUP:] = z[:, 2 * D_GROUP:].astype(BF16)


def _inproj(x, mod, layer, mod_row, norm_w, w_in, rope_tabs, tt, kv_prev=()):
    b, t, _ = x.shape
    rope = rope_tabs is not None
    n_prev = len(kv_prev) // 2
    if rope:
        grid = (b, t // tt)
        tok = lambda i, j: (i, j, 0)
        x3 = x
    else:
        grid = (1, b * t // tt)
        tok = lambda i, j: (0, j, 0)
        x3 = x.reshape(1, b * t, D_MODEL)
    in_specs = [
        pl.BlockSpec((None, tt, D_MODEL), tok),
        pl.BlockSpec((None, None, 1, N_MOD * D_MODEL), lambda i, j: (layer, mod_row(i), 0, 0)),
        _resident((None, None, 1, D_MODEL), lambda i, j: (layer, 0, 0, 0)),
        _resident((None, D_MODEL, D_IN), lambda i, j: (layer, 0, 0)),
    ]
    args = [x3, mod, norm_w, w_in]
    out_specs = [pl.BlockSpec((None, tt, D_IN), tok)]
    out_shape = [jax.ShapeDtypeStruct(x3.shape[:2] + (D_IN,), BF16)]
    if rope:
        in_specs += [pl.BlockSpec((tt, LANES), lambda i, j: (j, 0))] * 2
        args += list(rope_tabs)
    else:
        n_seq = tt // t
        per_layer = pl.BlockSpec((n_seq, N_ATT_HEADS, t, V_DIM), lambda i, j: (j, 0, 0, 0))
        in_specs += [per_layer] * (2 * n_prev)
        args += list(kv_prev)
        if n_prev:
            out_specs += [pl.BlockSpec((n_seq, n_prev + 1, N_ATT_HEADS, t, V_DIM),
                                       lambda i, j: (j, 0, 0, 0, 0))] * 2
            out_shape += [jax.ShapeDtypeStruct((b, n_prev + 1, N_ATT_HEADS, t, V_DIM), F32)] * 2
        else:
            out_specs += [per_layer] * 2
            out_shape += [jax.ShapeDtypeStruct((b, N_ATT_HEADS, t, V_DIM), F32)] * 2
    outs = pl.pallas_call(
        functools.partial(_inproj_kernel, rope=rope, n_prev=n_prev, seq_len=t),
        grid=grid,
        in_specs=in_specs,
        out_specs=out_specs,
        out_shape=out_shape,
        compiler_params=_params("arbitrary", "arbitrary"),
        name="inproj",
    )(*args)
    return [outs[0].reshape(b, t, D_IN)] + list(outs[1:])


ATT_ROWS = 128
LOG2E = math.log2(math.e)


def _attn_kernel(*refs, has_ctx, lam_init, t_len, n_steps):
    if has_ctx:
        (q_ref, qn_ref, k_ref, v_ref, ck_ref, cv_ref, dl_ref, sw_ref, g_ref, o_ref,
         kt, vs, s0, s1, pb, linv) = refs
    else:
        (q_ref, qn_ref, k_ref, v_ref, dl_ref, sw_ref, g_ref, o_ref,
         kt, vs, s0, s1, pb, linv) = refs
    rows = ATT_ROWS
    past = ck_ref.shape[1] if has_ctx else 0
    s_len = past + t_len
    lane = lax.broadcasted_iota(jnp.int32, (1, D_GROUP), 1)

    def scores(q_rows, dst):
        q = q_rows.astype(F32) * (DK ** -0.5 * LOG2E)
        stack = jnp.concatenate(
            [jnp.where((lane // DK) == hm, q, 0.0).astype(BF16) for hm in range(2 * N_ATT_HEADS)],
            axis=0)
        dst[...] = jnp.dot(stack, kt[...], preferred_element_type=F32)

    @pl.when(pl.program_id(1) == 0)
    def _():
        if has_ctx:
            heads = range(N_ATT_HEADS)
            ck = jnp.concatenate([ck_ref[h] for h in heads], axis=-1)
            cv = jnp.concatenate([cv_ref[h] for h in heads], axis=-1)
            kt[:, 0:past] = ck.T.astype(BF16)
            vs[0:past, :] = cv.astype(BF16)
        kt[:, past:s_len] = k_ref[...].astype(F32).T.astype(BF16)
        vs[past:s_len, :] = v_ref[...]
        scores(q_ref[0:rows, :], s0)

    dl = dl_ref[...]
    lam = (jnp.exp(jnp.sum(dl[0:1] * dl[1:2], axis=-1, keepdims=True))
           - jnp.exp(jnp.sum(dl[2:3] * dl[3:4], axis=-1, keepdims=True)) + lam_init)

    n_hm = 2 * N_ATT_HEADS
    half = rows // 2

    def unit(src, out_lo, nxt_q, nxt_dst):
        qn = None if nxt_q is None else nxt_q.astype(F32) * (DK ** -0.5 * LOG2E)

        def chunk(i, carry):
            if qn is not None:
                stack = jnp.where((lane // DK) == i, qn, 0.0).astype(BF16)
                nxt_dst[pl.ds(pl.multiple_of(i * rows, rows), rows), :] = jnp.dot(
                    stack, kt[...], preferred_element_type=F32)
            h = i // 2
            r0 = (i % 2) * half
            es, ls = [], []
            for m in range(2):
                lo = pl.multiple_of((2 * h + m) * rows + r0, half)
                s = src[pl.ds(lo, half), :]
                e = jnp.exp2(s - jnp.max(s, axis=-1, keepdims=True))
                es.append(e)
                ls.append(jnp.sum(e, axis=-1, keepdims=True))
            beta = lam * ls[0] / ls[1]
            dst0 = pl.multiple_of(h * rows + r0, half)
            pb[pl.ds(dst0, half), :] = (es[0] - beta * es[1]).astype(BF16)
            linv[pl.ds(dst0, half), :] = 1.0 / ls[0]
            return carry

        lax.fori_loop(0, n_hm, chunk, 0)
        full = jnp.dot(pb[...], vs[...], preferred_element_type=F32)
        acc = full[0:rows] * linv[0:rows, :]
        for h in range(1, N_ATT_HEADS):
            acc = jnp.where((lane // V_DIM) == h,
                            full[h * rows:(h + 1) * rows] * linv[h * rows:(h + 1) * rows, :], acc)
        ms = jnp.dot(acc * acc, g_ref[...], preferred_element_type=F32,
                     precision=lax.Precision.HIGHEST)
        o_ref[out_lo:out_lo + rows, :] = (
            acc * lax.rsqrt(ms + EPS) * sw_ref[...] * (1.0 - lam_init)).astype(BF16)

    unit(s0, 0, q_ref[rows:2 * rows, :], s1)
    unit(s1, rows, qn_ref[0:rows, :] if n_steps > 1 else None, s0)


def _attn(z, ctx, layer, dl, subw, gmat, lam_init):
    b, t, _ = z.shape
    has_ctx = ctx is not None
    past = ctx[0].shape[3] if has_ctx else 0
    s_len = past + t
    tq = 2 * ATT_ROWS
    n_steps = t // tq
    in_specs = [
        pl.BlockSpec((None, tq, D_GROUP), lambda i, j: (i, j, 0)),
        pl.BlockSpec((None, tq, D_GROUP), lambda i, j: (i, jnp.minimum(j + 1, n_steps - 1), 0)),
        pl.BlockSpec((None, t, D_GROUP), lambda i, j: (i, 0, 1)),
        pl.BlockSpec((None, t, D_GROUP), lambda i, j: (i, 0, 2)),
    ]
    args = [z, z, z, z]
    if has_ctx:
        in_specs += [pl.BlockSpec((None, None, N_ATT_HEADS, past, V_DIM),
                                  lambda i, j: (i, layer, 0, 0, 0))] * 2
        args += list(ctx)
    in_specs += [
        _resident((None, 4, DK), lambda i, j: (layer, 0, 0)),
        _resident((None, 1, D_GROUP), lambda i, j: (layer, 0, 0)),
        _resident((D_GROUP, D_GROUP), lambda i, j: (0, 0)),
    ]
    args += [dl, subw, gmat]
    n_hm = 2 * N_ATT_HEADS
    return pl.pallas_call(
        functools.partial(_attn_kernel, has_ctx=has_ctx, lam_init=lam_init, t_len=t,
                          n_steps=n_steps),
        grid=(b, n_steps),
        in_specs=in_specs,
        out_specs=pl.BlockSpec((None, tq, D_GROUP), lambda i, j: (i, j, 0)),
        out_shape=jax.ShapeDtypeStruct((b, t, D_GROUP), BF16),
        scratch_shapes=[pltpu.VMEM((D_GROUP, s_len), BF16),
                        pltpu.VMEM((s_len, D_GROUP), BF16),
                        pltpu.VMEM((n_hm * ATT_ROWS, s_len), F32),
                        pltpu.VMEM((n_hm * ATT_ROWS, s_len), F32),
                        pltpu.VMEM((N_ATT_HEADS * ATT_ROWS, s_len), BF16),
                        pltpu.VMEM((N_ATT_HEADS * ATT_ROWS, 1), F32)],
        compiler_params=_params("arbitrary", "arbitrary"),
        name="diff_attn",
    )(*args)


RG_ROWS = 512
RG_BLOCK = 64


def _strided8(ref, half, start):
    return ref[half, pl.ds(start, SUBLANES, stride=SUBLANES), :]


def _rg_gates(xc, wg, bg, c_dir):
    g = jnp.dot(xc.astype(BF16), wg, preferred_element_type=F32) + bg
    r = jax.nn.sigmoid(g[:, :D_GROUP])
    i = jax.nn.sigmoid(g[:, D_GROUP:])
    a = jnp.exp2(r * c_dir)
    om = 1.0 - a * a
    b = jnp.where(om > 0.0, om * lax.rsqrt(om), 0.0) * i * xc
    return a, b


def _scan_block(a, b, hp, reverse):
    n = SUBLANES
    order = list(range(n - 1, -1, -1)) if reverse else list(range(n))
    hs, cum = [None] * n, [None] * n
    prev = None
    for i in order:
        if prev is None:
            hs[i], cum[i] = b[i], a[i]
        else:
            hs[i], cum[i] = a[i] * hs[prev] + b[i], a[i] * cum[prev]
        prev = i
    p, f = cum[prev], hs[prev]
    row = lax.broadcasted_iota(jnp.int32, (n, LANES), 0)
    for d in (1, 2, 4):
        m = (row < n - d) if reverse else (row >= d)
        sh = n - d if reverse else d
        f = jnp.where(m, p * pltpu.roll(f, sh, 0) + f, f)
        p = jnp.where(m, p * pltpu.roll(p, sh, 0), p)
    first, last = (n - 1, 0) if reverse else (0, n - 1)
    sh = n - 1 if reverse else 1
    cin = jnp.where(row == first, hp, pltpu.roll(f, sh, 0) + pltpu.roll(p, sh, 0) * hp)
    out = [hs[i] + cum[i] * cin for i in range(n)]
    bc = lambda v: jnp.broadcast_to(v[last:last + 1, :], (n, LANES))
    return out, bc(f) + bc(p) * hp


def _rg_kernel(xr_ref, gr_ref, h0_ref, cw_ref, cb_ref, wg_ref, bg_ref, lam_ref,
               y_ref, last_ref, xs, gs, ys, xc_s, hf_s, *, t_len):
    rg_rows = min(RG_ROWS, t_len)
    n_chunks = t_len // rg_rows
    n_blk = rg_rows // RG_BLOCK
    zeros_halo = jnp.zeros((HALO, LANES), F32)
    for half in range(2):
        lo = half * LANES
        xs[half, 0:HALO, :] = zeros_halo
        xs[half, HALO + t_len:, :] = zeros_halo
        xs[half, HALO:HALO + t_len, :] = xr_ref[:, lo:lo + LANES].astype(F32)
        gs[half, :, :] = gr_ref[:, lo:lo + LANES].astype(F32)

    neg = -lam_ref[...]
    softplus = jnp.maximum(neg, 0.0) + jnp.log1p(jnp.exp(-jnp.abs(neg)))
    c_all = (-RG_C * LOG2E) * softplus
    cw = cw_ref[...]
    cb = cb_ref[...]

    def piece(v, blk, i, half):
        r0 = blk * RG_BLOCK + i * SUBLANES
        return v[r0:r0 + SUBLANES, half * LANES:(half + 1) * LANES]

    def scan_chunk(a, b, carry, reverse):
        carry = list(carry)
        out = {}
        for blk in (reversed(range(n_blk)) if reverse else range(n_blk)):
            for half in range(2):
                hs, carry[half] = _scan_block([piece(a, blk, i, half) for i in range(SUBLANES)],
                                              [piece(b, blk, i, half) for i in range(SUBLANES)],
                                              carry[half], reverse)
                for i in range(SUBLANES):
                    out[blk, i, half] = hs[i]
        rows = [jnp.concatenate([out[blk, i, 0], out[blk, i, 1]], axis=-1)
                for blk in range(n_blk) for i in range(SUBLANES)]
        return jnp.concatenate(rows, axis=0), tuple(carry)

    def start_state(d):
        return tuple(jnp.broadcast_to(h0_ref[d:d + 1, half * LANES:(half + 1) * LANES],
                                      (SUBLANES, LANES)) for half in range(2))

    def fwd_body(c, carry):
        s = pl.multiple_of(c * rg_rows, rg_rows)
        rows = []
        for blk in range(n_blk):
            base = s + blk * RG_BLOCK + HALO
            halves = []
            for half in range(2):
                lo = half * LANES
                x = {i: _strided8(xs, half, base + i) for i in range(-2, SUBLANES + 1)}
                halves.append([cw[0:1, lo:lo + LANES] * x[i - 2] + cw[1:2, lo:lo + LANES] * x[i - 1]
                               + cw[2:3, lo:lo + LANES] * x[i] + cw[3:4, lo:lo + LANES] * x[i + 1]
                               + cb[:, lo:lo + LANES] for i in range(SUBLANES)])
            rows += [jnp.concatenate([halves[0][i], halves[1][i]], axis=-1) for i in range(SUBLANES)]
        xc = jnp.concatenate(rows, axis=0)
        xc_s[pl.ds(s, rg_rows), :] = xc
        a, b = _rg_gates(xc, wg_ref[:, 0:2 * D_GROUP], bg_ref[:, 0:2 * D_GROUP], c_all[0:1])
        hf, carry = scan_chunk(a, b, carry, False)
        hf_s[pl.ds(s, rg_rows), :] = hf
        return carry

    h_f = lax.fori_loop(0, n_chunks, fwd_body, start_state(0))
    last_ref[0:1, :] = jnp.concatenate([h_f[0][0:1], h_f[1][0:1]], axis=-1)

    def bwd_body(ci, carry):
        c = n_chunks - 1 - ci
        s = pl.multiple_of(c * rg_rows, rg_rows)
        xc = xc_s[pl.ds(s, rg_rows), :]
        a, b = _rg_gates(xc, wg_ref[:, 2 * D_GROUP:], bg_ref[:, 2 * D_GROUP:], c_all[1:2])
        hb, carry = scan_chunk(a, b, carry, True)
        gr = jnp.concatenate(
            [jnp.concatenate([_strided8(gs, half, s + blk * RG_BLOCK + i) for half in range(2)], axis=-1)
             for blk in range(n_blk) for i in range(SUBLANES)], axis=0)
        k_gelu = math.sqrt(2.0 / math.pi)
        hg = 0.5 * gr
        gelu = hg + hg * jnp.tanh(gr * (k_gelu + (k_gelu * 0.044715) * (gr * gr)))
        y = (hf_s[pl.ds(s, rg_rows), :] + hb) * gelu
        for blk in range(n_blk):
            for i in range(SUBLANES):
                for half in range(2):
                    ys[half, pl.ds(s + blk * RG_BLOCK + i, SUBLANES, stride=SUBLANES), :] = (
                        piece(y, blk, i, half))
        y_ref[pl.ds(s, rg_rows), :] = jnp.concatenate(
            [ys[0, pl.ds(s, rg_rows), :], ys[1, pl.ds(s, rg_rows), :]], axis=-1).astype(BF16)
        return carry

    h_b = lax.fori_loop(0, n_chunks, bwd_body, start_state(1))
    last_ref[1:2, :] = jnp.concatenate([h_b[0][0:1], h_b[1][0:1]], axis=-1)


CP_ROWS = 512


def _window_sums(x, lo, hi, widths):
    out, prev, w = {}, {i: x[i] for i in range(lo, hi + 1)}, 1
    while w < max(widths):
        prev = {i: prev[i] + prev[i - w] for i in prev if i - w in prev}
        w *= 2
        out[w] = prev
    return out


def _convpool_kernel(gb_ref, gc_ref, xb_ref, xp_ref, cw_ref, pw_ref, ps_ref,
                     yb_ref, yd_ref, us, xs, cs, ds, *, t_len):
    cp_rows = min(CP_ROWS, t_len)
    n_chunks = t_len // cp_rows
    n_blk = cp_rows // RG_BLOCK
    zeros_halo = jnp.zeros((HALO, LANES), F32)
    for half in range(2):
        for pad in (us, xs):
            pad[half, 0:HALO, :] = zeros_halo
            pad[half, HALO + t_len:, :] = zeros_halo

    def fill(c, carry):
        s = pl.multiple_of(c * cp_rows, cp_rows)
        u = gc_ref[pl.ds(s, cp_rows), :].astype(F32) * xb_ref[pl.ds(s, cp_rows), :].astype(F32)
        x = xp_ref[pl.ds(s, cp_rows), :].astype(F32)
        for half in range(2):
            us[half, pl.ds(HALO + s, cp_rows), :] = u[:, half * LANES:(half + 1) * LANES]
            xs[half, pl.ds(HALO + s, cp_rows), :] = x[:, half * LANES:(half + 1) * LANES]
        return carry

    lax.fori_loop(0, n_chunks, fill, 0)

    cw = cw_ref[...]
    low_group = lax.broadcasted_iota(jnp.int32, (1, LANES), 1) < POOL_GW
    sub8 = lax.broadcasted_iota(jnp.int32, (SUBLANES, LANES), 0) * SUBLANES
    wins = [jnp.where(low_group, POOL_WINDOWS[2 * half], POOL_WINDOWS[2 * half + 1])
            for half in range(2)]

    def body(c, carry):
        s = pl.multiple_of(c * cp_rows, cp_rows)
        pm_rows = []
        for blk in range(n_blk):
            r0 = s + blk * RG_BLOCK
            halves = []
            for half in range(2):
                lo = half * LANES
                u = {i: _strided8(us, half, r0 + HALO + i) for i in range(-1, SUBLANES + 1)}
                for i in range(SUBLANES):
                    cs[half, pl.ds(r0 + i, SUBLANES, stride=SUBLANES), :] = (
                        cw[0:1, lo:lo + LANES] * u[i - 1] + cw[1:2, lo:lo + LANES] * u[i]
                        + cw[2:3, lo:lo + LANES] * u[i + 1])
                w_lo, w_hi = POOL_WINDOWS[2 * half], POOL_WINDOWS[2 * half + 1]
                p_lo, p_hi = -(w_hi // 2), SUBLANES - 1 + w_hi // 2 - 1
                x = {i: _strided8(xs, half, r0 + HALO + i) for i in range(p_lo, p_hi + 1)}
                sums = _window_sums(x, p_lo, p_hi, (w_lo, w_hi))
                win = wins[half]
                left = win // 2
                right = win - 1 - left
                pieces = []
                for i in range(SUBLANES):
                    tot = jnp.where(low_group, sums[w_lo][i + w_lo // 2 - 1], sums[w_hi][i + w_hi // 2 - 1])
                    t = sub8 + (r0 + i)
                    cnt = jnp.minimum(t + right, t_len - 1) - jnp.maximum(t - left, 0) + 1
                    pieces.append(tot / cnt.astype(F32) - x[i])
                halves.append(pieces)
            pm_rows += [jnp.concatenate([halves[0][i], halves[1][i]], axis=-1)
                        for i in range(SUBLANES)]
        pm = jnp.concatenate(pm_rows, axis=0)
        yd = jnp.dot(pm.astype(BF16), pw_ref[...], preferred_element_type=F32) * ps_ref[...]
        for blk in range(n_blk):
            for i in range(SUBLANES):
                k0 = blk * RG_BLOCK + i * SUBLANES
                for half in range(2):
                    ds[half, pl.ds(s + blk * RG_BLOCK + i, SUBLANES, stride=SUBLANES), :] = (
                        yd[k0:k0 + SUBLANES, half * LANES:(half + 1) * LANES])
        conv = jnp.concatenate([cs[0, pl.ds(s, cp_rows), :], cs[1, pl.ds(s, cp_rows), :]], axis=-1)
        yb_ref[pl.ds(s, cp_rows), :] = (gb_ref[pl.ds(s, cp_rows), :].astype(F32) * conv).astype(BF16)
        yd_ref[pl.ds(s, cp_rows), :] = jnp.concatenate(
            [ds[0, pl.ds(s, cp_rows), :], ds[1, pl.ds(s, cp_rows), :]], axis=-1).astype(BF16)
        return carry

    lax.fori_loop(0, n_chunks, body, 0)


def _seqmix_kernel(*refs, t_len, n_seq):
    seq_in, weights, seq_out, scratch = refs[:7], refs[7:15], refs[15:19], refs[19:]
    ccw_ref, ccb_ref, wg_ref, bg_ref, lam_ref, bcw_ref, pw_ref, ps_ref = weights
    for sq in range(n_seq):
        xr_ref, gr_ref, gb_ref, gc_ref, xb_ref, xp_ref, h0_ref = [r.at[sq] for r in seq_in]
        yb_ref, yc_ref, yd_ref, last_ref = [r.at[sq] for r in seq_out]
        xs, gs, ys, xc_s, hf_s, us, ps_s, ds = [r.at[sq] for r in scratch]
        _rg_kernel(xr_ref, gr_ref, h0_ref, ccw_ref, ccb_ref, wg_ref, bg_ref, lam_ref,
                   yc_ref, last_ref, xs, gs, ys, xc_s, hf_s, t_len=t_len)
        _convpool_kernel(gb_ref, gc_ref, xb_ref, xp_ref, bcw_ref, pw_ref, ps_ref,
                         yb_ref, yd_ref, us, ps_s, ys, ds, t_len=t_len)


SEQMIX_ROWS = 1024


def _seqmix(z, h0, layer, conv_c_w, conv_c_b, wg, bg, lam, conv_b_w, pw, ps):
    b, t, _ = z.shape
    n_seq = max(1, SEQMIX_ROWS // t)
    col = lambda k: pl.BlockSpec((n_seq, t, D_GROUP), lambda i: (i, 0, k))
    out = pl.BlockSpec((n_seq, t, D_GROUP), lambda i: (i, 0, 0))
    slab = lambda rows: pltpu.VMEM((n_seq, 2, rows, LANES), F32)
    flat = pltpu.VMEM((n_seq, t, D_GROUP), F32)
    if h0.ndim == 4:
        h0_spec = pl.BlockSpec((n_seq, None, 2, D_GROUP), lambda i: (i, layer, 0, 0))
    else:
        h0_spec = pl.BlockSpec((n_seq, 2, D_GROUP), lambda i: (i, 0, 0))
    yb, yc, yd, last = pl.pallas_call(
        functools.partial(_seqmix_kernel, t_len=t, n_seq=n_seq),
        grid=(b // n_seq,),
        in_specs=[col(6), col(7), col(3), col(4), col(5), col(8), h0_spec,
                  _resident((None, 4, D_GROUP), lambda i: (layer, 0, 0)),
                  _resident((None, 1, D_GROUP), lambda i: (layer, 0, 0)),
                  _resident((None, D_GROUP, 4 * D_GROUP), lambda i: (layer, 0, 0)),
                  _resident((None, 1, 4 * D_GROUP), lambda i: (layer, 0, 0)),
                  _resident((None, 2, D_GROUP), lambda i: (layer, 0, 0)),
                  _resident((None, 3, D_GROUP), lambda i: (layer, 0, 0)),
                  _resident((None, D_GROUP, D_GROUP), lambda i: (layer, 0, 0)),
                  _resident((None, 1, D_GROUP), lambda i: (layer, 0, 0))],
        out_specs=[out, out, out, pl.BlockSpec((n_seq, 2, D_GROUP), lambda i: (i, 0, 0))],
        out_shape=[jax.ShapeDtypeStruct((b, t, D_GROUP), BF16)] * 3
        + [jax.ShapeDtypeStruct((b, 2, D_GROUP), F32)],
        scratch_shapes=[slab(t + 2 * HALO), slab(t), slab(t), flat, flat,
                        slab(t + 2 * HALO), slab(t + 2 * HALO), slab(t)],
        compiler_params=_params("arbitrary"),
        name="seqmix",
    )(z, z, z, z, z, z, h0, conv_c_w, conv_c_b, wg, bg, lam, conv_b_w, pw, ps)
    return (yb, yc, yd), last


FF_CHUNK = 1024


def _mlp_kernel(x_ref, ya_ref, yb_ref, yc_ref, yd_ref, mod_ref, nw_ref, wo_ref, w1_ref, w2_ref,
                fw_ref, o_ref, *, final):
    x = x_ref[...]
    y = jnp.zeros(x.shape, F32)
    for i, r in enumerate((ya_ref, yb_ref, yc_ref, yd_ref)):
        y = y + jnp.dot(r[...], wo_ref[i * D_GROUP:(i + 1) * D_GROUP, :],
                        preferred_element_type=F32)
    gate1 = mod_ref[:, 2 * D_MODEL:3 * D_MODEL]
    shift2 = mod_ref[:, 3 * D_MODEL:4 * D_MODEL]
    scale2 = mod_ref[:, 4 * D_MODEL:5 * D_MODEL]
    gate2 = mod_ref[:, 5 * D_MODEL:6 * D_MODEL]
    x1 = x + gate1 * y
    hn = (_rms(x1, nw_ref[...]) * (1.0 + scale2) + shift2).astype(BF16)
    acc = jnp.zeros(x.shape, F32)
    for f in range(D_FF // FF_CHUNK):
        u = jnp.dot(hn, w1_ref[:, f * FF_CHUNK:(f + 1) * FF_CHUNK], preferred_element_type=F32)
        u = jnp.square(jnp.maximum(u, 0.0)).astype(BF16)
        acc = acc + jnp.dot(u, w2_ref[f * FF_CHUNK:(f + 1) * FF_CHUNK, :], preferred_element_type=F32)
    x2 = x1 + gate2 * acc
    if final:
        x2 = _rms(x2, fw_ref[...])
    o_ref[...] = x2


def _mlp(x, ys, mod, layer, mod_row, norm_w, wo, w1, w2, fw, final, tt, shared_mod=False):
    b, t, _ = x.shape
    if shared_mod:
        x, ys = x.reshape(1, b * t, D_MODEL), [y.reshape(1, b * t, D_GROUP) for y in ys]
    nb, nt, _ = x.shape
    ytile = pl.BlockSpec((None, tt, D_GROUP), lambda i, j: (i, j, 0))
    out = pl.pallas_call(
        functools.partial(_mlp_kernel, final=final),
        grid=(nb, nt // tt),
        in_specs=[
            pl.BlockSpec((None, tt, D_MODEL), lambda i, j: (i, j, 0)),
            ytile, ytile, ytile, ytile,
            pl.BlockSpec((None, None, 1, N_MOD * D_MODEL), lambda i, j: (layer, mod_row(i), 0, 0)),
            _resident((None, None, 1, D_MODEL), lambda i, j: (layer, 1, 0, 0)),
            _resident((None, D_MODEL, D_MODEL), lambda i, j: (layer, 0, 0)),
            _resident((None, D_MODEL, D_FF), lambda i, j: (layer, 0, 0)),
            _resident((None, D_FF, D_MODEL), lambda i, j: (layer, 0, 0)),
            _resident((1, D_MODEL), lambda i, j: (0, 0)),
        ],
        out_specs=pl.BlockSpec((None, tt, D_MODEL), lambda i, j: (i, j, 0)),
        out_shape=jax.ShapeDtypeStruct((nb, nt, D_MODEL), F32),
        compiler_params=_params("arbitrary", "arbitrary"),
        name="outproj_mlp",
    )(x, *ys, mod, norm_w, wo, w1, w2, fw)
    return out.reshape(b, t, D_MODEL)


def _block_diag(w):
    *lead, n, k, _ = w.shape
    eye = jnp.eye(n, dtype=w.dtype)
    return (eye[:, None, :, None] * w[..., :, :, None, :]).reshape(*lead, n * k, n * k)


def _rope_tables(t_len):
    half = DK // 4
    freqs = ROPE_BASE ** (-np.arange(half, dtype=np.float64) / half)
    lane = np.arange(LANES)
    m = lane % DK
    use_col = m >= DK // 2
    fidx = m % half
    t = np.arange(t_len)
    pos = np.where(use_col[None, :], (t % GRID_W)[:, None], (t // GRID_W)[:, None])
    ang = pos * freqs[fidx][None, :]
    sign = np.where((lane % (2 * half)) < half, -1.0, 1.0)
    return (jnp.asarray(np.cos(ang), dtype=F32), jnp.asarray(np.sin(ang) * sign[None, :], dtype=F32))


def kernel(x_prompt, x_sample, cache_k, cache_v, state_rglru, c, c_ctx, w_ada, b_ada, norm_w, w_in, diff_lambda, subln_w, conv_b_w, conv_c_w, conv_c_b, rg_w, rg_b, rg_lambda, pool_w, pool_scale, w_out, w_mlp1, w_mlp2, final_norm_w):
    n_lat = c.shape[0]
    ctx_row = n_lat
    cc = jnp.zeros((MOD_ROWS, D_MODEL), F32).at[:n_lat].set(c).at[ctx_row].set(c_ctx)
    mod = _ada(cc, w_ada, b_ada).reshape(DEPTH, MOD_ROWS, 1, N_MOD * D_MODEL)

    w_in_b = w_in.astype(BF16)
    w_out_b = w_out.astype(BF16)
    w1_b = w_mlp1.astype(BF16)
    w2_b = w_mlp2.astype(BF16)
    norm_w4 = norm_w.reshape(DEPTH, 2, 1, D_MODEL)
    rope_tabs = _rope_tables(x_sample.shape[1])
    gmat = _block_diag(jnp.full((N_ATT_HEADS, V_DIM, V_DIM), 1.0 / V_DIM, F32))
    subw = jnp.tile(subln_w, (1, N_ATT_HEADS)).reshape(DEPTH, 1, D_GROUP)
    wg = _block_diag(rg_w).transpose(0, 3, 1, 2, 4).reshape(DEPTH, D_GROUP, 4 * D_GROUP).astype(BF16)
    bg = rg_b.reshape(DEPTH, 1, 4 * D_GROUP)
    pw = _block_diag(pool_w).astype(BF16)
    ps = pool_scale.reshape(DEPTH, 1, D_GROUP)
    cb = conv_c_b.reshape(DEPTH, 1, D_GROUP)
    fw = final_norm_w.reshape(1, D_MODEL)
    zero_state = jnp.zeros((x_prompt.shape[0], 2, D_GROUP), F32)

    xp, xs = x_prompt, x_sample
    kv_out, hs_out = [], []
    for l in range(DEPTH):
        lam_init = 0.8 - 0.6 * math.exp(-0.3 * l)
        final = l == DEPTH - 1

        def mixers(z, ctx, h0):
            ya = _attn(z, ctx, l, diff_lambda, subw, gmat, lam_init)
            (yb, yc, yd), last = _seqmix(z, h0, l, conv_c_w, cb, wg, bg, rg_lambda,
                                         conv_b_w, pw, ps)
            return (ya, yb, yc, yd), last

        ctx_rows = lambda i: ctx_row
        kv_prev = kv_out if final else ()
        zp, k_new, v_new = _inproj(xp, mod, l, ctx_rows, norm_w4, w_in_b, None, INPROJ_TILE, kv_prev)
        ys, last_p = mixers(zp, None, zero_state)
        xp = _mlp(xp, ys, mod, l, ctx_rows, norm_w4, w_out_b, w1_b, w2_b, fw, final, TOKEN_TILE,
                  shared_mod=True)
        kv_out += [k_new, v_new]
        hs_out.append(last_p)

        lat_rows = lambda i: i
        (zs,) = _inproj(xs, mod, l, lat_rows, norm_w4, w_in_b, rope_tabs, INPROJ_TILE)
        ys, _ = mixers(zs, (cache_k, cache_v), state_rglru)
        xs = _mlp(xs, ys, mod, l, lat_rows, norm_w4, w_out_b, w1_b, w2_b, fw, final, TOKEN_TILE)

    return (xp, xs, kv_out[-2], kv_out[-1], jnp.stack(hs_out, axis=1))
```
